```python
import jax
import jax.numpy as jnp
from jax import lax
import numpy as np


D_MODEL = 2048
BATCH = 1
SEQ = 16384
DEPTH = 4

GRID_W = 64
CTX_LEN = 256
HEAD_DIM = 128
CONV_W = D_MODEL // 4
FOURIER_GROUPS = D_MODEL // 512
FOURIER_W = FOURIER_GROUPS * HEAD_DIM
NA_HEADS = D_MODEL // 512
NA_W = NA_HEADS * HEAD_DIM
NA_ROWS = 8
NA_COLS = 16
GQA_Q_HEADS = D_MODEL // 256
GQA_KV_HEADS = GQA_Q_HEADS // 4
GQA_GROUP = GQA_Q_HEADS // GQA_KV_HEADS
GQA_W = GQA_Q_HEADS * HEAD_DIM
GQA_KV_W = GQA_KV_HEADS * HEAD_DIM
Q_BLOCK = 128
N_BRANCH = 4
_SA = 3 * CONV_W
_SF = _SA + FOURIER_W
_SN = _SF + 3 * NA_W
_SQ = _SN + GQA_W
_SK = _SQ + GQA_KV_W
IN_SPLITS = (_SA, _SF, _SN, _SQ, _SK)
N_IN = _SK + GQA_KV_W
D_FF = 11 * D_MODEL // 4
ROPE_THETA = 10000.0
EPS = 1e-6

kernel_name = 'hybrid_parallel_conv_fourier_natten_gqa_dit'


def rms_norm(x, gain):
    xf = x.astype(jnp.float32)
    y = xf * lax.rsqrt(jnp.mean(xf * xf, axis=-1, keepdims=True) + EPS)
    return y.astype(x.dtype) * gain


def modulate(x, shift, scale):
    return x * (1.0 + scale) + shift


def ada_mod(cvec, w, b):
    return jnp.split(jax.nn.silu(cvec) @ w + b, 6, axis=-1)


def dwconv3(x, w):
    xp = jnp.pad(x, ((0, 0), (1, 1), (0, 0)))
    return xp[:, :-2] * w[0] + xp[:, 1:-1] * w[1] + xp[:, 2:] * w[2]


def to_heads(z, n_heads):
    return z.reshape(z.shape[0], z.shape[1], n_heads, HEAD_DIM)


def qk_norm(z, n_heads, gain):
    return rms_norm(to_heads(z, n_heads), gain)


def axial_rope(n_tok, dtype):
    t = jnp.arange(n_tok)
    row = (t // GRID_W).astype(jnp.float32)
    col = (t % GRID_W).astype(jnp.float32)
    n_freq = HEAD_DIM // 4
    inv_freq = ROPE_THETA ** (-jnp.arange(n_freq, dtype=jnp.float32) / n_freq)
    ang = jnp.concatenate([row[:, None] * inv_freq, col[:, None] * inv_freq], axis=-1)
    return jnp.cos(ang).astype(dtype)[None, :, None, :], jnp.sin(ang).astype(dtype)[None, :, None, :]


def apply_rope(x, cos, sin):
    x1, x2 = jnp.split(x, 2, axis=-1)
    return jnp.concatenate([x1 * cos - x2 * sin, x1 * sin + x2 * cos], axis=-1)


def short_conv_mixer(z, w):
    xa, bg, cg = jnp.split(z, 3, axis=-1)
    return bg * dwconv3(cg * xa, w)


def fourier_mixer(z):
    b, t, _ = z.shape
    g = z.reshape(b, t, FOURIER_GROUPS, HEAD_DIM).astype(jnp.float32)
    f = jnp.fft.fftn(g, axes=(1, 3), norm='ortho').real
    return f.reshape(b, t, FOURIER_W).astype(z.dtype)


def dense_attention(q, k, v):
    b, l, hq, dh = q.shape
    hkv = k.shape[2]
    qg = q.reshape(b, l, hkv, hq // hkv, dh)
    s = jnp.einsum('bqhgd,bkhd->bhgqk', qg, k).astype(jnp.float32) * (dh ** -0.5)
    p = jax.nn.softmax(s, axis=-1).astype(v.dtype)
    return jnp.einsum('bhgqk,bkhd->bqhgd', p, v).reshape(b, l, hq * dh)


def gqa_latent_attention(q, k, v, kc, vc):
    b, s_len, hq, dh = q.shape
    k_all = jnp.concatenate([k, kc], axis=1)
    v_all = jnp.concatenate([v, vc], axis=1)
    n_blk = s_len // Q_BLOCK
    qb = q.reshape(b, n_blk, Q_BLOCK, GQA_KV_HEADS, GQA_GROUP, dh).transpose(1, 0, 2, 3, 4, 5)
    scale = dh ** -0.5

    def block(q_blk):
        s = jnp.einsum('bqhgd,bkhd->bhgqk', q_blk, k_all).astype(jnp.float32) * scale
        p = jax.nn.softmax(s, axis=-1).astype(v_all.dtype)
        return jnp.einsum('bhgqk,bkhd->bqhgd', p, v_all)

    o = lax.map(block, qb)
    return o.transpose(1, 0, 2, 3, 4, 5).reshape(b, s_len, hq * dh)


def neighborhood_attention(q, k, v, kc, vc, rpb):
    b, s_len, h, dh = q.shape
    rows = s_len // GRID_W
    kr = min(NA_ROWS, rows)
    qg = q.reshape(b, rows, GRID_W, h, dh)
    kg = k.reshape(b, rows, GRID_W, h, dh)
    vg = v.reshape(b, rows, GRID_W, h, dh)
    col = jnp.arange(GRID_W)
    col_idx = jnp.clip(col - NA_COLS // 2, 0, GRID_W - NA_COLS)[:, None] + jnp.arange(NA_COLS)[None, :]
    dc_idx = col_idx - col[:, None] + NA_COLS - 1
    n_nb = kr * NA_COLS
    scale = dh ** -0.5

    def row_block(r):
        r0 = jnp.clip(r - kr // 2, 0, rows - kr)
        k_nb = lax.dynamic_slice_in_dim(kg, r0, kr, axis=1)[:, :, col_idx]
        v_nb = lax.dynamic_slice_in_dim(vg, r0, kr, axis=1)[:, :, col_idx]
        q_r = lax.dynamic_index_in_dim(qg, r, axis=1, keepdims=False)
        dr_idx = r0 + jnp.arange(kr) - r + NA_ROWS - 1
        bias = rpb[:, dr_idx][:, :, dc_idx].transpose(0, 2, 1, 3)
        s_nb = jnp.einsum('bqhd,brqchd->bhqrc', q_r, k_nb).astype(jnp.float32) * scale + bias.astype(jnp.float32)
        s_ctx = jnp.einsum('bqhd,bkhd->bhqk', q_r, kc).astype(jnp.float32) * scale
        s = jnp.concatenate([s_nb.reshape(b, h, GRID_W, n_nb), s_ctx], axis=-1)
        p = jax.nn.softmax(s, axis=-1).astype(v.dtype)
        p_nb = p[..., :n_nb].reshape(b, h, GRID_W, kr, NA_COLS)
        p_ctx = p[..., n_nb:]
        return (jnp.einsum('bhqrc,brqchd->bqhd', p_nb, v_nb)
                + jnp.einsum('bhqk,bkhd->bqhd', p_ctx, vc))

    o = lax.map(row_block, jnp.arange(rows))
    return o.transpose(1, 0, 2, 3, 4).reshape(b, s_len, h * dh)


def merge_branches(xn, ys, w_outs, w_gate, b_gate, w_o):
    g = jnp.split(jax.nn.sigmoid(xn @ w_gate + b_gate), N_BRANCH, axis=-1)
    merged = (g[0] * (ys[0] @ w_outs[0]) + g[1] * (ys[1] @ w_outs[1])
              + g[2] * (ys[2] @ w_outs[2]) + g[3] * (ys[3] @ w_outs[3]))
    return merged @ w_o


def conv_ffn(xn, w_up, conv_w, w_down):
    u = dwconv3(xn @ w_up, conv_w)
    a, gate = jnp.split(u, 2, axis=-1)
    return (jax.nn.silu(a) * gate) @ w_down


def setup_inputs(seed: int = 0) -> dict:
    key = jax.random.key(seed)
    keys = jax.random.split(key, 32)
    counter = [0]

    def nrm(shape, scale):
        k = keys[counter[0]]
        counter[0] += 1
        return jax.random.normal(k, shape, jnp.float32) * scale

    def gain(shape):
        return 1.0 + nrm(shape, 0.02)

    L = DEPTH
    D = D_MODEL
    return {
        'x': nrm((BATCH, SEQ, D), 1.0),
        'c': nrm((BATCH, D), 1.0),
        'ctx': nrm((BATCH, CTX_LEN, D), 1.0),
        'c_ctx': nrm((D,), 1.0),
        'w_ada': nrm((L, D, 6 * D), 0.5 * D ** -0.5),
        'b_ada': nrm((L, 6 * D), 0.01),
        'norm1': gain((L, D)),
        'w_in': nrm((L, D, N_IN), D ** -0.5),
        'conv_w': nrm((L, 3, CONV_W), 3 ** -0.5),
        'na_q_gain': gain((L, HEAD_DIM)),
        'na_k_gain': gain((L, HEAD_DIM)),
        'na_rpb': nrm((L, NA_HEADS, 2 * NA_ROWS - 1, 2 * NA_COLS - 1), 0.1),
        'gqa_q_gain': gain((L, HEAD_DIM)),
        'gqa_k_gain': gain((L, HEAD_DIM)),
        'w_conv_out': nrm((L, CONV_W, D), CONV_W ** -0.5),
        'w_fourier_out': nrm((L, FOURIER_W, D), FOURIER_W ** -0.5),
        'w_na_out': nrm((L, NA_W, D), NA_W ** -0.5),
        'w_gqa_out': nrm((L, GQA_W, D), GQA_W ** -0.5),
        'w_gate': nrm((L, D, N_BRANCH * D), D ** -0.5),
        'b_gate': nrm((L, N_BRANCH * D), 0.01),
        'w_o': nrm((L, D, D), D ** -0.5),
        'norm2': gain((L, D)),
        'w_up': nrm((L, D, 2 * D_FF), D ** -0.5),
        'ffn_conv_w': nrm((L, 3, 2 * D_FF), 3 ** -0.5),
        'w_down': nrm((L, D_FF, D), D_FF ** -0.5),
    }


def reference(x, c, ctx, c_ctx, w_ada, b_ada, norm1, w_in, conv_w, na_q_gain, na_k_gain, na_rpb,
              gqa_q_gain, gqa_k_gain, w_conv_out, w_fourier_out, w_na_out, w_gqa_out, w_gate, b_gate,
              w_o, norm2, w_up, ffn_conv_w, w_down):
    h = x
    hc = ctx
    cos, sin = axial_rope(x.shape[1], x.dtype)
    c_lat = c[:, None, :]
    for i in range(DEPTH):
        last = i == DEPTH - 1
        sh1, sc1, g1, sh2, sc2, g2 = ada_mod(c_lat, w_ada[i], b_ada[i])
        csh1, csc1, cg1, csh2, csc2, cg2 = ada_mod(c_ctx, w_ada[i], b_ada[i])
        xn = modulate(rms_norm(h, norm1[i]), sh1, sc1)
        xcn = modulate(rms_norm(hc, norm1[i]), csh1, csc1)
        za, zf, zn, zq, zk, zv = jnp.split(xn @ w_in[i], IN_SPLITS, axis=-1)
        cza, czf, czn, czq, czk, czv = jnp.split(xcn @ w_in[i], IN_SPLITS, axis=-1)
        nq_raw, nk_raw, nv_raw = jnp.split(zn, 3, axis=-1)
        cnq_raw, cnk_raw, cnv_raw = jnp.split(czn, 3, axis=-1)
        nq = qk_norm(nq_raw, NA_HEADS, na_q_gain[i])
        nk = qk_norm(nk_raw, NA_HEADS, na_k_gain[i])
        nv = to_heads(nv_raw, NA_HEADS)
        cnk = qk_norm(cnk_raw, NA_HEADS, na_k_gain[i])
        cnv = to_heads(cnv_raw, NA_HEADS)
        gq = apply_rope(qk_norm(zq, GQA_Q_HEADS, gqa_q_gain[i]), cos, sin)
        gk = apply_rope(qk_norm(zk, GQA_KV_HEADS, gqa_k_gain[i]), cos, sin)
        gv = to_heads(zv, GQA_KV_HEADS)
        ck = qk_norm(czk, GQA_KV_HEADS, gqa_k_gain[i])
        cv = to_heads(czv, GQA_KV_HEADS)
        merge_w = (w_conv_out[i], w_fourier_out[i], w_na_out[i], w_gqa_out[i])
        ys = (short_conv_mixer(za, conv_w[i]),
              fourier_mixer(zf),
              neighborhood_attention(nq, nk, nv, cnk, cnv, na_rpb[i]),
              gqa_latent_attention(gq, gk, gv, ck, cv))
        h = h + g1 * merge_branches(xn, ys, merge_w, w_gate[i], b_gate[i], w_o[i])
        h = h + g2 * conv_ffn(modulate(rms_norm(h, norm2[i]), sh2, sc2), w_up[i], ffn_conv_w[i], w_down[i])
        if not last:
            cnq = qk_norm(cnq_raw, NA_HEADS, na_q_gain[i])
            cq = qk_norm(czq, GQA_Q_HEADS, gqa_q_gain[i])
            cys = (short_conv_mixer(cza, conv_w[i]),
                   fourier_mixer(czf),
                   dense_attention(cnq, cnk, cnv),
                   dense_attention(cq, ck, cv))
            hc = hc + cg1 * merge_branches(xcn, cys, merge_w, w_gate[i], b_gate[i], w_o[i])
            hc = hc + cg2 * conv_ffn(modulate(rms_norm(hc, norm2[i]), csh2, csc2), w_up[i], ffn_conv_w[i], w_down[i])
    return h
```

```python
import functools
import math

import jax
import jax.numpy as jnp
import numpy as np
from jax import lax
from jax.experimental import pallas as pl
from jax.experimental.pallas import tpu as pltpu

D_MODEL = 2048
SEQ = 16384
DEPTH = 4
GRID_W = 64
CTX_LEN = 256
HEAD_DIM = 128
CONV_W = 512
FOURIER_GROUPS = 4
FOURIER_W = 512
NA_HEADS = 4
NA_W = 512
NA_ROWS = 8
NA_COLS = 16
GQA_Q_HEADS = 8
GQA_KV_HEADS = 2
GQA_GROUP = 4
GQA_W = 1024
GQA_KV_W = 256
N_BRANCH = 4
N_IN = 5120
D_FF = 5632
ROPE_THETA = 10000.0
EPS = 1e-6

_C_XA, _C_BG, _C_CG = 0, 512, 1024
_C_ZF = 1536
_C_NQ, _C_NK, _C_NV = 2048, 2560, 3072
_C_ZQ, _C_ZK, _C_ZV = 3584, 4608, 4864
_C_GATE = N_IN

BF16 = jnp.bfloat16
F32 = jnp.float32
V7X_VMEM_LIMIT = 52 * 1024 * 1024
NEG_BIG = -1e30
HALO = 16


def _cp(*sem):
    return pltpu.CompilerParams(dimension_semantics=sem, vmem_limit_bytes=V7X_VMEM_LIMIT)


def _ada_kernel(c_ref, w_ref, b_ref, o_ref):
    c = c_ref[...]
    s = c * (1.0 / (1.0 + jnp.exp(-c)))
    o_ref[...] = jnp.dot(s.astype(BF16), w_ref[...].astype(BF16),
                         preferred_element_type=F32) + b_ref[...]


def ada_mod(cvec, w, b):
    n = w.shape[1]
    tn = 1024
    return pl.pallas_call(
        _ada_kernel,
        out_shape=jax.ShapeDtypeStruct((8, n), F32),
        grid=(n // tn,),
        in_specs=[pl.BlockSpec((8, D_MODEL), lambda j: (0, 0)),
                  pl.BlockSpec((D_MODEL, tn), lambda j: (0, j)),
                  pl.BlockSpec((1, tn), lambda j: (0, j))],
        out_specs=pl.BlockSpec((8, tn), lambda j: (0, j)),
        compiler_params=_cp("parallel"),
        name="ada_mod",
    )(cvec, w, b)


def _nm_mm_kernel(h_ref, g_ref, sh_ref, sc_ref, w_ref, b_ref, o_ref, xn_ref, *, n_plain, n_total):
    j = pl.program_id(1)

    @pl.when(j == 0)
    def _():
        x = h_ref[...]
        ms = jnp.mean(x * x, axis=-1, keepdims=True)
        y = x * lax.rsqrt(ms + EPS) * g_ref[...]
        xn_ref[...] = (y * (1.0 + sc_ref[...]) + sh_ref[...]).astype(BF16)

    acc = jnp.dot(xn_ref[...], w_ref[...], preferred_element_type=F32)
    if n_plain == n_total:
        o_ref[...] = acc.astype(o_ref.dtype)
    else:
        @pl.when(j < n_plain)
        def _():
            o_ref[...] = acc.astype(o_ref.dtype)

        @pl.when(j >= n_plain)
        def _():
            t = acc + b_ref[...]
            o_ref[...] = (1.0 / (1.0 + jnp.exp(-t))).astype(o_ref.dtype)


def nm_matmul(h, gain, shift, scale, w, bias, n_plain_cols, name):
    m, d = h.shape
    n = w.shape[1]
    tm = min(m, 1024)
    tn = 1024
    kern = functools.partial(_nm_mm_kernel, n_plain=n_plain_cols // tn, n_total=n // tn)
    vec = pl.BlockSpec((1, d), lambda i, j: (0, 0))
    return pl.pallas_call(
        kern,
        out_shape=jax.ShapeDtypeStruct((m, n), BF16),
        grid=(m // tm, n // tn),
        in_specs=[pl.BlockSpec((tm, d), lambda i, j: (i, 0)), vec, vec, vec,
                  pl.BlockSpec((d, tn), lambda i, j: (0, j)),
                  pl.BlockSpec((1, tn), lambda i, j: (0, j))],
        out_specs=pl.BlockSpec((tm, tn), lambda i, j: (i, j)),
        scratch_shapes=[pltpu.VMEM((tm, d), BF16)],
        compiler_params=_cp("parallel", "arbitrary"),
        name=name,
    )(h, gain, shift, scale, w, bias)


def _mm_res_kernel(x_ref, w_ref, h_ref, g_ref, o_ref):
    acc = jnp.dot(x_ref[...], w_ref[...], preferred_element_type=F32)
    o_ref[...] = h_ref[...] + g_ref[...] * acc


def mm_residual(x, w, h, gate, name):
    m, k = x.shape
    n = w.shape[1]
    tm = min(m, 1024)
    tn = 512
    return pl.pallas_call(
        _mm_res_kernel,
        out_shape=jax.ShapeDtypeStruct((m, n), F32),
        grid=(m // tm, n // tn),
        in_specs=[pl.BlockSpec((tm, k), lambda i, j: (i, 0)),
                  pl.BlockSpec((k, tn), lambda i, j: (0, j)),
                  pl.BlockSpec((tm, tn), lambda i, j: (i, j)),
                  pl.BlockSpec((1, tn), lambda i, j: (0, j))],
        out_specs=pl.BlockSpec((tm, tn), lambda i, j: (i, j)),
        input_output_aliases={2: 0},
        compiler_params=_cp("parallel", "arbitrary"),
        name=name,
    )(x, w, h, gate)


def _merge_kernel(yc, yf, yn, yg, wc, wf, wn, wg, g0, g1, g2, g3, o_ref):
    def term(y, w, g):
        return g[...].astype(F32) * jnp.dot(y[...], w[...], preferred_element_type=F32)

    o_ref[...] = (term(yc, wc, g0) + term(yf, wf, g1) + term(yn, wn, g2)
                  + term(yg, wg, g3)).astype(o_ref.dtype)


def merge_branches(ys, ws, zg):
    m = zg.shape[0]
    tm = min(m, 1024)
    tn = 512
    y_specs = [pl.BlockSpec((tm, y.shape[1]), lambda i, j: (i, 0)) for y in ys]
    w_specs = [pl.BlockSpec((w.shape[0], tn), lambda i, j: (0, j)) for w in ws]
    g_specs = [pl.BlockSpec((tm, tn), functools.partial(
        lambda i, j, b: (i, (_C_GATE + b * D_MODEL) // tn + j), b=b)) for b in range(N_BRANCH)]
    return pl.pallas_call(
        _merge_kernel,
        out_shape=jax.ShapeDtypeStruct((m, D_MODEL), BF16),
        grid=(m // tm, D_MODEL // tn),
        in_specs=y_specs + w_specs + g_specs,
        out_specs=pl.BlockSpec((tm, tn), lambda i, j: (i, j)),
        compiler_params=_cp("parallel", "arbitrary"),
        name="merge",
    )(*ys, *ws, zg, zg, zg, zg)


def _conv3(p, prev_row, next_row, w_ref):
    tm = p.shape[0]
    row = lax.broadcasted_iota(jnp.int32, p.shape, 0)
    up = jnp.where(row == 0, prev_row, pltpu.roll(p, 1, 0))
    dn = jnp.where(row == tm - 1, next_row, pltpu.roll(p, tm - 1, 0))
    return up * w_ref[0:1, :] + p * w_ref[1:2, :] + dn * w_ref[2:3, :]


def _halo_specs(tm, tc, m, col_fn):
    nb = m // HALO
    per = tm // HALO
    main = pl.BlockSpec((tm, tc), lambda i, j: (i, col_fn(j)))
    prev = pl.BlockSpec((HALO, tc), lambda i, j: (jnp.maximum(i * per - 1, 0), col_fn(j)))
    nxt = pl.BlockSpec((HALO, tc), lambda i, j: (jnp.minimum((i + 1) * per, nb - 1), col_fn(j)))
    return main, prev, nxt


def _convmix_kernel(xa, xap, xan, cg, cgp, cgn, bg, w_ref, o_ref):
    i = pl.program_id(0)
    last = pl.num_programs(0) - 1
    p = cg[...].astype(F32) * xa[...].astype(F32)
    pp = cgp[...].astype(F32)[HALO - 1:HALO, :] * xap[...].astype(F32)[HALO - 1:HALO, :]
    pn = cgn[...].astype(F32)[0:1, :] * xan[...].astype(F32)[0:1, :]
    pp = jnp.where(i > 0, pp, 0.0)
    pn = jnp.where(i < last, pn, 0.0)
    o_ref[...] = (bg[...].astype(F32) * _conv3(p, pp, pn, w_ref)).astype(o_ref.dtype)


def conv_mixer(zg, w):
    m = zg.shape[0]
    tm = min(m, 1024)
    tc = CONV_W
    xa = _halo_specs(tm, tc, m, lambda j: _C_XA // tc)
    cg = _halo_specs(tm, tc, m, lambda j: _C_CG // tc)
    bg = pl.BlockSpec((tm, tc), lambda i, j: (i, _C_BG // tc))
    return pl.pallas_call(
        _convmix_kernel,
        out_shape=jax.ShapeDtypeStruct((m, CONV_W), BF16),
        grid=(m // tm, 1),
        in_specs=[*xa, *cg, bg, pl.BlockSpec((3, tc), lambda i, j: (0, 0))],
        out_specs=pl.BlockSpec((tm, tc), lambda i, j: (i, 0)),
        compiler_params=_cp("parallel", "arbitrary"),
        name="conv_mixer",
    )(zg, zg, zg, zg, zg, zg, zg, w)


def _ffn_act_kernel(a, ap, an, g, gp, gn, wa, wg, o_ref):
    i = pl.program_id(0)
    last = pl.num_programs(0) - 1

    def conv(x, xp, xn, w):
        pp = jnp.where(i > 0, xp[...].astype(F32)[HALO - 1:HALO, :], 0.0)
        pn = jnp.where(i < last, xn[...].astype(F32)[0:1, :], 0.0)
        return _conv3(x[...].astype(F32), pp, pn, w)

    ca = conv(a, ap, an, wa)
    cgt = conv(g, gp, gn, wg)
    o_ref[...] = (ca * (1.0 / (1.0 + jnp.exp(-ca))) * cgt).astype(o_ref.dtype)


def ffn_act(u, w):
    m = u.shape[0]
    tm = min(m, 1024)
    tc = 512
    nj = D_FF // tc
    a = _halo_specs(tm, tc, m, lambda j: j)
    g = _halo_specs(tm, tc, m, lambda j: nj + j)
    return pl.pallas_call(
        _ffn_act_kernel,
        out_shape=jax.ShapeDtypeStruct((m, D_FF), BF16),
        grid=(m // tm, nj),
        in_specs=[*a, *g, pl.BlockSpec((3, tc), lambda i, j: (0, j)),
                  pl.BlockSpec((3, tc), lambda i, j: (0, nj + j))],
        out_specs=pl.BlockSpec((tm, tc), lambda i, j: (i, j)),
        compiler_params=_cp("parallel", "arbitrary"),
        name="ffn_act",
    )(u, u, u, u, u, u, w, w)


def _head_prep_kernel(*refs, hb, use_norm, use_rope, scale, layout):
    it = iter(refs)
    z_ref = next(it)
    gain_ref = next(it) if use_norm else None
    cos_ref = next(it) if use_rope else None
    sin_ref = next(it) if use_rope else None
    o_ref = next(it)
    for h in range(hb):
        x = z_ref[:, h * HEAD_DIM:(h + 1) * HEAD_DIM].astype(F32)
        if use_norm:
            ms = jnp.mean(x * x, axis=-1, keepdims=True)
            x = x * lax.rsqrt(ms + EPS) * gain_ref[...]
        if use_rope:
            x = x * cos_ref[...] + pltpu.roll(x, HEAD_DIM // 2, 1) * sin_ref[...]
        if scale != 1.0:
            x = x * scale
        if layout == "rows":
            o_ref[:, h * HEAD_DIM:(h + 1) * HEAD_DIM] = x.astype(o_ref.dtype)
        elif layout == "heads_rows":
            o_ref[h] = x.astype(o_ref.dtype)
        else:
            o_ref[h] = x.T.astype(o_ref.dtype)


def head_prep(zg, col0, n_heads, gain=None, rope=None, scale=1.0, layout="rows"):
    m = zg.shape[0]
    tm = min(m, 512)
    hb = 2
    wb = hb * HEAD_DIM
    kern = functools.partial(_head_prep_kernel, hb=hb, use_norm=gain is not None,
                             use_rope=rope is not None, scale=scale, layout=layout)
    in_specs = [pl.BlockSpec((tm, wb), lambda i, j: (i, col0 // wb + j))]
    args = [zg]
    if gain is not None:
        in_specs.append(pl.BlockSpec((1, HEAD_DIM), lambda i, j: (0, 0)))
        args.append(gain)
    if rope is not None:
        in_specs += [pl.BlockSpec((tm, HEAD_DIM), lambda i, j: (i, 0))] * 2
        args += list(rope)
    if layout == "rows":
        out_shape = jax.ShapeDtypeStruct((m, n_heads * HEAD_DIM), BF16)
        out_spec = pl.BlockSpec((tm, wb), lambda i, j: (i, j))
    elif layout == "heads_rows":
        out_shape = jax.ShapeDtypeStruct((n_heads, m, HEAD_DIM), BF16)
        out_spec = pl.BlockSpec((hb, tm, HEAD_DIM), lambda i, j: (j, i, 0))
    else:
        out_shape = jax.ShapeDtypeStruct((n_heads, HEAD_DIM, m), BF16)
        out_spec = pl.BlockSpec((hb, HEAD_DIM, tm), lambda i, j: (j, 0, i))
    return pl.pallas_call(
        kern, out_shape=out_shape, grid=(m // tm, n_heads // hb),
        in_specs=in_specs, out_specs=out_spec,
        compiler_params=_cp("parallel", "arbitrary"),
        name="head_prep_" + layout,
    )(*args)


def _flash_kernel(qT_ref, k_ref, vT_ref, o_ref, *, group, tk, n_keys):
    tq = qT_ref.shape[2]
    n_full = n_keys // tk
    rem = n_keys - n_full * tk

    for h in range(group):
        qT = qT_ref[h]

        def step(start, size, carry):
            m, l, acc = carry
            kblk = k_ref[0, pl.ds(start, size), :]
            s = jnp.dot(kblk, qT, preferred_element_type=F32)
            m_new = jnp.maximum(m, jnp.max(s, axis=0, keepdims=True))
            alpha = jnp.exp(m - m_new)
            p = jnp.exp(s - m_new)
            l = alpha * l + jnp.sum(p, axis=0, keepdims=True)
            vblk = vT_ref[0, :, pl.ds(start, size)]
            acc = alpha * acc + jnp.dot(vblk, p.astype(BF16), preferred_element_type=F32)
            return m_new, l, acc

        carry = (jnp.full((1, tq), NEG_BIG, F32), jnp.zeros((1, tq), F32),
                 jnp.zeros((HEAD_DIM, tq), F32))
        if n_full > 0:
            carry = lax.fori_loop(
                0, n_full, lambda kb, c: step(pl.multiple_of(kb * tk, tk), tk, c), carry)
        if rem > 0:
            carry = step(n_full * tk, rem, carry)
        _, l, acc = carry
        o = (acc * (1.0 / l)).T
        o_ref[:, h * HEAD_DIM:(h + 1) * HEAD_DIM] = o.astype(o_ref.dtype)


def flash_attention(qT, k, vT, tq):
    hq, _, mq = qT.shape
    hkv, nk, _ = k.shape
    group = hq // hkv
    kern = functools.partial(_flash_kernel, group=group, tk=512, n_keys=nk)
    return pl.pallas_call(
        kern,
        out_shape=jax.ShapeDtypeStruct((mq, hq * HEAD_DIM), BF16),
        grid=(hkv, mq // tq),
        in_specs=[pl.BlockSpec((group, HEAD_DIM, tq), lambda g, i: (g, 0, i)),
                  pl.BlockSpec((1, nk, HEAD_DIM), lambda g, i: (g, 0, 0)),
                  pl.BlockSpec((1, HEAD_DIM, nk), lambda g, i: (g, 0, 0))],
        out_specs=pl.BlockSpec((tq, group * HEAD_DIM), lambda g, i: (i, g)),
        compiler_params=_cp("parallel", "arbitrary"),
        name="flash_attention",
    )(qT, k, vT)


NA_RB = 8
NA_TOK = NA_RB * GRID_W
NA_WIN = NA_ROWS * GRID_W


def _na_kernel(q_ref, kp, kc, kn, vp, vc, vn, ck_ref, cv_ref, bias_ref, o_ref, kbuf, vbuf):
    b = pl.program_id(0)
    rows = pl.num_programs(0) * NA_RB
    kbuf[0:NA_TOK, :] = kp[...]
    kbuf[NA_TOK:2 * NA_TOK, :] = kc[...]
    kbuf[2 * NA_TOK:3 * NA_TOK, :] = kn[...]
    vbuf[0:NA_TOK, :] = vp[...]
    vbuf[NA_TOK:2 * NA_TOK, :] = vc[...]
    vbuf[2 * NA_TOK:3 * NA_TOK, :] = vn[...]
    nt = (((1,), (1,)), ((), ()))

    def row_body(rho, _):
        r = b * NA_RB + rho
        r0 = jnp.clip(r - NA_ROWS // 2, 0, rows - NA_ROWS)
        delta = r - r0
        off = pl.multiple_of((r0 - b * NA_RB + NA_RB) * GRID_W, GRID_W)
        qoff = pl.multiple_of(rho * GRID_W, GRID_W)
        q = q_ref[pl.ds(qoff, GRID_W), :]
        kw = kbuf[pl.ds(off, NA_WIN), :]
        vw = vbuf[pl.ds(off, NA_WIN), :]
        for h in range(NA_HEADS):
            sl = slice(h * HEAD_DIM, (h + 1) * HEAD_DIM)
            qh = q[:, sl]
            s = lax.dot_general(qh, kw[:, sl], nt, preferred_element_type=F32) + bias_ref[delta, h]
            sc = lax.dot_general(qh, ck_ref[:, sl], nt, preferred_element_type=F32)
            m = jnp.maximum(jnp.max(s, axis=-1, keepdims=True), jnp.max(sc, axis=-1, keepdims=True))
            p = jnp.exp(s - m)
            pc = jnp.exp(sc - m)
            l = jnp.sum(p, axis=-1, keepdims=True) + jnp.sum(pc, axis=-1, keepdims=True)
            o = (jnp.dot(p.astype(BF16), vw[:, sl], preferred_element_type=F32)
                 + jnp.dot(pc.astype(BF16), cv_ref[:, sl], preferred_element_type=F32))
            o_ref[pl.ds(qoff, GRID_W), sl] = (o * (1.0 / l)).astype(o_ref.dtype)
        return 0

    lax.fori_loop(0, NA_RB, row_body, 0)


def neighborhood_attention(qn, kn, zg, ckn, czg, bias):
    m = qn.shape[0]
    nb = m // NA_TOK
    vcol = _C_NV // NA_W
    blk = (NA_TOK, NA_W)
    return pl.pallas_call(
        _na_kernel,
        out_shape=jax.ShapeDtypeStruct((m, NA_W), BF16),
        grid=(nb,),
        in_specs=[pl.BlockSpec(blk, lambda b: (b, 0)),
                  pl.BlockSpec(blk, lambda b: (jnp.maximum(b - 1, 0), 0)),
                  pl.BlockSpec(blk, lambda b: (b, 0)),
                  pl.BlockSpec(blk, lambda b: (jnp.minimum(b + 1, nb - 1), 0)),
                  pl.BlockSpec(blk, lambda b: (jnp.maximum(b - 1, 0), vcol)),
                  pl.BlockSpec(blk, lambda b: (b, vcol)),
                  pl.BlockSpec(blk, lambda b: (jnp.minimum(b + 1, nb - 1), vcol)),
                  pl.BlockSpec((CTX_LEN, NA_W), lambda b: (0, 0)),
                  pl.BlockSpec((CTX_LEN, NA_W), lambda b: (0, vcol)),
                  pl.BlockSpec((NA_ROWS, NA_HEADS, GRID_W, NA_WIN), lambda b: (0, 0, 0, 0))],
        out_specs=pl.BlockSpec(blk, lambda b: (b, 0)),
        scratch_shapes=[pltpu.VMEM((3 * NA_TOK, NA_W), BF16), pltpu.VMEM((3 * NA_TOK, NA_W), BF16)],
        compiler_params=_cp("arbitrary"),
        name="neighborhood_attention",
    )(qn, kn, kn, kn, zg, zg, zg, ckn, czg, bias)


def na_bias_table(rpb):
    col = np.arange(GRID_W)
    c0 = np.clip(col - NA_COLS // 2, 0, GRID_W - NA_COLS)
    kc = np.arange(GRID_W)
    inside = (kc[None, :] >= c0[:, None]) & (kc[None, :] < c0[:, None] + NA_COLS)
    dc = np.clip(kc[None, :] - col[:, None] + NA_COLS - 1, 0, 2 * NA_COLS - 2)
    delta = np.arange(NA_ROWS)
    dr = np.arange(NA_ROWS)[None, :] - delta[:, None] + NA_ROWS - 1
    t = rpb[:, dr[:, :, None, None], dc[None, None, :, :]]
    t = jnp.where(inside[None, None, None, :, :], t, NEG_BIG)
    t = t.transpose(1, 0, 3, 2, 4)
    return t.reshape(NA_ROWS, NA_HEADS, GRID_W, NA_WIN).astype(F32)


def _dft_mats(n):
    a = 2.0 * np.pi * np.outer(np.arange(n), np.arange(n)) / n
    return np.cos(a), np.sin(a)


def _chan_dft_kernel(z_ref, w_ref, o_ref):
    for g in range(FOURIER_GROUPS):
        x = z_ref[:, g * HEAD_DIM:(g + 1) * HEAD_DIM]
        r = jnp.dot(x, w_ref[...], preferred_element_type=F32)
        o_ref[0, :, g * HEAD_DIM:(g + 1) * HEAD_DIM] = r[:, :HEAD_DIM].astype(o_ref.dtype)
        o_ref[1, :, g * HEAD_DIM:(g + 1) * HEAD_DIM] = r[:, HEAD_DIM:].astype(o_ref.dtype)


def chan_dft(zg, wch):
    m = zg.shape[0]
    tm = min(m, 1024)
    return pl.pallas_call(
        _chan_dft_kernel,
        out_shape=jax.ShapeDtypeStruct((2, m, FOURIER_W), BF16),
        grid=(m // tm,),
        in_specs=[pl.BlockSpec((tm, FOURIER_W), lambda i: (i, _C_ZF // FOURIER_W)),
                  pl.BlockSpec((HEAD_DIM, 2 * HEAD_DIM), lambda i: (0, 0))],
        out_specs=pl.BlockSpec((2, tm, FOURIER_W), lambda i: (0, i, 0)),
        compiler_params=_cp("parallel"),
        name="chan_dft",
    )(zg, wch)


def _left_mm_kernel(l_ref, x_ref, o_ref, *, n_out):
    x = jnp.concatenate([x_ref[0], x_ref[1]], axis=0)
    r = jnp.dot(l_ref[...], x, preferred_element_type=F32)
    if n_out == 1:
        o_ref[...] = r.astype(o_ref.dtype)
    else:
        half = r.shape[0] // 2
        o_ref[0] = r[:half].astype(o_ref.dtype)
        o_ref[1] = r[half:].astype(o_ref.dtype)


def dft_stage1(zc, lmat):
    _, r, n = zc.shape
    tn = 4096
    return pl.pallas_call(
        functools.partial(_left_mm_kernel, n_out=2),
        out_shape=jax.ShapeDtypeStruct((2, r, n), BF16),
        grid=(n // tn,),
        in_specs=[pl.BlockSpec((2 * r, 2 * r), lambda j: (0, 0)),
                  pl.BlockSpec((2, r, tn), lambda j: (0, 0, j))],
        out_specs=pl.BlockSpec((2, r, tn), lambda j: (0, 0, j)),
        compiler_params=_cp("parallel"),
        name="dft_stage1",
    )(lmat, zc)


def _dft_stage3_kernel(l_ref, x_ref, o_ref, *, kb):
    for i in range(kb):
        x = jnp.concatenate([x_ref[0, i], x_ref[1, i]], axis=0)
        o_ref[i] = jnp.dot(l_ref[i], x, preferred_element_type=F32).astype(o_ref.dtype)


def dft_stage3(a, tables):
    _, n1, n2, c = a.shape
    kb = 8
    return pl.pallas_call(
        functools.partial(_dft_stage3_kernel, kb=kb),
        out_shape=jax.ShapeDtypeStruct((n1, n2, c), BF16),
        grid=(n1 // kb,),
        in_specs=[pl.BlockSpec((kb, n2, 2 * n2), lambda j: (j, 0, 0)),
                  pl.BlockSpec((2, kb, n2, c), lambda j: (0, j, 0, 0))],
        out_specs=pl.BlockSpec((kb, n2, c), lambda j: (j, 0, 0)),
        compiler_params=_cp("parallel"),
        name="dft_stage3",
    )(tables, a)


def dft_direct(zc, lmat):
    _, t, c = zc.shape
    return pl.pallas_call(
        functools.partial(_left_mm_kernel, n_out=1),
        out_shape=jax.ShapeDtypeStruct((t, c), BF16),
        grid=(1,),
        in_specs=[pl.BlockSpec((t, 2 * t), lambda j: (0, 0)),
                  pl.BlockSpec((2, t, c), lambda j: (0, 0, 0))],
        out_specs=pl.BlockSpec((t, c), lambda j: (0, 0)),
        compiler_params=_cp("arbitrary"),
        name="dft_direct",
    )(lmat, zc)


def _fourier_tables():
    c128, s128 = _dft_mats(HEAD_DIM)
    wch = np.concatenate([c128, -s128], axis=1)
    n1 = SEQ // HEAD_DIM
    l1 = np.block([[c128, s128], [-s128, c128]])
    k1 = np.arange(n1)[:, None, None]
    k2 = np.arange(n1)[None, :, None]
    t2 = np.arange(n1)[None, None, :]
    ang = 2.0 * np.pi * ((n1 * k2 + k1) * t2 % SEQ) / SEQ
    norm = 1.0 / math.sqrt(SEQ * HEAD_DIM)
    l3 = np.concatenate([np.cos(ang), np.sin(ang)], axis=2) * norm
    cc, sc = _dft_mats(CTX_LEN)
    lc = np.concatenate([cc, sc], axis=1) / math.sqrt(CTX_LEN * HEAD_DIM)
    as_bf = lambda a: jnp.asarray(a, F32).astype(BF16)
    return as_bf(wch), as_bf(l1), as_bf(l3), as_bf(lc)


def fourier_latent(zg, tabs):
    wch, l1, l3, _ = tabs
    n1 = SEQ // HEAD_DIM
    zc = chan_dft(zg, wch)
    a = dft_stage1(zc.reshape(2, n1, n1 * FOURIER_W), l1)
    o3 = dft_stage3(a.reshape(2, n1, n1, FOURIER_W), l3)
    return o3.transpose(1, 0, 2).reshape(SEQ, FOURIER_W)


def fourier_ctx(czg, tabs):
    wch, _, _, lc = tabs
    return dft_direct(chan_dft(czg, wch), lc)


def _rope_tables(n_tok):
    t = jnp.arange(n_tok)
    row = (t // GRID_W).astype(F32)
    col = (t % GRID_W).astype(F32)
    n_freq = HEAD_DIM // 4
    inv_freq = ROPE_THETA ** (-jnp.arange(n_freq, dtype=F32) / n_freq)
    ang = jnp.concatenate([row[:, None] * inv_freq, col[:, None] * inv_freq], axis=-1)
    cos, sin = jnp.cos(ang), jnp.sin(ang)
    return jnp.concatenate([cos, cos], axis=-1), jnp.concatenate([-sin, sin], axis=-1)


def kernel(x, c, ctx, c_ctx, w_ada, b_ada, norm1, w_in, conv_w, na_q_gain, na_k_gain, na_rpb, gqa_q_gain, gqa_k_gain, w_conv_out, w_fourier_out, w_na_out, w_gqa_out, w_gate, b_gate, w_o, norm2, w_up, ffn_conv_w, w_down):
    d = D_MODEL
    h = x[0]
    hc = ctx[0]
    rope = _rope_tables(SEQ)
    ftabs = _fourier_tables()
    qscale = HEAD_DIM ** -0.5
    cvec = jnp.zeros((8, d), F32).at[0].set(c[0]).at[1].set(c_ctx)
    row = lambda v: v.reshape(1, -1)

    for i in range(DEPTH):
        last = i == DEPTH - 1
        w_cat = jnp.concatenate([w_in[i], w_gate[i]], axis=1).astype(BF16)
        b_cat = jnp.concatenate([jnp.zeros((N_IN,), F32), b_gate[i]]).reshape(1, -1)
        merge_w = [w_conv_out[i].astype(BF16), w_fourier_out[i].astype(BF16),
                   w_na_out[i].astype(BF16), w_gqa_out[i].astype(BF16)]
        w_o_b = w_o[i].astype(BF16)
        w_up_b = w_up[i].astype(BF16)
        w_down_b = w_down[i].astype(BF16)
        zero_b = jnp.zeros((1, 2 * D_FF), F32)
        bias_tab = na_bias_table(na_rpb[i])
        nqg, nkg = row(na_q_gain[i]), row(na_k_gain[i])
        gqg, gkg = row(gqa_q_gain[i]), row(gqa_k_gain[i])

        mods = ada_mod(cvec, w_ada[i], row(b_ada[i]))
        lat = [mods[0:1, k * d:(k + 1) * d] for k in range(6)]
        cm = [mods[1:2, k * d:(k + 1) * d] for k in range(6)]

        zg = nm_matmul(h, row(norm1[i]), lat[0], lat[1], w_cat, b_cat, N_IN, "in_proj")
        czg = nm_matmul(hc, row(norm1[i]), cm[0], cm[1], w_cat, b_cat, N_IN, "in_proj_ctx")

        cnk_rows = head_prep(czg, _C_NK, NA_HEADS, gain=nkg, layout="rows")
        ck = head_prep(czg, _C_ZK, GQA_KV_HEADS, gain=gkg, layout="heads_rows")
        cvT = head_prep(czg, _C_ZV, GQA_KV_HEADS, layout="heads_t")

        y_conv = conv_mixer(zg, conv_w[i])
        y_four = fourier_latent(zg, ftabs)
        nq = head_prep(zg, _C_NQ, NA_HEADS, gain=nqg, scale=qscale, layout="rows")
        nk = head_prep(zg, _C_NK, NA_HEADS, gain=nkg, layout="rows")
        y_na = neighborhood_attention(nq, nk, zg, cnk_rows, czg, bias_tab)
        gqT = head_prep(zg, _C_ZQ, GQA_Q_HEADS, gain=gqg, rope=rope, scale=qscale, layout="heads_t")
        gk = head_prep(zg, _C_ZK, GQA_KV_HEADS, gain=gkg, rope=rope, layout="heads_rows")
        gvT = head_prep(zg, _C_ZV, GQA_KV_HEADS, layout="heads_t")
        k_all = jnp.concatenate([gk, ck], axis=1)
        vT_all = jnp.concatenate([gvT, cvT], axis=2)
        y_gqa = flash_attention(gqT, k_all, vT_all, tq=512)

        merged = merge_branches([y_conv, y_four, y_na, y_gqa], merge_w, zg)
        h = mm_residual(merged, w_o_b, h, lat[2], "out_proj")
        u = nm_matmul(h, row(norm2[i]), lat[3], lat[4], w_up_b, zero_b, 2 * D_FF, "ffn_up")
        h = mm_residual(ffn_act(u, ffn_conv_w[i]), w_down_b, h, lat[5], "ffn_down")

        if not last:
            cy_conv = conv_mixer(czg, conv_w[i])
            cy_four = fourier_ctx(czg, ftabs)
            cnqT = head_prep(czg, _C_NQ, NA_HEADS, gain=nqg, scale=qscale, layout="heads_t")
            cnk = head_prep(czg, _C_NK, NA_HEADS, gain=nkg, layout="heads_rows")
            cnvT = head_prep(czg, _C_NV, NA_HEADS, layout="heads_t")
            cy_na = flash_attention(cnqT, cnk, cnvT, tq=CTX_LEN)
            cqT = head_prep(czg, _C_ZQ, GQA_Q_HEADS, gain=gqg, scale=qscale, layout="heads_t")
            cy_gqa = flash_attention(cqT, ck, cvT, tq=CTX_LEN)
            cmerged = merge_branches([cy_conv, cy_four, cy_na, cy_gqa], merge_w, czg)
            hc = mm_residual(cmerged, w_o_b, hc, cm[2], "out_proj_ctx")
            cu = nm_matmul(hc, row(norm2[i]), cm[3], cm[4], w_up_b, zero_b, 2 * D_FF, "ffn_up_ctx")
            hc = mm_residual(ffn_act(cu, ffn_conv_w[i]), w_down_b, hc, cm[5], "ffn_down_ctx")

    return h[None]
```

```python
import functools
import math

import jax
import jax.numpy as jnp
import numpy as np
from jax import lax
from jax.experimental import pallas as pl
from jax.experimental.pallas import tpu as pltpu

D_MODEL = 2048
SEQ = 16384
DEPTH = 4
GRID_W = 64
CTX_LEN = 256
HEAD_DIM = 128
CONV_W = 512
FOURIER_GROUPS = 4
FOURIER_W = 512
NA_HEADS = 4
NA_W = 512
NA_ROWS = 8
NA_COLS = 16
GQA_Q_HEADS = 8
GQA_KV_HEADS = 2
GQA_GROUP = 4
GQA_W = 1024
GQA_KV_W = 256
N_BRANCH = 4
N_IN = 5120
D_FF = 5632
ROPE_THETA = 10000.0
EPS = 1e-6

_C_XA, _C_BG, _C_CG = 0, 512, 1024
_C_ZF = 1536
_C_NQ, _C_NK, _C_NV = 2048, 2560, 3072
_C_ZQ, _C_ZK, _C_ZV = 3584, 4608, 4864
_C_GATE = N_IN

BF16 = jnp.bfloat16
F32 = jnp.float32
V7X_VMEM_LIMIT = 52 * 1024 * 1024
NEG_BIG = -1e30
HALO = 16


def _cp(*sem, flags=None):
    return pltpu.CompilerParams(dimension_semantics=sem, vmem_limit_bytes=V7X_VMEM_LIMIT, flags=flags)


def _ada_kernel(c_ref, w_ref, b_ref, o_ref):
    c = c_ref[...]
    s = c * (1.0 / (1.0 + jnp.exp(-c)))
    o_ref[...] = jnp.dot(s.astype(BF16), w_ref[...].astype(BF16),
                         preferred_element_type=F32) + b_ref[...]


def ada_mod(cvec, w, b):
    n = w.shape[1]
    tn = 1024
    return pl.pallas_call(
        _ada_kernel,
        out_shape=jax.ShapeDtypeStruct((8, n), F32),
        grid=(n // tn,),
        in_specs=[pl.BlockSpec((8, D_MODEL), lambda j: (0, 0)),
                  pl.BlockSpec((D_MODEL, tn), lambda j: (0, j)),
                  pl.BlockSpec((1, tn), lambda j: (0, j))],
        out_specs=pl.BlockSpec((8, tn), lambda j: (0, j)),
        compiler_params=_cp("parallel"),
        name="ada_mod",
    )(cvec, w, b)


def _nm_mm_kernel(h_ref, g_ref, sh_ref, sc_ref, w_ref, b_ref, o_ref, xn_ref, *, n_plain, n_total):
    j = pl.program_id(1)

    @pl.when(j == 0)
    def _():
        x = h_ref[...]
        ms = jnp.mean(x * x, axis=-1, keepdims=True)
        y = x * lax.rsqrt(ms + EPS) * g_ref[...]
        xn_ref[...] = (y * (1.0 + sc_ref[...]) + sh_ref[...]).astype(BF16)

    acc = jnp.dot(xn_ref[...], w_ref[...], preferred_element_type=F32)
    if n_plain == n_total:
        o_ref[...] = acc.astype(o_ref.dtype)
    else:
        @pl.when(j < n_plain)
        def _():
            o_ref[...] = acc.astype(o_ref.dtype)

        @pl.when(j >= n_plain)
        def _():
            t = acc + b_ref[...]
            o_ref[...] = (1.0 / (1.0 + jnp.exp(-t))).astype(o_ref.dtype)


def nm_matmul(h, gain, shift, scale, w, bias, n_plain_cols, name):
    m, d = h.shape
    n = w.shape[1]
    tm = min(m, 1024)
    tn = 1024
    kern = functools.partial(_nm_mm_kernel, n_plain=n_plain_cols // tn, n_total=n // tn)
    vec = pl.BlockSpec((1, d), lambda i, j: (0, 0))
    return pl.pallas_call(
        kern,
        out_shape=jax.ShapeDtypeStruct((m, n), BF16),
        grid=(m // tm, n // tn),
        in_specs=[pl.BlockSpec((tm, d), lambda i, j: (i, 0)), vec, vec, vec,
                  pl.BlockSpec((d, tn), lambda i, j: (0, j)),
                  pl.BlockSpec((1, tn), lambda i, j: (0, j))],
        out_specs=pl.BlockSpec((tm, tn), lambda i, j: (i, j)),
        scratch_shapes=[pltpu.VMEM((tm, d), BF16)],
        compiler_params=_cp("parallel", "arbitrary"),
        name=name,
    )(h, gain, shift, scale, w, bias)


def _mm_res_kernel(x_ref, w_ref, h_ref, g_ref, o_ref):
    acc = jnp.dot(x_ref[...], w_ref[...], preferred_element_type=F32)
    o_ref[...] = h_ref[...] + g_ref[...] * acc


def mm_residual(x, w, h, gate, name):
    m, k = x.shape
    n = w.shape[1]
    tm = min(m, 1024)
    tn = 512
    return pl.pallas_call(
        _mm_res_kernel,
        out_shape=jax.ShapeDtypeStruct((m, n), F32),
        grid=(m // tm, n // tn),
        in_specs=[pl.BlockSpec((tm, k), lambda i, j: (i, 0)),
                  pl.BlockSpec((k, tn), lambda i, j: (0, j)),
                  pl.BlockSpec((tm, tn), lambda i, j: (i, j)),
                  pl.BlockSpec((1, tn), lambda i, j: (0, j))],
        out_specs=pl.BlockSpec((tm, tn), lambda i, j: (i, j)),
        input_output_aliases={2: 0},
        compiler_params=_cp("parallel", "arbitrary"),
        name=name,
    )(x, w, h, gate)


def _merge_kernel(yc, yf, yn, yg, wc, wf, wn, wg, g0, g1, g2, g3, o_ref):
    def term(y, w, g):
        return g[...].astype(F32) * jnp.dot(y[...], w[...], preferred_element_type=F32)

    o_ref[...] = (term(yc, wc, g0) + term(yf, wf, g1) + term(yn, wn, g2)
                  + term(yg, wg, g3)).astype(o_ref.dtype)


def merge_branches(ys, ws, zg):
    m = zg.shape[0]
    tm = min(m, 1024)
    tn = 512
    y_specs = [pl.BlockSpec((tm, y.shape[1]), lambda i, j: (i, 0)) for y in ys]
    w_specs = [pl.BlockSpec((w.shape[0], tn), lambda i, j: (0, j)) for w in ws]
    g_specs = [pl.BlockSpec((tm, tn), functools.partial(
        lambda i, j, b: (i, (_C_GATE + b * D_MODEL) // tn + j), b=b)) for b in range(N_BRANCH)]
    return pl.pallas_call(
        _merge_kernel,
        out_shape=jax.ShapeDtypeStruct((m, D_MODEL), BF16),
        grid=(m // tm, D_MODEL // tn),
        in_specs=y_specs + w_specs + g_specs,
        out_specs=pl.BlockSpec((tm, tn), lambda i, j: (i, j)),
        compiler_params=_cp("parallel", "arbitrary"),
        name="merge",
    )(*ys, *ws, zg, zg, zg, zg)


def _conv3(p, prev_row, next_row, w_ref):
    tm = p.shape[0]
    row = lax.broadcasted_iota(jnp.int32, p.shape, 0)
    up = jnp.where(row == 0, prev_row, pltpu.roll(p, 1, 0))
    dn = jnp.where(row == tm - 1, next_row, pltpu.roll(p, tm - 1, 0))
    return up * w_ref[0:1, :] + p * w_ref[1:2, :] + dn * w_ref[2:3, :]


def _halo_specs(tm, tc, m, col_fn):
    nb = m // HALO
    per = tm // HALO
    main = pl.BlockSpec((tm, tc), lambda i, j: (i, col_fn(j)))
    prev = pl.BlockSpec((HALO, tc), lambda i, j: (jnp.maximum(i * per - 1, 0), col_fn(j)))
    nxt = pl.BlockSpec((HALO, tc), lambda i, j: (jnp.minimum((i + 1) * per, nb - 1), col_fn(j)))
    return main, prev, nxt


def _convmix_kernel(xa, xap, xan, cg, cgp, cgn, bg, w_ref, o_ref):
    i = pl.program_id(0)
    last = pl.num_programs(0) - 1
    p = cg[...].astype(F32) * xa[...].astype(F32)
    pp = cgp[...].astype(F32)[HALO - 1:HALO, :] * xap[...].astype(F32)[HALO - 1:HALO, :]
    pn = cgn[...].astype(F32)[0:1, :] * xan[...].astype(F32)[0:1, :]
    pp = jnp.where(i > 0, pp, 0.0)
    pn = jnp.where(i < last, pn, 0.0)
    o_ref[...] = (bg[...].astype(F32) * _conv3(p, pp, pn, w_ref)).astype(o_ref.dtype)


def conv_mixer(zg, w):
    m = zg.shape[0]
    tm = min(m, 1024)
    tc = CONV_W
    xa = _halo_specs(tm, tc, m, lambda j: _C_XA // tc)
    cg = _halo_specs(tm, tc, m, lambda j: _C_CG // tc)
    bg = pl.BlockSpec((tm, tc), lambda i, j: (i, _C_BG // tc))
    return pl.pallas_call(
        _convmix_kernel,
        out_shape=jax.ShapeDtypeStruct((m, CONV_W), BF16),
        grid=(m // tm, 1),
        in_specs=[*xa, *cg, bg, pl.BlockSpec((3, tc), lambda i, j: (0, 0))],
        out_specs=pl.BlockSpec((tm, tc), lambda i, j: (i, 0)),
        compiler_params=_cp("parallel", "arbitrary"),
        name="conv_mixer",
    )(zg, zg, zg, zg, zg, zg, zg, w)


def _ffn_act_kernel(a, ap, an, g, gp, gn, wa, wg, o_ref):
    i = pl.program_id(0)
    last = pl.num_programs(0) - 1

    def conv(x, xp, xn, w):
        pp = jnp.where(i > 0, xp[...].astype(F32)[HALO - 1:HALO, :], 0.0)
        pn = jnp.where(i < last, xn[...].astype(F32)[0:1, :], 0.0)
        return _conv3(x[...].astype(F32), pp, pn, w)

    ca = conv(a, ap, an, wa)
    cgt = conv(g, gp, gn, wg)
    o_ref[...] = (ca * (1.0 / (1.0 + jnp.exp(-ca))) * cgt).astype(o_ref.dtype)


def ffn_act(u, w):
    m = u.shape[0]
    tm = min(m, 1024)
    tc = 512
    nj = D_FF // tc
    a = _halo_specs(tm, tc, m, lambda j: j)
    g = _halo_specs(tm, tc, m, lambda j: nj + j)
    return pl.pallas_call(
        _ffn_act_kernel,
        out_shape=jax.ShapeDtypeStruct((m, D_FF), BF16),
        grid=(m // tm, nj),
        in_specs=[*a, *g, pl.BlockSpec((3, tc), lambda i, j: (0, j)),
                  pl.BlockSpec((3, tc), lambda i, j: (0, nj + j))],
        out_specs=pl.BlockSpec((tm, tc), lambda i, j: (i, j)),
        compiler_params=_cp("parallel", "arbitrary"),
        name="ffn_act",
    )(u, u, u, u, u, u, w, w)


def _head_prep_kernel(*refs, hb, use_norm, use_rope, scale, layout):
    it = iter(refs)
    z_ref = next(it)
    gain_ref = next(it) if use_norm else None
    cos_ref = next(it) if use_rope else None
    sin_ref = next(it) if use_rope else None
    o_ref = next(it)
    for h in range(hb):
        x = z_ref[:, h * HEAD_DIM:(h + 1) * HEAD_DIM].astype(F32)
        if use_norm:
            ms = jnp.mean(x * x, axis=-1, keepdims=True)
            x = x * lax.rsqrt(ms + EPS) * gain_ref[...]
        if use_rope:
            x = x * cos_ref[...] + pltpu.roll(x, HEAD_DIM // 2, 1) * sin_ref[...]
        if scale != 1.0:
            x = x * scale
        if layout == "rows":
            o_ref[:, h * HEAD_DIM:(h + 1) * HEAD_DIM] = x.astype(o_ref.dtype)
        elif layout == "heads_rows":
            o_ref[h] = x.astype(o_ref.dtype)
        else:
            o_ref[h] = x.T.astype(o_ref.dtype)


def head_prep(zg, col0, n_heads, gain=None, rope=None, scale=1.0, layout="rows"):
    m = zg.shape[0]
    tm = min(m, 512)
    hb = 2
    wb = hb * HEAD_DIM
    kern = functools.partial(_head_prep_kernel, hb=hb, use_norm=gain is not None,
                             use_rope=rope is not None, scale=scale, layout=layout)
    in_specs = [pl.BlockSpec((tm, wb), lambda i, j: (i, col0 // wb + j))]
    args = [zg]
    if gain is not None:
        in_specs.append(pl.BlockSpec((1, HEAD_DIM), lambda i, j: (0, 0)))
        args.append(gain)
    if rope is not None:
        in_specs += [pl.BlockSpec((tm, HEAD_DIM), lambda i, j: (i, 0))] * 2
        args += list(rope)
    if layout == "rows":
        out_shape = jax.ShapeDtypeStruct((m, n_heads * HEAD_DIM), BF16)
        out_spec = pl.BlockSpec((tm, wb), lambda i, j: (i, j))
    elif layout == "heads_rows":
        out_shape = jax.ShapeDtypeStruct((n_heads, m, HEAD_DIM), BF16)
        out_spec = pl.BlockSpec((hb, tm, HEAD_DIM), lambda i, j: (j, i, 0))
    else:
        out_shape = jax.ShapeDtypeStruct((n_heads, HEAD_DIM, m), BF16)
        out_spec = pl.BlockSpec((hb, HEAD_DIM, tm), lambda i, j: (j, 0, i))
    return pl.pallas_call(
        kern, out_shape=out_shape, grid=(m // tm, n_heads // hb),
        in_specs=in_specs, out_specs=out_spec,
        compiler_params=_cp("parallel", "arbitrary"),
        name="head_prep_" + layout,
    )(*args)


SCORE_BOUND_SAFE = 60.0
LOG2E = 1.4426950408889634
FLASH_UNROLL = 4


def _flash_kernel(qT_ref, k_ref, vT_ref, o_ref, kmax_ref, acc_ref, *, group, tk, n_keys):
    tq = qT_ref.shape[2]
    n_full = n_keys // tk
    rem = n_keys - n_full * tk

    @pl.when(pl.program_id(1) == 0)
    def _():
        def ksq(start, size):
            kb = k_ref[0, pl.ds(start, size), :].astype(F32)
            return jnp.max(jnp.sum(kb * kb, axis=-1, keepdims=True), axis=0, keepdims=True)

        mx = jnp.zeros((1, 1), F32)
        if n_full > 0:
            mx = lax.fori_loop(
                0, n_full, lambda i, c: jnp.maximum(c, ksq(pl.multiple_of(i * tk, tk), tk)), mx)
        if rem > 0:
            mx = jnp.maximum(mx, ksq(n_full * tk, rem))
        kmax_ref[...] = mx

    def run_blocks(step, carry):
        if n_full > 0:
            carry = lax.fori_loop(
                0, n_full, lambda kb, c: step(pl.multiple_of(kb * tk, tk), tk, c), carry)
        if rem > 0:
            carry = step(n_full * tk, rem, carry)
        return carry

    qTs = [qT_ref[h] for h in range(group)]
    shifts = []
    for qT in qTs:
        qf = qT.astype(F32)
        qsq = jnp.sum(qf * qf, axis=0, keepdims=True)
        shifts.append(jnp.sqrt(qsq * kmax_ref[...]) * 1.01)
    safe = jnp.max(functools.reduce(jnp.maximum, shifts)) <= SCORE_BOUND_SAFE

    @pl.when(safe)
    def _():
        def step(start, size, l8s, nb=1):
            units = [(j, h) for j in range(nb) for h in range(group)]
            kblks = [k_ref[0, pl.ds(start + j * size, size), :] for j in range(nb)]
            vblks = [vT_ref[0, :, pl.ds(start + j * size, size)] for j in range(nb)]
            l8s = list(l8s)
            s_next = jnp.dot(kblks[0], qTs[0], preferred_element_type=F32)
            for u, (j, h) in enumerate(units):
                s = s_next
                if u + 1 < len(units):
                    jn, hn = units[u + 1]
                    s_next = jnp.dot(kblks[jn], qTs[hn], preferred_element_type=F32)
                p = jnp.exp2(s - shifts[h])
                l8s[h] = l8s[h] + jnp.sum(p.reshape(size // 8, 8, tq), axis=0)
                acc_ref[h] += jnp.dot(vblks[j], p.astype(BF16), preferred_element_type=F32)
            return tuple(l8s)

        acc_ref[...] = jnp.zeros_like(acc_ref)
        l8s = tuple(jnp.zeros((8, tq), F32) for _ in range(group))
        n_pair = n_full // FLASH_UNROLL
        if n_pair > 0:
            l8s = lax.fori_loop(
                0, n_pair,
                lambda kb, c: step(pl.multiple_of(kb * (tk * FLASH_UNROLL), tk * FLASH_UNROLL), tk, c,
                                   nb=FLASH_UNROLL), l8s)
        for kb in range(n_pair * FLASH_UNROLL, n_full):
            l8s = step(kb * tk, tk, l8s)
        if rem > 0:
            l8s = step(n_full * tk, rem, l8s)
        for h in range(group):
            l = jnp.sum(l8s[h], axis=0, keepdims=True)
            acc_ref[h] = acc_ref[h] * (1.0 / l)

    @pl.when(jnp.logical_not(safe))
    def _():
        for h in range(group):
            def step(start, size, carry, qT=qTs[h]):
                m, l, acc = carry
                kblk = k_ref[0, pl.ds(start, size), :]
                s = jnp.dot(kblk, qT, preferred_element_type=F32)
                m_new = jnp.maximum(m, jnp.max(s, axis=0, keepdims=True))
                alpha = jnp.exp2(m - m_new)
                p = jnp.exp2(s - m_new)
                l = alpha * l + jnp.sum(p, axis=0, keepdims=True)
                vblk = vT_ref[0, :, pl.ds(start, size)]
                acc = alpha * acc + jnp.dot(vblk, p.astype(BF16), preferred_element_type=F32)
                return m_new, l, acc

            carry = (jnp.full((1, tq), NEG_BIG, F32), jnp.zeros((1, tq), F32),
                     jnp.zeros((HEAD_DIM, tq), F32))
            _, l, acc = run_blocks(step, carry)
            acc_ref[h] = acc * (1.0 / l)

    for h in range(group):
        o_ref[:, h * HEAD_DIM:(h + 1) * HEAD_DIM] = acc_ref[h].T.astype(o_ref.dtype)


def flash_attention(qT, k, vT, tq):
    hq, _, mq = qT.shape
    hkv, nk, _ = k.shape
    group = hq // hkv
    kern = functools.partial(_flash_kernel, group=group, tk=512, n_keys=nk)
    return pl.pallas_call(
        kern,
        out_shape=jax.ShapeDtypeStruct((mq, hq * HEAD_DIM), BF16),
        grid=(hkv, mq // tq),
        in_specs=[pl.BlockSpec((group, HEAD_DIM, tq), lambda g, i: (g, 0, i)),
                  pl.BlockSpec((1, nk, HEAD_DIM), lambda g, i: (g, 0, 0)),
                  pl.BlockSpec((1, HEAD_DIM, nk), lambda g, i: (g, 0, 0))],
        out_specs=pl.BlockSpec((tq, group * HEAD_DIM), lambda g, i: (i, g)),
        scratch_shapes=[pltpu.VMEM((1, 1), F32), pltpu.VMEM((group, HEAD_DIM, tq), F32)],
        compiler_params=_cp("parallel", "arbitrary"),
        name="flash_attention",
    )(qT, k, vT)


NA_RB = 8
NA_TOK = NA_RB * GRID_W
NA_WIN = NA_ROWS * GRID_W


def _na_kernel(q_ref, kp, kc, kn, vp, vc, vn, ck_ref, cv_ref, bias_ref, o_ref, kbuf, vbuf):
    b = pl.program_id(0)
    rows = pl.num_programs(0) * NA_RB
    kbuf[0:NA_TOK, :] = kp[...]
    kbuf[NA_TOK:2 * NA_TOK, :] = kc[...]
    kbuf[2 * NA_TOK:3 * NA_TOK, :] = kn[...]
    vbuf[0:NA_TOK, :] = vp[...]
    vbuf[NA_TOK:2 * NA_TOK, :] = vc[...]
    vbuf[2 * NA_TOK:3 * NA_TOK, :] = vn[...]
    nt = (((1,), (1,)), ((), ()))

    def row_body(rho, _):
        r = b * NA_RB + rho
        r0 = jnp.clip(r - NA_ROWS // 2, 0, rows - NA_ROWS)
        delta = r - r0
        off = pl.multiple_of((r0 - b * NA_RB + NA_RB) * GRID_W, GRID_W)
        qoff = pl.multiple_of(rho * GRID_W, GRID_W)
        q = q_ref[pl.ds(qoff, GRID_W), :]
        kw = kbuf[pl.ds(off, NA_WIN), :]
        vw = vbuf[pl.ds(off, NA_WIN), :]
        for h in range(NA_HEADS):
            sl = slice(h * HEAD_DIM, (h + 1) * HEAD_DIM)
            qh = q[:, sl]
            s = lax.dot_general(qh, kw[:, sl], nt, preferred_element_type=F32) + bias_ref[delta, h]
            sc = lax.dot_general(qh, ck_ref[:, sl], nt, preferred_element_type=F32)
            m = jnp.maximum(jnp.max(s, axis=-1, keepdims=True), jnp.max(sc, axis=-1, keepdims=True))
            p = jnp.exp(s - m)
            pc = jnp.exp(sc - m)
            l = jnp.sum(p, axis=-1, keepdims=True) + jnp.sum(pc, axis=-1, keepdims=True)
            o = (jnp.dot(p.astype(BF16), vw[:, sl], preferred_element_type=F32)
                 + jnp.dot(pc.astype(BF16), cv_ref[:, sl], preferred_element_type=F32))
            o_ref[pl.ds(qoff, GRID_W), sl] = (o * (1.0 / l)).astype(o_ref.dtype)
        return 0

    lax.fori_loop(0, NA_RB, row_body, 0)


def neighborhood_attention(qn, kn, zg, ckn, czg, bias):
    m = qn.shape[0]
    nb = m // NA_TOK
    vcol = _C_NV // NA_W
    blk = (NA_TOK, NA_W)
    return pl.pallas_call(
        _na_kernel,
        out_shape=jax.ShapeDtypeStruct((m, NA_W), BF16),
        grid=(nb,),
        in_specs=[pl.BlockSpec(blk, lambda b: (b, 0)),
                  pl.BlockSpec(blk, lambda b: (jnp.maximum(b - 1, 0), 0)),
                  pl.BlockSpec(blk, lambda b: (b, 0)),
                  pl.BlockSpec(blk, lambda b: (jnp.minimum(b + 1, nb - 1), 0)),
                  pl.BlockSpec(blk, lambda b: (jnp.maximum(b - 1, 0), vcol)),
                  pl.BlockSpec(blk, lambda b: (b, vcol)),
                  pl.BlockSpec(blk, lambda b: (jnp.minimum(b + 1, nb - 1), vcol)),
                  pl.BlockSpec((CTX_LEN, NA_W), lambda b: (0, 0)),
                  pl.BlockSpec((CTX_LEN, NA_W), lambda b: (0, vcol)),
                  pl.BlockSpec((NA_ROWS, NA_HEADS, GRID_W, NA_WIN), lambda b: (0, 0, 0, 0))],
        out_specs=pl.BlockSpec(blk, lambda b: (b, 0)),
        scratch_shapes=[pltpu.VMEM((3 * NA_TOK, NA_W), BF16), pltpu.VMEM((3 * NA_TOK, NA_W), BF16)],
        compiler_params=_cp("arbitrary"),
        name="neighborhood_attention",
    )(qn, kn, kn, kn, zg, zg, zg, ckn, czg, bias)


def na_bias_table(rpb):
    col = np.arange(GRID_W)
    c0 = np.clip(col - NA_COLS // 2, 0, GRID_W - NA_COLS)
    kc = np.arange(GRID_W)
    inside = (kc[None, :] >= c0[:, None]) & (kc[None, :] < c0[:, None] + NA_COLS)
    dc = kc[None, :] - col[:, None] + NA_COLS - 1
    onehot = (dc[None] == np.arange(2 * NA_COLS - 1)[:, None, None]) & inside[None]
    t = jnp.einsum('hrd,dck->hrck', rpb.astype(F32), jnp.asarray(onehot, F32),
                   precision=lax.Precision.HIGHEST)
    t = jnp.where(inside[None, None], t, NEG_BIG)
    tab = jnp.stack([t[:, NA_ROWS - 1 - dl:2 * NA_ROWS - 1 - dl] for dl in range(NA_ROWS)], axis=0)
    tab = tab.transpose(0, 1, 3, 2, 4)
    return tab.reshape(NA_ROWS, NA_HEADS, GRID_W, NA_WIN)


def _dft_mats(n):
    a = 2.0 * np.pi * np.outer(np.arange(n), np.arange(n)) / n
    return np.cos(a), np.sin(a)


def _chan_dft_kernel(z_ref, w_ref, o_ref):
    for g in range(FOURIER_GROUPS):
        x = z_ref[:, g * HEAD_DIM:(g + 1) * HEAD_DIM]
        r = jnp.dot(x, w_ref[...], preferred_element_type=F32)
        o_ref[0, :, g * HEAD_DIM:(g + 1) * HEAD_DIM] = r[:, :HEAD_DIM].astype(o_ref.dtype)
        o_ref[1, :, g * HEAD_DIM:(g + 1) * HEAD_DIM] = r[:, HEAD_DIM:].astype(o_ref.dtype)


def chan_dft(zg, wch):
    m = zg.shape[0]
    tm = min(m, 1024)
    return pl.pallas_call(
        _chan_dft_kernel,
        out_shape=jax.ShapeDtypeStruct((2, m, FOURIER_W), BF16),
        grid=(m // tm,),
        in_specs=[pl.BlockSpec((tm, FOURIER_W), lambda i: (i, _C_ZF // FOURIER_W)),
                  pl.BlockSpec((HEAD_DIM, 2 * HEAD_DIM), lambda i: (0, 0))],
        out_specs=pl.BlockSpec((2, tm, FOURIER_W), lambda i: (0, i, 0)),
        compiler_params=_cp("parallel"),
        name="chan_dft",
    )(zg, wch)


def _left_mm_kernel(l_ref, x_ref, o_ref, *, n_out):
    x = jnp.concatenate([x_ref[0], x_ref[1]], axis=0)
    r = jnp.dot(l_ref[...], x, preferred_element_type=F32)
    if n_out == 1:
        o_ref[...] = r.astype(o_ref.dtype)
    else:
        half = r.shape[0] // 2
        o_ref[0] = r[:half].astype(o_ref.dtype)
        o_ref[1] = r[half:].astype(o_ref.dtype)


def dft_stage1(zc, lmat):
    _, r, n = zc.shape
    tn = 4096
    return pl.pallas_call(
        functools.partial(_left_mm_kernel, n_out=2),
        out_shape=jax.ShapeDtypeStruct((2, r, n), BF16),
        grid=(n // tn,),
        in_specs=[pl.BlockSpec((2 * r, 2 * r), lambda j: (0, 0)),
                  pl.BlockSpec((2, r, tn), lambda j: (0, 0, j))],
        out_specs=pl.BlockSpec((2, r, tn), lambda j: (0, 0, j)),
        compiler_params=_cp("parallel"),
        name="dft_stage1",
    )(lmat, zc)


def _dft_stage3_kernel(l_ref, x_ref, o_ref, *, kb):
    for i in range(kb):
        x = jnp.concatenate([x_ref[0, i], x_ref[1, i]], axis=0)
        o_ref[i] = jnp.dot(l_ref[i], x, preferred_element_type=F32).astype(o_ref.dtype)


def dft_stage3(a, tables):
    _, n1, n2, c = a.shape
    kb = 8
    return pl.pallas_call(
        functools.partial(_dft_stage3_kernel, kb=kb),
        out_shape=jax.ShapeDtypeStruct((n1, n2, c), BF16),
        grid=(n1 // kb,),
        in_specs=[pl.BlockSpec((kb, n2, 2 * n2), lambda j: (j, 0, 0)),
                  pl.BlockSpec((2, kb, n2, c), lambda j: (0, j, 0, 0))],
        out_specs=pl.BlockSpec((kb, n2, c), lambda j: (j, 0, 0)),
        compiler_params=_cp("parallel"),
        name="dft_stage3",
    )(tables, a)


def dft_direct(zc, lmat):
    _, t, c = zc.shape
    return pl.pallas_call(
        functools.partial(_left_mm_kernel, n_out=1),
        out_shape=jax.ShapeDtypeStruct((t, c), BF16),
        grid=(1,),
        in_specs=[pl.BlockSpec((t, 2 * t), lambda j: (0, 0)),
                  pl.BlockSpec((2, t, c), lambda j: (0, 0, 0))],
        out_specs=pl.BlockSpec((t, c), lambda j: (0, 0)),
        compiler_params=_cp("arbitrary"),
        name="dft_direct",
    )(lmat, zc)


def _fourier_tables():
    c128, s128 = _dft_mats(HEAD_DIM)
    wch = np.concatenate([c128, -s128], axis=1)
    n1 = SEQ // HEAD_DIM
    l1 = np.block([[c128, s128], [-s128, c128]])
    k1 = np.arange(n1)[:, None, None]
    k2 = np.arange(n1)[None, :, None]
    t2 = np.arange(n1)[None, None, :]
    ang = 2.0 * np.pi * ((n1 * k2 + k1) * t2 % SEQ) / SEQ
    norm = 1.0 / math.sqrt(SEQ * HEAD_DIM)
    l3 = np.concatenate([np.cos(ang), np.sin(ang)], axis=2) * norm
    cc, sc = _dft_mats(CTX_LEN)
    lc = np.concatenate([cc, sc], axis=1) / math.sqrt(CTX_LEN * HEAD_DIM)
    as_bf = lambda a: jnp.asarray(a, F32).astype(BF16)
    return as_bf(wch), as_bf(l1), as_bf(l3), as_bf(lc)


def fourier_latent(zg, tabs):
    wch, l1, l3, _ = tabs
    n1 = SEQ // HEAD_DIM
    zc = chan_dft(zg, wch)
    a = dft_stage1(zc.reshape(2, n1, n1 * FOURIER_W), l1)
    o3 = dft_stage3(a.reshape(2, n1, n1, FOURIER_W), l3)
    return o3.transpose(1, 0, 2).reshape(SEQ, FOURIER_W)


def fourier_ctx(czg, tabs):
    wch, _, _, lc = tabs
    return dft_direct(chan_dft(czg, wch), lc)


def _rope_tables(n_tok):
    t = jnp.arange(n_tok)
    row = (t // GRID_W).astype(F32)
    col = (t % GRID_W).astype(F32)
    n_freq = HEAD_DIM // 4
    inv_freq = ROPE_THETA ** (-jnp.arange(n_freq, dtype=F32) / n_freq)
    ang = jnp.concatenate([row[:, None] * inv_freq, col[:, None] * inv_freq], axis=-1)
    cos, sin = jnp.cos(ang), jnp.sin(ang)
    return jnp.concatenate([cos, cos], axis=-1), jnp.concatenate([-sin, sin], axis=-1)


def kernel(x, c, ctx, c_ctx, w_ada, b_ada, norm1, w_in, conv_w, na_q_gain, na_k_gain, na_rpb, gqa_q_gain, gqa_k_gain, w_conv_out, w_fourier_out, w_na_out, w_gqa_out, w_gate, b_gate, w_o, norm2, w_up, ffn_conv_w, w_down):
    d = D_MODEL
    h = x[0]
    hc = ctx[0]
    rope = _rope_tables(SEQ)
    ftabs = _fourier_tables()
    qscale = HEAD_DIM ** -0.5
    qscale2 = qscale * LOG2E
    cvec = jnp.zeros((8, d), F32).at[0].set(c[0]).at[1].set(c_ctx)
    row = lambda v: v.reshape(1, -1)

    for i in range(DEPTH):
        last = i == DEPTH - 1
        w_cat = jnp.concatenate([w_in[i], w_gate[i]], axis=1).astype(BF16)
        b_cat = jnp.concatenate([jnp.zeros((N_IN,), F32), b_gate[i]]).reshape(1, -1)
        merge_w = [w_conv_out[i].astype(BF16), w_fourier_out[i].astype(BF16),
                   w_na_out[i].astype(BF16), w_gqa_out[i].astype(BF16)]
        w_o_b = w_o[i].astype(BF16)
        w_up_b = w_up[i].astype(BF16)
        w_down_b = w_down[i].astype(BF16)
        zero_b = jnp.zeros((1, 2 * D_FF), F32)
        bias_tab = na_bias_table(na_rpb[i])
        nqg, nkg = row(na_q_gain[i]), row(na_k_gain[i])
        gqg, gkg = row(gqa_q_gain[i]), row(gqa_k_gain[i])

        mods = ada_mod(cvec, w_ada[i], row(b_ada[i]))
        lat = [mods[0:1, k * d:(k + 1) * d] for k in range(6)]
        cm = [mods[1:2, k * d:(k + 1) * d] for k in range(6)]

        zg = nm_matmul(h, row(norm1[i]), lat[0], lat[1], w_cat, b_cat, N_IN, "in_proj")
        czg = nm_matmul(hc, row(norm1[i]), cm[0], cm[1], w_cat, b_cat, N_IN, "in_proj_ctx")

        cnk_rows = head_prep(czg, _C_NK, NA_HEADS, gain=nkg, layout="rows")
        ck = head_prep(czg, _C_ZK, GQA_KV_HEADS, gain=gkg, layout="heads_rows")
        cvT = head_prep(czg, _C_ZV, GQA_KV_HEADS, layout="heads_t")

        y_conv = conv_mixer(zg, conv_w[i])
        y_four = fourier_latent(zg, ftabs)
        nq = head_prep(zg, _C_NQ, NA_HEADS, gain=nqg, scale=qscale, layout="rows")
        nk = head_prep(zg, _C_NK, NA_HEADS, gain=nkg, layout="rows")
        y_na = neighborhood_attention(nq, nk, zg, cnk_rows, czg, bias_tab)
        gqT = head_prep(zg, _C_ZQ, GQA_Q_HEADS, gain=gqg, rope=rope, scale=qscale2, layout="heads_t")
        gk = head_prep(zg, _C_ZK, GQA_KV_HEADS, gain=gkg, rope=rope, layout="heads_rows")
        gvT = head_prep(zg, _C_ZV, GQA_KV_HEADS, layout="heads_t")
        k_all = jnp.concatenate([gk, ck], axis=1)
        vT_all = jnp.concatenate([gvT, cvT], axis=2)
        y_gqa = flash_attention(gqT, k_all, vT_all, tq=512)

        merged = merge_branches([y_conv, y_four, y_na, y_gqa], merge_w, zg)
        h = mm_residual(merged, w_o_b, h, lat[2], "out_proj")
        u = nm_matmul(h, row(norm2[i]), lat[3], lat[4], w_up_b, zero_b, 2 * D_FF, "ffn_up")
        h = mm_residual(ffn_act(u, ffn_conv_w[i]), w_down_b, h, lat[5], "ffn_down")

        if not last:
            cy_conv = conv_mixer(czg, conv_w[i])
            cy_four = fourier_ctx(czg, ftabs)
            cnqT = head_prep(czg, _C_NQ, NA_HEADS, gain=nqg, scale=qscale2, layout="heads_t")
            cnk = head_prep(czg, _C_NK, NA_HEADS, gain=nkg, layout="heads_rows")
            cnvT = head_prep(czg, _C_NV, NA_HEADS, layout="heads_t")
            cy_na = flash_attention(cnqT, cnk, cnvT, tq=CTX_LEN)
            cqT = head_prep(czg, _C_ZQ, GQA_Q_HEADS, gain=gqg, scale=qscale2, layout="heads_t")
            cy_gqa = flash_attention(cqT, ck, cvT, tq=CTX_LEN)
            cmerged = merge_branches([cy_conv, cy_four, cy_na, cy_gqa], merge_w, czg)
            hc = mm_residual(cmerged, w_o_b, hc, cm[2], "out_proj_ctx")
            cu = nm_matmul(hc, row(norm2[i]), cm[3], cm[4], w_up_b, zero_b, 2 * D_FF, "ffn_up_ctx")
            hc = mm_residual(ffn_act(cu, ffn_conv_w[i]), w_down_b, hc, cm[5], "ffn_down_ctx")

    return h[None]
```

```python
import functools
import math

import jax
import jax.numpy as jnp
import numpy as np
from jax import lax
from jax.experimental import pallas as pl
from jax.experimental.pallas import tpu as pltpu

D_MODEL = 2048
SEQ = 16384
DEPTH = 4
GRID_W = 64
CTX_LEN = 256
HEAD_DIM = 128
CONV_W = 512
FOURIER_GROUPS = 4
FOURIER_W = 512
NA_HEADS = 4
NA_W = 512
NA_ROWS = 8
NA_COLS = 16
GQA_Q_HEADS = 8
GQA_KV_HEADS = 2
GQA_GROUP = 4
GQA_W = 1024
GQA_KV_W = 256
N_BRANCH = 4
N_IN = 5120
D_FF = 5632
ROPE_THETA = 10000.0
EPS = 1e-6

_C_XA, _C_BG, _C_CG = 0, 512, 1024
_C_ZF = 1536
_C_NQ, _C_NK, _C_NV = 2048, 2560, 3072
_C_ZQ, _C_ZK, _C_ZV = 3584, 4608, 4864
_C_GATE = N_IN

BF16 = jnp.bfloat16
F32 = jnp.float32
V7X_VMEM_LIMIT = 52 * 1024 * 1024
NEG_BIG = -1e30
HALO = 16
LOG2E = 1.4426950408889634
N_KEYS = SEQ + CTX_LEN


def _cp(*sem):
    return pltpu.CompilerParams(dimension_semantics=sem, vmem_limit_bytes=V7X_VMEM_LIMIT)


def _sigmoid(t):
    return 1.0 / (1.0 + jnp.exp(-t))


def _norm_mod(x, g, sh, sc):
    ms = jnp.mean(x * x, axis=-1, keepdims=True)
    return (x * lax.rsqrt(ms + EPS) * g) * (1.0 + sc) + sh


def _ada_kernel(c_ref, w_ref, b_ref, o_ref):
    c = c_ref[...]
    s = c * _sigmoid(c)
    o_ref[...] = jnp.dot(s.astype(BF16), w_ref[...].astype(BF16),
                         preferred_element_type=F32) + b_ref[...]


def ada_mod(cvec, w, b, layer):
    n = w.shape[2]
    tn = 1024
    return pl.pallas_call(
        _ada_kernel,
        out_shape=jax.ShapeDtypeStruct((8, n), F32),
        grid=(n // tn,),
        in_specs=[pl.BlockSpec((8, D_MODEL), lambda j: (0, 0)),
                  pl.BlockSpec((None, D_MODEL, tn), lambda j: (layer, 0, j)),
                  pl.BlockSpec((None, 1, tn), lambda j: (layer, 0, j))],
        out_specs=pl.BlockSpec((8, tn), lambda j: (0, j)),
        compiler_params=_cp("parallel"),
        name="ada_mod",
    )(cvec, w, b)


def _in_proj_kernel(h_ref, g_ref, sh_ref, sc_ref, win_ref, wgate_ref, b_ref, o_ref, xn_ref, *, n_plain):
    j = pl.program_id(1)

    @pl.when(j == 0)
    def _():
        xn_ref[...] = _norm_mod(h_ref[...], g_ref[...], sh_ref[...], sc_ref[...]).astype(BF16)

    @pl.when(j < n_plain)
    def _():
        o_ref[...] = jnp.dot(xn_ref[...], win_ref[...], preferred_element_type=F32).astype(o_ref.dtype)

    @pl.when(j >= n_plain)
    def _():
        t = jnp.dot(xn_ref[...], wgate_ref[...], preferred_element_type=F32) + b_ref[...]
        o_ref[...] = _sigmoid(t).astype(o_ref.dtype)


def in_proj(h, gain, shift, scale, w_in, w_gate, b_gate, layer, name):
    m, d = h.shape
    n_in, n_gate = w_in.shape[2], w_gate.shape[2]
    tm = min(m, 1024)
    tn = 1024
    n_plain = n_in // tn
    vec = pl.BlockSpec((1, d), lambda i, j: (0, 0))
    return pl.pallas_call(
        functools.partial(_in_proj_kernel, n_plain=n_plain),
        out_shape=jax.ShapeDtypeStruct((m, n_in + n_gate), BF16),
        grid=(m // tm, (n_in + n_gate) // tn),
        in_specs=[pl.BlockSpec((tm, d), lambda i, j: (i, 0)), vec, vec, vec,
                  pl.BlockSpec((None, d, tn), lambda i, j: (layer, 0, jnp.minimum(j, n_plain - 1))),
                  pl.BlockSpec((None, d, tn), lambda i, j: (layer, 0, jnp.maximum(j - n_plain, 0))),
                  pl.BlockSpec((None, 1, tn), lambda i, j: (layer, 0, jnp.maximum(j - n_plain, 0)))],
        out_specs=pl.BlockSpec((tm, tn), lambda i, j: (i, j)),
        scratch_shapes=[pltpu.VMEM((tm, d), BF16)],
        compiler_params=_cp("parallel", "arbitrary"),
        name=name,
    )(h, gain, shift, scale, w_in, w_gate, b_gate)


def _mm_res_kernel(x_ref, w_ref, h_ref, g_ref, o_ref):
    acc = jnp.dot(x_ref[...], w_ref[...], preferred_element_type=F32)
    o_ref[...] = h_ref[...] + g_ref[...] * acc


def mm_residual(x, w, layer, h, gate, name, in_place):
    m, k = x.shape
    n = w.shape[2]
    tm = min(m, 1024)
    tn = 512
    return pl.pallas_call(
        _mm_res_kernel,
        out_shape=jax.ShapeDtypeStruct((m, n), F32),
        grid=(m // tm, n // tn),
        in_specs=[pl.BlockSpec((tm, k), lambda i, j: (i, 0)),
                  pl.BlockSpec((None, k, tn), lambda i, j: (layer, 0, j)),
                  pl.BlockSpec((tm, tn), lambda i, j: (i, j)),
                  pl.BlockSpec((1, tn), lambda i, j: (0, j))],
        out_specs=pl.BlockSpec((tm, tn), lambda i, j: (i, j)),
        input_output_aliases={2: 0} if in_place else {},
        compiler_params=_cp("parallel", "arbitrary"),
        name=name,
    )(x, w, h, gate)


def _merge_kernel(yc, yf, yn, yg, wc, wf, wn, wg, g0, g1, g2, g3, o_ref):
    def term(y, w, g):
        return g[...].astype(F32) * jnp.dot(y[...], w[...], preferred_element_type=F32)

    o_ref[...] = (term(yc, wc, g0) + term(yf, wf, g1) + term(yn, wn, g2)
                  + term(yg, wg, g3)).astype(o_ref.dtype)


def merge_branches(ys, ws, layer, zg):
    m = zg.shape[0]
    tm = min(m, 1024)
    tn = 512
    y_specs = [pl.BlockSpec((tm, y.shape[1]), lambda i, j: (i, 0)) for y in ys]
    w_specs = [pl.BlockSpec((None, w.shape[1], tn), lambda i, j: (layer, 0, j)) for w in ws]
    g_specs = [pl.BlockSpec((tm, tn), functools.partial(
        lambda i, j, b: (i, (_C_GATE + b * D_MODEL) // tn + j), b=b)) for b in range(N_BRANCH)]
    return pl.pallas_call(
        _merge_kernel,
        out_shape=jax.ShapeDtypeStruct((m, D_MODEL), BF16),
        grid=(m // tm, D_MODEL // tn),
        in_specs=y_specs + w_specs + g_specs,
        out_specs=pl.BlockSpec((tm, tn), lambda i, j: (i, j)),
        compiler_params=_cp("parallel", "arbitrary"),
        name="merge",
    )(*ys, *ws, zg, zg, zg, zg)


def _conv3(p, prev_row, next_row, w_ref):
    tm = p.shape[0]
    row = lax.broadcasted_iota(jnp.int32, p.shape, 0)
    up = jnp.where(row == 0, prev_row, pltpu.roll(p, 1, 0))
    dn = jnp.where(row == tm - 1, next_row, pltpu.roll(p, tm - 1, 0))
    return up * w_ref[0:1, :] + p * w_ref[1:2, :] + dn * w_ref[2:3, :]


def _halo_specs(tm, tc, m, col):
    nb = m // HALO
    per = tm // HALO
    main = pl.BlockSpec((tm, tc), lambda i: (i, col))
    prev = pl.BlockSpec((HALO, tc), lambda i: (jnp.maximum(i * per - 1, 0), col))
    nxt = pl.BlockSpec((HALO, tc), lambda i: (jnp.minimum((i + 1) * per, nb - 1), col))
    return main, prev, nxt


def _convmix_kernel(xa, xap, xan, cg, cgp, cgn, bg, w_ref, o_ref):
    i = pl.program_id(0)
    last = pl.num_programs(0) - 1
    p = cg[...].astype(F32) * xa[...].astype(F32)
    pp = cgp[...].astype(F32)[HALO - 1:HALO, :] * xap[...].astype(F32)[HALO - 1:HALO, :]
    pn = cgn[...].astype(F32)[0:1, :] * xan[...].astype(F32)[0:1, :]
    pp = jnp.where(i > 0, pp, 0.0)
    pn = jnp.where(i < last, pn, 0.0)
    o_ref[...] = (bg[...].astype(F32) * _conv3(p, pp, pn, w_ref)).astype(o_ref.dtype)


def conv_mixer(zg, w, layer):
    m = zg.shape[0]
    tm = min(m, 1024)
    tc = CONV_W
    xa = _halo_specs(tm, tc, m, _C_XA // tc)
    cg = _halo_specs(tm, tc, m, _C_CG // tc)
    bg = pl.BlockSpec((tm, tc), lambda i: (i, _C_BG // tc))
    return pl.pallas_call(
        _convmix_kernel,
        out_shape=jax.ShapeDtypeStruct((m, CONV_W), BF16),
        grid=(m // tm,),
        in_specs=[*xa, *cg, bg, pl.BlockSpec((None, 3, tc), lambda i: (layer, 0, 0))],
        out_specs=pl.BlockSpec((tm, tc), lambda i: (i, 0)),
        compiler_params=_cp("parallel"),
        name="conv_mixer",
    )(zg, zg, zg, zg, zg, zg, zg, w)


def _ffn_up_act_kernel(h_ref, hp_ref, hn_ref, g_ref, sh_ref, sc_ref, wa_ref, wg_ref, cwa_ref, cwg_ref,
                       o_ref, xn_ref):
    i = pl.program_id(0)
    j = pl.program_id(1)
    last = pl.num_programs(0) - 1
    tm = h_ref.shape[0]
    rows = tm + 2 * HALO

    @pl.when(j == 0)
    def _():
        nm = lambda x: _norm_mod(x, g_ref[...], sh_ref[...], sc_ref[...])
        xn_ref[0:HALO, :] = jnp.where(i > 0, nm(hp_ref[...]), 0.0).astype(BF16)
        xn_ref[HALO:HALO + tm, :] = nm(h_ref[...]).astype(BF16)
        xn_ref[HALO + tm:rows, :] = jnp.where(i < last, nm(hn_ref[...]), 0.0).astype(BF16)

    xn = xn_ref[...]

    def conv(w_ref, cw_ref):
        u = jnp.dot(xn, w_ref[...], preferred_element_type=F32)
        c = (pltpu.roll(u, 1, 0) * cw_ref[0:1, :] + u * cw_ref[1:2, :]
             + pltpu.roll(u, rows - 1, 0) * cw_ref[2:3, :])
        return c[HALO:HALO + tm]

    ca = conv(wa_ref, cwa_ref)
    cgt = conv(wg_ref, cwg_ref)
    o_ref[...] = (ca * _sigmoid(ca) * cgt).astype(o_ref.dtype)


def ffn_up_act(h, gain, shift, scale, w_up, conv_w, layer, name):
    m, d = h.shape
    tm = min(m, 1024)
    tn = 512
    nj = D_FF // tn
    nb = m // HALO
    per = tm // HALO
    vec = pl.BlockSpec((1, d), lambda i, j: (0, 0))
    return pl.pallas_call(
        _ffn_up_act_kernel,
        out_shape=jax.ShapeDtypeStruct((m, D_FF), BF16),
        grid=(m // tm, nj),
        in_specs=[pl.BlockSpec((tm, d), lambda i, j: (i, 0)),
                  pl.BlockSpec((HALO, d), lambda i, j: (jnp.maximum(i * per - 1, 0), 0)),
                  pl.BlockSpec((HALO, d), lambda i, j: (jnp.minimum((i + 1) * per, nb - 1), 0)),
                  vec, vec, vec,
                  pl.BlockSpec((None, d, tn), lambda i, j: (layer, 0, j)),
                  pl.BlockSpec((None, d, tn), lambda i, j: (layer, 0, nj + j)),
                  pl.BlockSpec((None, 3, tn), lambda i, j: (layer, 0, j)),
                  pl.BlockSpec((None, 3, tn), lambda i, j: (layer, 0, nj + j))],
        out_specs=pl.BlockSpec((tm, tn), lambda i, j: (i, j)),
        scratch_shapes=[pltpu.VMEM((tm + 2 * HALO, d), BF16)],
        compiler_params=_cp("parallel", "arbitrary"),
        name=name,
    )(h, h, h, gain, shift, scale, w_up, w_up, conv_w, conv_w)


def _head(z_ref, h, gain_ref=None, rope=None, scale=1.0):
    x = z_ref[:, h * HEAD_DIM:(h + 1) * HEAD_DIM].astype(F32)
    if gain_ref is not None:
        ms = jnp.mean(x * x, axis=-1, keepdims=True)
        x = x * lax.rsqrt(ms + EPS) * gain_ref[...]
    if rope is not None:
        x = x * rope[0][...] + pltpu.roll(x, HEAD_DIM // 2, 1) * rope[1][...]
    if scale != 1.0:
        x = x * scale
    return x


def _prep_latent_kernel(nq_ref, nk_ref, zqa_ref, zqb_ref, zk_ref, zv_ref, nqg, nkg, gqg, gkg, cos_ref, sin_ref,
                        kin_ref, vin_ref, nq_o, nk_o, gqT_o, k_o, vT_o, *, qscale):
    del kin_ref, vin_ref
    rope = (cos_ref, sin_ref)
    for h in range(NA_HEADS):
        sl = slice(h * HEAD_DIM, (h + 1) * HEAD_DIM)
        nq_o[:, sl] = _head(nq_ref, h, nqg, scale=qscale).astype(BF16)
        nk_o[:, sl] = _head(nk_ref, h, nkg).astype(BF16)
    for h in range(GQA_Q_HEADS):
        src = zqa_ref if h < GQA_Q_HEADS // 2 else zqb_ref
        gqT_o[h] = _head(src, h % (GQA_Q_HEADS // 2), gqg, rope, qscale).T.astype(BF16)
    for h in range(GQA_KV_HEADS):
        k_o[h] = _head(zk_ref, h, gkg, rope).astype(BF16)
        vT_o[h] = _head(zv_ref, h).T.astype(BF16)


def prep_latent(zg, gains, rope, kbuf, vTbuf, qscale):
    m = zg.shape[0]
    tm = 1024
    col = lambda c0, w: pl.BlockSpec((tm, w), lambda i: (i, c0 // w))
    gain = pl.BlockSpec((1, HEAD_DIM), lambda i: (0, 0))
    tab = pl.BlockSpec((tm, HEAD_DIM), lambda i: (i, 0))
    anyspec = pl.BlockSpec(memory_space=pl.ANY)
    return pl.pallas_call(
        functools.partial(_prep_latent_kernel, qscale=qscale),
        out_shape=(jax.ShapeDtypeStruct((m, NA_W), BF16), jax.ShapeDtypeStruct((m, NA_W), BF16),
                   jax.ShapeDtypeStruct((GQA_Q_HEADS, HEAD_DIM, m), BF16),
                   jax.ShapeDtypeStruct(kbuf.shape, BF16), jax.ShapeDtypeStruct(vTbuf.shape, BF16)),
        grid=(m // tm,),
        in_specs=[col(_C_NQ, NA_W), col(_C_NK, NA_W), col(_C_ZQ, GQA_W // 2), col(_C_ZQ + GQA_W // 2, GQA_W // 2),
                  col(_C_ZK, GQA_KV_W), col(_C_ZV, GQA_KV_W), gain, gain, gain, gain, tab, tab,
                  anyspec, anyspec],
        out_specs=(pl.BlockSpec((tm, NA_W), lambda i: (i, 0)), pl.BlockSpec((tm, NA_W), lambda i: (i, 0)),
                   pl.BlockSpec((GQA_Q_HEADS, HEAD_DIM, tm), lambda i: (0, 0, i)),
                   pl.BlockSpec((GQA_KV_HEADS, tm, HEAD_DIM), lambda i: (0, i, 0)),
                   pl.BlockSpec((GQA_KV_HEADS, HEAD_DIM, tm), lambda i: (0, 0, i))),
        input_output_aliases={12: 3, 13: 4},
        compiler_params=_cp("parallel"),
        name="prep_latent",
    )(zg, zg, zg, zg, zg, zg, *gains, *rope, kbuf, vTbuf)


def _prep_ctx_kernel(nq_ref, nk_ref, nv_ref, zqa_ref, zqb_ref, zk_ref, zv_ref, nqg, nkg, gqg, gkg,
                     nk_o, k_o, vT_o, cnqT_o, cnk_o, cnvT_o, cqT_o, ck_o, cvT_o, *, qscale):
    for h in range(NA_HEADS):
        sl = slice(h * HEAD_DIM, (h + 1) * HEAD_DIM)
        kn = _head(nk_ref, h, nkg).astype(BF16)
        nk_o[:, sl] = kn
        cnk_o[h] = kn
        cnqT_o[h] = _head(nq_ref, h, nqg, scale=qscale).T.astype(BF16)
        cnvT_o[h] = _head(nv_ref, h).T.astype(BF16)
    for h in range(GQA_Q_HEADS):
        src = zqa_ref if h < GQA_Q_HEADS // 2 else zqb_ref
        cqT_o[h] = _head(src, h % (GQA_Q_HEADS // 2), gqg, scale=qscale).T.astype(BF16)
    for h in range(GQA_KV_HEADS):
        k = _head(zk_ref, h, gkg).astype(BF16)
        vT = _head(zv_ref, h).T.astype(BF16)
        k_o[h] = k
        ck_o[h] = k
        vT_o[h] = vT
        cvT_o[h] = vT


def prep_ctx(czg, gains, qscale):
    m = czg.shape[0]
    col = lambda c0, w: pl.BlockSpec((m, w), lambda i: (0, c0 // w))
    gain = pl.BlockSpec((1, HEAD_DIM), lambda i: (0, 0))
    full = lambda shape: pl.BlockSpec(shape, lambda i: (0,) * len(shape))
    blk = SEQ // m
    return pl.pallas_call(
        functools.partial(_prep_ctx_kernel, qscale=qscale),
        out_shape=(jax.ShapeDtypeStruct((m, NA_W), BF16),
                   jax.ShapeDtypeStruct((GQA_KV_HEADS, N_KEYS, HEAD_DIM), BF16),
                   jax.ShapeDtypeStruct((GQA_KV_HEADS, HEAD_DIM, N_KEYS), BF16),
                   jax.ShapeDtypeStruct((NA_HEADS, HEAD_DIM, m), BF16),
                   jax.ShapeDtypeStruct((NA_HEADS, m, HEAD_DIM), BF16),
                   jax.ShapeDtypeStruct((NA_HEADS, HEAD_DIM, m), BF16),
                   jax.ShapeDtypeStruct((GQA_Q_HEADS, HEAD_DIM, m), BF16),
                   jax.ShapeDtypeStruct((GQA_KV_HEADS, m, HEAD_DIM), BF16),
                   jax.ShapeDtypeStruct((GQA_KV_HEADS, HEAD_DIM, m), BF16)),
        grid=(1,),
        in_specs=[col(_C_NQ, NA_W), col(_C_NK, NA_W), col(_C_NV, NA_W), col(_C_ZQ, GQA_W // 2),
                  col(_C_ZQ + GQA_W // 2, GQA_W // 2), col(_C_ZK, GQA_KV_W), col(_C_ZV, GQA_KV_W),
                  gain, gain, gain, gain],
        out_specs=(full((m, NA_W)),
                   pl.BlockSpec((GQA_KV_HEADS, m, HEAD_DIM), lambda i: (0, blk, 0)),
                   pl.BlockSpec((GQA_KV_HEADS, HEAD_DIM, m), lambda i: (0, 0, blk)),
                   full((NA_HEADS, HEAD_DIM, m)), full((NA_HEADS, m, HEAD_DIM)), full((NA_HEADS, HEAD_DIM, m)),
                   full((GQA_Q_HEADS, HEAD_DIM, m)), full((GQA_KV_HEADS, m, HEAD_DIM)),
                   full((GQA_KV_HEADS, HEAD_DIM, m))),
        compiler_params=_cp("arbitrary"),
        name="prep_ctx",
    )(czg, czg, czg, czg, czg, czg, czg, *gains)


SCORE_BOUND_SAFE = 60.0
FLASH_UNROLL = 4


def _flash_kernel(qT_ref, k_ref, vT_ref, o_ref, kmax_ref, acc_ref, *, group, tk, n_keys):
    tq = qT_ref.shape[2]
    n_full = n_keys // tk
    rem = n_keys - n_full * tk

    @pl.when(pl.program_id(1) == 0)
    def _():
        def ksq(start, size):
            kb = k_ref[0, pl.ds(start, size), :].astype(F32)
            return jnp.max(jnp.sum(kb * kb, axis=-1, keepdims=True), axis=0, keepdims=True)

        mx = jnp.zeros((1, 1), F32)
        if n_full > 0:
            mx = lax.fori_loop(
                0, n_full, lambda i, c: jnp.maximum(c, ksq(pl.multiple_of(i * tk, tk), tk)), mx)
        if rem > 0:
            mx = jnp.maximum(mx, ksq(n_full * tk, rem))
        kmax_ref[...] = mx

    def run_blocks(step, carry):
        if n_full > 0:
            carry = lax.fori_loop(
                0, n_full, lambda kb, c: step(pl.multiple_of(kb * tk, tk), tk, c), carry)
        if rem > 0:
            carry = step(n_full * tk, rem, carry)
        return carry

    qTs = [qT_ref[h] for h in range(group)]
    shifts = []
    for qT in qTs:
        qf = qT.astype(F32)
        qsq = jnp.sum(qf * qf, axis=0, keepdims=True)
        shifts.append(jnp.sqrt(qsq * kmax_ref[...]) * 1.01)
    safe = jnp.max(functools.reduce(jnp.maximum, shifts)) <= SCORE_BOUND_SAFE

    @pl.when(safe)
    def _():
        def step(start, size, l8s, nb=1):
            units = [(j, h) for j in range(nb) for h in range(group)]
            kblks = [k_ref[0, pl.ds(start + j * size, size), :] for j in range(nb)]
            vblks = [vT_ref[0, :, pl.ds(start + j * size, size)] for j in range(nb)]
            l8s = list(l8s)
            s_next = jnp.dot(kblks[0], qTs[0], preferred_element_type=F32)
            for u, (j, h) in enumerate(units):
                s = s_next
                if u + 1 < len(units):
                    jn, hn = units[u + 1]
                    s_next = jnp.dot(kblks[jn], qTs[hn], preferred_element_type=F32)
                p = jnp.exp2(s - shifts[h])
                l8s[h] = l8s[h] + jnp.sum(p.reshape(size // 8, 8, tq), axis=0)
                acc_ref[h] += jnp.dot(vblks[j], p.astype(BF16), preferred_element_type=F32)
            return tuple(l8s)

        acc_ref[...] = jnp.zeros_like(acc_ref)
        l8s = tuple(jnp.zeros((8, tq), F32) for _ in range(group))
        n_pair = n_full // FLASH_UNROLL
        if n_pair > 0:
            l8s = lax.fori_loop(
                0, n_pair,
                lambda kb, c: step(pl.multiple_of(kb * (tk * FLASH_UNROLL), tk * FLASH_UNROLL), tk, c,
                                   nb=FLASH_UNROLL), l8s)
        for kb in range(n_pair * FLASH_UNROLL, n_full):
            l8s = step(kb * tk, tk, l8s)
        if rem > 0:
            l8s = step(n_full * tk, rem, l8s)
        for h in range(group):
            l = jnp.sum(l8s[h], axis=0, keepdims=True)
            acc_ref[h] = acc_ref[h] * (1.0 / l)

    @pl.when(jnp.logical_not(safe))
    def _():
        for h in range(group):
            def step(start, size, carry, qT=qTs[h]):
                m, l, acc = carry
                kblk = k_ref[0, pl.ds(start, size), :]
                s = jnp.dot(kblk, qT, preferred_element_type=F32)
                m_new = jnp.maximum(m, jnp.max(s, axis=0, keepdims=True))
                alpha = jnp.exp2(m - m_new)
                p = jnp.exp2(s - m_new)
                l = alpha * l + jnp.sum(p, axis=0, keepdims=True)
                vblk = vT_ref[0, :, pl.ds(start, size)]
                acc = alpha * acc + jnp.dot(vblk, p.astype(BF16), preferred_element_type=F32)
                return m_new, l, acc

            carry = (jnp.full((1, tq), NEG_BIG, F32), jnp.zeros((1, tq), F32),
                     jnp.zeros((HEAD_DIM, tq), F32))
            _, l, acc = run_blocks(step, carry)
            acc_ref[h] = acc * (1.0 / l)

    for h in range(group):
        o_ref[:, h * HEAD_DIM:(h + 1) * HEAD_DIM] = acc_ref[h].T.astype(o_ref.dtype)


def flash_attention(qT, k, vT, tq):
    hq, _, mq = qT.shape
    hkv, nk, _ = k.shape
    group = hq // hkv
    kern = functools.partial(_flash_kernel, group=group, tk=512, n_keys=nk)
    return pl.pallas_call(
        kern,
        out_shape=jax.ShapeDtypeStruct((mq, hq * HEAD_DIM), BF16),
        grid=(hkv, mq // tq),
        in_specs=[pl.BlockSpec((group, HEAD_DIM, tq), lambda g, i: (g, 0, i)),
                  pl.BlockSpec((1, nk, HEAD_DIM), lambda g, i: (g, 0, 0)),
                  pl.BlockSpec((1, HEAD_DIM, nk), lambda g, i: (g, 0, 0))],
        out_specs=pl.BlockSpec((tq, group * HEAD_DIM), lambda g, i: (i, g)),
        scratch_shapes=[pltpu.VMEM((1, 1), F32), pltpu.VMEM((group, HEAD_DIM, tq), F32)],
        compiler_params=_cp("parallel", "arbitrary"),
        name="flash_attention",
    )(qT, k, vT)


NA_RB = 8
NA_TOK = NA_RB * GRID_W
NA_PAIR = 2 * GRID_W
NA_SLAB_ROWS = NA_ROWS + 2
NA_SLAB = NA_SLAB_ROWS * GRID_W
NA_PATTERNS = 5


def _na_kernel(q_ref, kp, kc, kn, vp, vc, vn, ck_ref, cv_ref, bias_ref, o_ref, kbuf, vbuf):
    b = pl.program_id(0)
    rows = pl.num_programs(0) * NA_RB
    kbuf[0:NA_TOK, :] = kp[...]
    kbuf[NA_TOK:2 * NA_TOK, :] = kc[...]
    kbuf[2 * NA_TOK:3 * NA_TOK, :] = kn[...]
    vbuf[0:NA_TOK, :] = vp[...]
    vbuf[NA_TOK:2 * NA_TOK, :] = vc[...]
    vbuf[2 * NA_TOK:3 * NA_TOK, :] = vn[...]
    nt = (((1,), (1,)), ((), ()))

    for pair in range(NA_RB // 2):
        r = b * NA_RB + 2 * pair
        u0 = jnp.clip(r - NA_ROWS // 2, 0, rows - NA_ROWS)
        off = pl.multiple_of((u0 - b * NA_RB + NA_RB) * GRID_W, NA_PAIR)
        pat = jnp.where(r < NA_ROWS // 2, r // 2,
                        jnp.where(r >= rows - NA_ROWS // 2, 3 + (r - (rows - NA_ROWS // 2)) // 2, 2))
        q = q_ref[pair * NA_PAIR:(pair + 1) * NA_PAIR, :]
        kw = kbuf[pl.ds(off, NA_SLAB), :]
        vw = vbuf[pl.ds(off, NA_SLAB), :]
        for h in range(NA_HEADS):
            sl = slice(h * HEAD_DIM, (h + 1) * HEAD_DIM)
            qh = q[:, sl]
            s = lax.dot_general(qh, kw[:, sl], nt, preferred_element_type=F32) + bias_ref[pat, h]
            sc = lax.dot_general(qh, ck_ref[:, sl], nt, preferred_element_type=F32)
            m = jnp.maximum(jnp.max(s, axis=-1, keepdims=True), jnp.max(sc, axis=-1, keepdims=True))
            p = jnp.exp2(s - m)
            pc = jnp.exp2(sc - m)
            l = jnp.sum(p, axis=-1, keepdims=True) + jnp.sum(pc, axis=-1, keepdims=True)
            o = (jnp.dot(p.astype(BF16), vw[:, sl], preferred_element_type=F32)
                 + jnp.dot(pc.astype(BF16), cv_ref[:, sl], preferred_element_type=F32))
            o_ref[pair * NA_PAIR:(pair + 1) * NA_PAIR, sl] = (o * (1.0 / l)).astype(o_ref.dtype)


def neighborhood_attention(qn, kn, zg, ckn, czg, bias):
    m = qn.shape[0]
    nb = m // NA_TOK
    vcol = _C_NV // NA_W
    blk = (NA_TOK, NA_W)
    return pl.pallas_call(
        _na_kernel,
        out_shape=jax.ShapeDtypeStruct((m, NA_W), BF16),
        grid=(nb,),
        in_specs=[pl.BlockSpec(blk, lambda b: (b, 0)),
                  pl.BlockSpec(blk, lambda b: (jnp.maximum(b - 1, 0), 0)),
                  pl.BlockSpec(blk, lambda b: (b, 0)),
                  pl.BlockSpec(blk, lambda b: (jnp.minimum(b + 1, nb - 1), 0)),
                  pl.BlockSpec(blk, lambda b: (jnp.maximum(b - 1, 0), vcol)),
                  pl.BlockSpec(blk, lambda b: (b, vcol)),
                  pl.BlockSpec(blk, lambda b: (jnp.minimum(b + 1, nb - 1), vcol)),
                  pl.BlockSpec((CTX_LEN, NA_W), lambda b: (0, 0)),
                  pl.BlockSpec((CTX_LEN, NA_W), lambda b: (0, vcol)),
                  pl.BlockSpec((NA_PATTERNS, NA_HEADS, NA_PAIR, NA_SLAB), lambda b: (0, 0, 0, 0))],
        out_specs=pl.BlockSpec(blk, lambda b: (b, 0)),
        scratch_shapes=[pltpu.VMEM((3 * NA_TOK, NA_W), BF16), pltpu.VMEM((3 * NA_TOK, NA_W), BF16)],
        compiler_params=_cp("arbitrary"),
        name="neighborhood_attention",
    )(qn, kn, kn, kn, zg, zg, zg, ckn, czg, bias)


def na_bias_table(rpb):
    col = np.arange(GRID_W)
    c0 = np.clip(col - NA_COLS // 2, 0, GRID_W - NA_COLS)
    kc = np.arange(GRID_W)
    inside = (kc[None, :] >= c0[:, None]) & (kc[None, :] < c0[:, None] + NA_COLS)
    dc = kc[None, :] - col[:, None] + NA_COLS - 1
    onehot = (dc[None] == np.arange(2 * NA_COLS - 1)[:, None, None]) & inside[None]
    t = jnp.einsum('hrd,dck->hrck', rpb.astype(F32) * LOG2E, jnp.asarray(onehot, F32),
                   precision=lax.Precision.HIGHEST)
    t = jnp.where(inside[None, None], t, NEG_BIG)
    masked = jnp.full((NA_HEADS, GRID_W, GRID_W), NEG_BIG, F32)
    patterns = [(0, 0), (2, 0), (4, 1), (4, 0), (6, 0)]
    tabs = []
    for a, e in patterns:
        per_q = []
        for start, dr0 in ((0, NA_ROWS - 1 - a), (e, NA_ROWS - 2 - a)):
            blocks = [t[:, w + dr0] if start <= w < start + NA_ROWS else masked for w in range(NA_SLAB_ROWS)]
            per_q.append(jnp.stack(blocks, axis=2).reshape(NA_HEADS, GRID_W, NA_SLAB))
        tabs.append(jnp.concatenate(per_q, axis=1))
    return jnp.stack(tabs, axis=0)


def _dft_mats(n):
    a = 2.0 * np.pi * np.outer(np.arange(n), np.arange(n)) / n
    return np.cos(a), np.sin(a)


def _chan_dft_kernel(z_ref, w_ref, o_ref):
    for g in range(FOURIER_GROUPS):
        x = z_ref[:, g * HEAD_DIM:(g + 1) * HEAD_DIM]
        r = jnp.dot(x, w_ref[...], preferred_element_type=F32)
        o_ref[0, :, g * HEAD_DIM:(g + 1) * HEAD_DIM] = r[:, :HEAD_DIM].astype(o_ref.dtype)
        o_ref[1, :, g * HEAD_DIM:(g + 1) * HEAD_DIM] = r[:, HEAD_DIM:].astype(o_ref.dtype)


def chan_dft(zg, wch):
    m = zg.shape[0]
    tm = min(m, 1024)
    return pl.pallas_call(
        _chan_dft_kernel,
        out_shape=jax.ShapeDtypeStruct((2, m, FOURIER_W), BF16),
        grid=(m // tm,),
        in_specs=[pl.BlockSpec((tm, FOURIER_W), lambda i: (i, _C_ZF // FOURIER_W)),
                  pl.BlockSpec((HEAD_DIM, 2 * HEAD_DIM), lambda i: (0, 0))],
        out_specs=pl.BlockSpec((2, tm, FOURIER_W), lambda i: (0, i, 0)),
        compiler_params=_cp("parallel"),
        name="chan_dft",
    )(zg, wch)


def _left_mm_kernel(l_ref, x_ref, o_ref, *, n_out):
    x = jnp.concatenate([x_ref[0], x_ref[1]], axis=0)
    r = jnp.dot(l_ref[...], x, preferred_element_type=F32)
    if n_out == 1:
        o_ref[...] = r.astype(o_ref.dtype)
    else:
        half = r.shape[0] // 2
        o_ref[0] = r[:half].astype(o_ref.dtype)
        o_ref[1] = r[half:].astype(o_ref.dtype)


def dft_stage1(zc, lmat):
    _, r, n = zc.shape
    tn = 4096
    return pl.pallas_call(
        functools.partial(_left_mm_kernel, n_out=2),
        out_shape=jax.ShapeDtypeStruct((2, r, n), BF16),
        grid=(n // tn,),
        in_specs=[pl.BlockSpec((2 * r, 2 * r), lambda j: (0, 0)),
                  pl.BlockSpec((2, r, tn), lambda j: (0, 0, j))],
        out_specs=pl.BlockSpec((2, r, tn), lambda j: (0, 0, j)),
        compiler_params=_cp("parallel"),
        name="dft_stage1",
    )(lmat, zc)


def _dft_stage3_kernel(l_ref, x_ref, o_ref, *, kb):
    for i in range(kb):
        x = jnp.concatenate([x_ref[0, i], x_ref[1, i]], axis=0)
        o_ref[i] = jnp.dot(l_ref[i], x, preferred_element_type=F32).astype(o_ref.dtype)


def dft_stage3(a, tables):
    _, n1, n2, c = a.shape
    kb = 8
    return pl.pallas_call(
        functools.partial(_dft_stage3_kernel, kb=kb),
        out_shape=jax.ShapeDtypeStruct((n1, n2, c), BF16),
        grid=(n1 // kb,),
        in_specs=[pl.BlockSpec((kb, n2, 2 * n2), lambda j: (j, 0, 0)),
                  pl.BlockSpec((2, kb, n2, c), lambda j: (0, j, 0, 0))],
        out_specs=pl.BlockSpec((kb, n2, c), lambda j: (j, 0, 0)),
        compiler_params=_cp("parallel"),
        name="dft_stage3",
    )(tables, a)


def dft_direct(zc, lmat):
    _, t, c = zc.shape
    return pl.pallas_call(
        functools.partial(_left_mm_kernel, n_out=1),
        out_shape=jax.ShapeDtypeStruct((t, c), BF16),
        grid=(1,),
        in_specs=[pl.BlockSpec((t, 2 * t), lambda j: (0, 0)),
                  pl.BlockSpec((2, t, c), lambda j: (0, 0, 0))],
        out_specs=pl.BlockSpec((t, c), lambda j: (0, 0)),
        compiler_params=_cp("arbitrary"),
        name="dft_direct",
    )(lmat, zc)


def _fourier_tables():
    c128, s128 = _dft_mats(HEAD_DIM)
    wch = np.concatenate([c128, -s128], axis=1)
    n1 = SEQ // HEAD_DIM
    l1 = np.block([[c128, s128], [-s128, c128]])
    k1 = np.arange(n1)[:, None, None]
    k2 = np.arange(n1)[None, :, None]
    t2 = np.arange(n1)[None, None, :]
    ang = 2.0 * np.pi * ((n1 * k2 + k1) * t2 % SEQ) / SEQ
    norm = 1.0 / math.sqrt(SEQ * HEAD_DIM)
    l3 = np.concatenate([np.cos(ang), np.sin(ang)], axis=2) * norm
    cc, sc = _dft_mats(CTX_LEN)
    lc = np.concatenate([cc, sc], axis=1) / math.sqrt(CTX_LEN * HEAD_DIM)
    as_bf = lambda a: jnp.asarray(a, F32).astype(BF16)
    return as_bf(wch), as_bf(l1), as_bf(l3), as_bf(lc)


def fourier_latent(zg, tabs):
    wch, l1, l3, _ = tabs
    n1 = SEQ // HEAD_DIM
    zc = chan_dft(zg, wch)
    a = dft_stage1(zc.reshape(2, n1, n1 * FOURIER_W), l1)
    o3 = dft_stage3(a.reshape(2, n1, n1, FOURIER_W), l3)
    return o3.transpose(1, 0, 2).reshape(SEQ, FOURIER_W)


def fourier_ctx(czg, tabs):
    wch, _, _, lc = tabs
    return dft_direct(chan_dft(czg, wch), lc)


def _rope_tables(n_tok):
    t = jnp.arange(n_tok)
    row = (t // GRID_W).astype(F32)
    col = (t % GRID_W).astype(F32)
    n_freq = HEAD_DIM // 4
    inv_freq = ROPE_THETA ** (-jnp.arange(n_freq, dtype=F32) / n_freq)
    ang = jnp.concatenate([row[:, None] * inv_freq, col[:, None] * inv_freq], axis=-1)
    cos, sin = jnp.cos(ang), jnp.sin(ang)
    return jnp.concatenate([cos, cos], axis=-1), jnp.concatenate([-sin, sin], axis=-1)


def kernel(x, c, ctx, c_ctx, w_ada, b_ada, norm1, w_in, conv_w, na_q_gain, na_k_gain, na_rpb, gqa_q_gain, gqa_k_gain, w_conv_out, w_fourier_out, w_na_out, w_gqa_out, w_gate, b_gate, w_o, norm2, w_up, ffn_conv_w, w_down):
    d = D_MODEL
    h = x[0]
    hc = ctx[0]
    rope = _rope_tables(SEQ)
    ftabs = _fourier_tables()
    qscale2 = HEAD_DIM ** -0.5 * LOG2E
    cvec = jnp.zeros((8, d), F32).at[0].set(c[0]).at[1].set(c_ctx)
    row = lambda v: v.reshape(1, -1)

    w_in_b, w_gate_b = w_in.astype(BF16), w_gate.astype(BF16)
    merge_w = [w.astype(BF16) for w in (w_conv_out, w_fourier_out, w_na_out, w_gqa_out)]
    w_o_b, w_up_b, w_down_b = w_o.astype(BF16), w_up.astype(BF16), w_down.astype(BF16)
    b_ada3 = b_ada.reshape(DEPTH, 1, -1)
    b_gate3 = b_gate.reshape(DEPTH, 1, -1)

    for i in range(DEPTH):
        last = i == DEPTH - 1
        bias_tab = na_bias_table(na_rpb[i])
        gains = (row(na_q_gain[i]), row(na_k_gain[i]), row(gqa_q_gain[i]), row(gqa_k_gain[i]))

        mods = ada_mod(cvec, w_ada, b_ada3, i)
        lat = [mods[0:1, k * d:(k + 1) * d] for k in range(6)]
        cm = [mods[1:2, k * d:(k + 1) * d] for k in range(6)]

        zg = in_proj(h, row(norm1[i]), lat[0], lat[1], w_in_b, w_gate_b, b_gate3, i, "in_proj")
        czg = in_proj(hc, row(norm1[i]), cm[0], cm[1], w_in_b, w_gate_b, b_gate3, i, "in_proj_ctx")

        cnk_rows, kbuf, vTbuf, cnqT, cnk, cnvT, cqT, ck, cvT = prep_ctx(czg, gains, qscale2)
        nq, nk, gqT, k_all, vT_all = prep_latent(zg, gains, rope, kbuf, vTbuf, qscale2)

        y_conv = conv_mixer(zg, conv_w, i)
        y_four = fourier_latent(zg, ftabs)
        y_na = neighborhood_attention(nq, nk, zg, cnk_rows, czg, bias_tab)
        y_gqa = flash_attention(gqT, k_all, vT_all, tq=512)

        merged = merge_branches([y_conv, y_four, y_na, y_gqa], merge_w, i, zg)
        h = mm_residual(merged, w_o_b, i, h, lat[2], "out_proj", in_place=i > 0)
        act = ffn_up_act(h, row(norm2[i]), lat[3], lat[4], w_up_b, ffn_conv_w, i, "ffn_up_act")
        h = mm_residual(act, w_down_b, i, h, lat[5], "ffn_down", in_place=True)

        if not last:
            cy_conv = conv_mixer(czg, conv_w, i)
            cy_four = fourier_ctx(czg, ftabs)
            cy_na = flash_attention(cnqT, cnk, cnvT, tq=CTX_LEN)
            cy_gqa = flash_attention(cqT, ck, cvT, tq=CTX_LEN)
            cmerged = merge_branches([cy_conv, cy_four, cy_na, cy_gqa], merge_w, i, czg)
            hc = mm_residual(cmerged, w_o_b, i, hc, cm[2], "out_proj_ctx", in_place=i > 0)
            cact = ffn_up_act(hc, row(norm2[i]), cm[3], cm[4], w_up_b, ffn_conv_w, i, "ffn_up_act_ctx")
            hc = mm_residual(cact, w_down_b, i, hc, cm[5], "ffn_down_ctx", in_place=True)

    return h[None]
```

```python
import functools
import math

import jax
import jax.numpy as jnp
import numpy as np
from jax import lax
from jax.experimental import pallas as pl
from jax.experimental.pallas import tpu as pltpu

D_MODEL = 2048
SEQ = 16384
DEPTH = 4
GRID_W = 64
CTX_LEN = 256
HEAD_DIM = 128
CONV_W = 512
FOURIER_GROUPS = 4
FOURIER_W = 512
NA_HEADS = 4
NA_W = 512
NA_ROWS = 8
NA_COLS = 16
GQA_Q_HEADS = 8
GQA_KV_HEADS = 2
GQA_GROUP = 4
GQA_W = 1024
GQA_KV_W = 256
N_BRANCH = 4
N_IN = 5120
D_FF = 5632
ROPE_THETA = 10000.0
EPS = 1e-6

_C_XA, _C_BG, _C_CG = 0, 512, 1024
_C_ZF = 1536
_C_NQ, _C_NK, _C_NV = 2048, 2560, 3072
_C_ZQ, _C_ZK, _C_ZV = 3584, 4608, 4864
_C_GATE = N_IN

BF16 = jnp.bfloat16
F32 = jnp.float32
V7X_VMEM_LIMIT = 52 * 1024 * 1024
NEG_BIG = -1e30
HALO = 16
NORM_CHUNK = 256
LOG2E = 1.4426950408889634
N_KEYS = SEQ + CTX_LEN


def _cp(*sem):
    return pltpu.CompilerParams(dimension_semantics=sem, vmem_limit_bytes=V7X_VMEM_LIMIT)


def _sigmoid(t):
    return 1.0 / (1.0 + jnp.exp(-t))


def _norm_mod(x, g, sh, sc):
    ms = jnp.mean(x * x, axis=-1, keepdims=True)
    return (x * lax.rsqrt(ms + EPS) * g) * (1.0 + sc) + sh


def _ada_kernel(c_ref, w_ref, b_ref, o_ref):
    c = c_ref[...]
    s = c * _sigmoid(c)
    o_ref[...] = jnp.dot(s.astype(BF16), w_ref[...].astype(BF16),
                         preferred_element_type=F32) + b_ref[...]


def ada_mod(cvec, w, b, layer):
    n = w.shape[2]
    tn = 1024
    return pl.pallas_call(
        _ada_kernel,
        out_shape=jax.ShapeDtypeStruct((8, n), F32),
        grid=(n // tn,),
        in_specs=[pl.BlockSpec((8, D_MODEL), lambda j: (0, 0)),
                  pl.BlockSpec((None, D_MODEL, tn), lambda j: (layer, 0, j)),
                  pl.BlockSpec((None, 1, tn), lambda j: (layer, 0, j))],
        out_specs=pl.BlockSpec((8, tn), lambda j: (0, j)),
        compiler_params=_cp("parallel"),
        name="ada_mod",
    )(cvec, w, b)


def _in_proj_kernel(h_ref, g_ref, sh_ref, sc_ref, win_ref, wgate_ref, b_ref, o_ref, xn_ref, *, n_plain):
    j = pl.program_id(1)
    tm = h_ref.shape[0]
    cm = min(tm, NORM_CHUNK)

    @pl.when(j == 0)
    def _():
        for c in range(tm // cm):
            rows = slice(c * cm, (c + 1) * cm)
            xc = _norm_mod(h_ref[rows, :], g_ref[...], sh_ref[...], sc_ref[...]).astype(BF16)
            xn_ref[rows, :] = xc
            o_ref[rows, :] = jnp.dot(xc, win_ref[...], preferred_element_type=F32).astype(o_ref.dtype)

    @pl.when(jnp.logical_and(j > 0, j < n_plain))
    def _():
        o_ref[...] = jnp.dot(xn_ref[...], win_ref[...], preferred_element_type=F32).astype(o_ref.dtype)

    @pl.when(j >= n_plain)
    def _():
        t = jnp.dot(xn_ref[...], wgate_ref[...], preferred_element_type=F32) + b_ref[...]
        o_ref[...] = _sigmoid(t).astype(o_ref.dtype)


def in_proj(h, gain, shift, scale, w_in, w_gate, b_gate, layer, name):
    m, d = h.shape
    n_in, n_gate = w_in.shape[2], w_gate.shape[2]
    tm = min(m, 1024)
    tn = 1024
    n_plain = n_in // tn
    vec = pl.BlockSpec((1, d), lambda i, j: (0, 0))
    return pl.pallas_call(
        functools.partial(_in_proj_kernel, n_plain=n_plain),
        out_shape=jax.ShapeDtypeStruct((m, n_in + n_gate), BF16),
        grid=(m // tm, (n_in + n_gate) // tn),
        in_specs=[pl.BlockSpec((tm, d), lambda i, j: (i, 0)), vec, vec, vec,
                  pl.BlockSpec((None, d, tn), lambda i, j: (layer, 0, jnp.minimum(j, n_plain - 1))),
                  pl.BlockSpec((None, d, tn), lambda i, j: (layer, 0, jnp.maximum(j - n_plain, 0))),
                  pl.BlockSpec((None, 1, tn), lambda i, j: (layer, 0, jnp.maximum(j - n_plain, 0)))],
        out_specs=pl.BlockSpec((tm, tn), lambda i, j: (i, j)),
        scratch_shapes=[pltpu.VMEM((tm, d), BF16)],
        compiler_params=_cp("parallel", "arbitrary"),
        name=name,
    )(h, gain, shift, scale, w_in, w_gate, b_gate)


def _mm_res_kernel(x_ref, w_ref, h_ref, g_ref, o_ref):
    acc = jnp.dot(x_ref[...], w_ref[...], preferred_element_type=F32)
    o_ref[...] = h_ref[...] + g_ref[...] * acc


def mm_residual(x, w, layer, h, gate, name, in_place):
    m, k = x.shape
    n = w.shape[2]
    tm = min(m, 1024)
    tn = 512
    return pl.pallas_call(
        _mm_res_kernel,
        out_shape=jax.ShapeDtypeStruct((m, n), F32),
        grid=(m // tm, n // tn),
        in_specs=[pl.BlockSpec((tm, k), lambda i, j: (i, 0)),
                  pl.BlockSpec((None, k, tn), lambda i, j: (layer, 0, j)),
                  pl.BlockSpec((tm, tn), lambda i, j: (i, j)),
                  pl.BlockSpec((1, tn), lambda i, j: (0, j))],
        out_specs=pl.BlockSpec((tm, tn), lambda i, j: (i, j)),
        input_output_aliases={2: 0} if in_place else {},
        compiler_params=_cp("parallel", "arbitrary"),
        name=name,
    )(x, w, h, gate)


def _merge_kernel(yc, yf, yn, yg, wc, wf, wn, wg, g0, g1, g2, g3, o_ref):
    def term(y, w, g):
        return g[...].astype(F32) * jnp.dot(y[...], w[...], preferred_element_type=F32)

    o_ref[...] = (term(yc, wc, g0) + term(yf, wf, g1) + term(yn, wn, g2)
                  + term(yg, wg, g3)).astype(o_ref.dtype)


def merge_branches(ys, ws, layer, zg):
    m = zg.shape[0]
    tm = min(m, 1024)
    tn = 512
    y_specs = [pl.BlockSpec((tm, y.shape[1]), lambda i, j: (i, 0)) for y in ys]
    w_specs = [pl.BlockSpec((None, w.shape[1], tn), lambda i, j: (layer, 0, j)) for w in ws]
    g_specs = [pl.BlockSpec((tm, tn), functools.partial(
        lambda i, j, b: (i, (_C_GATE + b * D_MODEL) // tn + j), b=b)) for b in range(N_BRANCH)]
    return pl.pallas_call(
        _merge_kernel,
        out_shape=jax.ShapeDtypeStruct((m, D_MODEL), BF16),
        grid=(m // tm, D_MODEL // tn),
        in_specs=y_specs + w_specs + g_specs,
        out_specs=pl.BlockSpec((tm, tn), lambda i, j: (i, j)),
        compiler_params=_cp("parallel", "arbitrary"),
        name="merge",
    )(*ys, *ws, zg, zg, zg, zg)


def _conv3(p, prev_row, next_row, w_ref):
    tm = p.shape[0]
    row = lax.broadcasted_iota(jnp.int32, p.shape, 0)
    up = jnp.where(row == 0, prev_row, pltpu.roll(p, 1, 0))
    dn = jnp.where(row == tm - 1, next_row, pltpu.roll(p, tm - 1, 0))
    return up * w_ref[0:1, :] + p * w_ref[1:2, :] + dn * w_ref[2:3, :]


def _halo_specs(tm, tc, m, col):
    nb = m // HALO
    per = tm // HALO
    main = pl.BlockSpec((tm, tc), lambda i: (i, col))
    prev = pl.BlockSpec((HALO, tc), lambda i: (jnp.maximum(i * per - 1, 0), col))
    nxt = pl.BlockSpec((HALO, tc), lambda i: (jnp.minimum((i + 1) * per, nb - 1), col))
    return main, prev, nxt


def _convmix_kernel(xa, xap, xan, cg, cgp, cgn, bg, w_ref, o_ref):
    i = pl.program_id(0)
    last = pl.num_programs(0) - 1
    p = cg[...].astype(F32) * xa[...].astype(F32)
    pp = cgp[...].astype(F32)[HALO - 1:HALO, :] * xap[...].astype(F32)[HALO - 1:HALO, :]
    pn = cgn[...].astype(F32)[0:1, :] * xan[...].astype(F32)[0:1, :]
    pp = jnp.where(i > 0, pp, 0.0)
    pn = jnp.where(i < last, pn, 0.0)
    o_ref[...] = (bg[...].astype(F32) * _conv3(p, pp, pn, w_ref)).astype(o_ref.dtype)


def conv_mixer(zg, w, layer):
    m = zg.shape[0]
    tm = min(m, 1024)
    tc = CONV_W
    xa = _halo_specs(tm, tc, m, _C_XA // tc)
    cg = _halo_specs(tm, tc, m, _C_CG // tc)
    bg = pl.BlockSpec((tm, tc), lambda i: (i, _C_BG // tc))
    return pl.pallas_call(
        _convmix_kernel,
        out_shape=jax.ShapeDtypeStruct((m, CONV_W), BF16),
        grid=(m // tm,),
        in_specs=[*xa, *cg, bg, pl.BlockSpec((None, 3, tc), lambda i: (layer, 0, 0))],
        out_specs=pl.BlockSpec((tm, tc), lambda i: (i, 0)),
        compiler_params=_cp("parallel"),
        name="conv_mixer",
    )(zg, zg, zg, zg, zg, zg, zg, w)


def _ffn_up_act_kernel(h_ref, hp_ref, hn_ref, g_ref, sh_ref, sc_ref, wa_ref, wg_ref, cwa_ref, cwg_ref,
                       o_ref, xn_ref):
    i = pl.program_id(0)
    j = pl.program_id(1)
    last = pl.num_programs(0) - 1
    tm = h_ref.shape[0]
    rows = tm + 2 * HALO

    cm = min(tm, NORM_CHUNK)
    nch = tm // cm

    def finish(ua, ug):
        def conv(u, cw_ref):
            c = (pltpu.roll(u, 1, 0) * cw_ref[0:1, :] + u * cw_ref[1:2, :]
                 + pltpu.roll(u, rows - 1, 0) * cw_ref[2:3, :])
            return c[HALO:HALO + tm]

        ca = conv(ua, cwa_ref)
        cgt = conv(ug, cwg_ref)
        o_ref[...] = (ca * _sigmoid(ca) * cgt).astype(o_ref.dtype)

    up = lambda x, w_ref: jnp.dot(x, w_ref[...], preferred_element_type=F32)

    @pl.when(j == 0)
    def _():
        nm = lambda x: _norm_mod(x, g_ref[...], sh_ref[...], sc_ref[...])
        xn_ref[0:HALO, :] = jnp.where(i > 0, nm(hp_ref[...]), 0.0).astype(BF16)
        uas, ugs = [], []
        for c in range(nch):
            lo, hi = c * cm, (c + 1) * cm
            xn_ref[HALO + lo:HALO + hi, :] = nm(h_ref[lo:hi, :]).astype(BF16)
            if c == nch - 1:
                xn_ref[HALO + tm:rows, :] = jnp.where(i < last, nm(hn_ref[...]), 0.0).astype(BF16)
            r0 = 0 if c == 0 else HALO + lo
            r1 = rows if c == nch - 1 else HALO + hi
            xc = xn_ref[r0:r1, :]
            uas.append(up(xc, wa_ref))
            ugs.append(up(xc, wg_ref))
        cat = lambda parts: parts[0] if len(parts) == 1 else jnp.concatenate(parts, axis=0)
        finish(cat(uas), cat(ugs))

    @pl.when(j > 0)
    def _():
        xn = xn_ref[...]
        finish(up(xn, wa_ref), up(xn, wg_ref))


def ffn_up_act(h, gain, shift, scale, w_up, conv_w, layer, name):
    m, d = h.shape
    tm = min(m, 1024)
    tn = 512
    nj = D_FF // tn
    nb = m // HALO
    per = tm // HALO
    vec = pl.BlockSpec((1, d), lambda i, j: (0, 0))
    return pl.pallas_call(
        _ffn_up_act_kernel,
        out_shape=jax.ShapeDtypeStruct((m, D_FF), BF16),
        grid=(m // tm, nj),
        in_specs=[pl.BlockSpec((tm, d), lambda i, j: (i, 0)),
                  pl.BlockSpec((HALO, d), lambda i, j: (jnp.maximum(i * per - 1, 0), 0)),
                  pl.BlockSpec((HALO, d), lambda i, j: (jnp.minimum((i + 1) * per, nb - 1), 0)),
                  vec, vec, vec,
                  pl.BlockSpec((None, d, tn), lambda i, j: (layer, 0, j)),
                  pl.BlockSpec((None, d, tn), lambda i, j: (layer, 0, nj + j)),
                  pl.BlockSpec((None, 3, tn), lambda i, j: (layer, 0, j)),
                  pl.BlockSpec((None, 3, tn), lambda i, j: (layer, 0, nj + j))],
        out_specs=pl.BlockSpec((tm, tn), lambda i, j: (i, j)),
        scratch_shapes=[pltpu.VMEM((tm + 2 * HALO, d), BF16)],
        compiler_params=_cp("parallel", "arbitrary"),
        name=name,
    )(h, h, h, gain, shift, scale, w_up, w_up, conv_w, conv_w)


def _head(z_ref, h, gain_ref=None, rope=None, scale=1.0):
    x = z_ref[:, h * HEAD_DIM:(h + 1) * HEAD_DIM].astype(F32)
    if gain_ref is not None:
        ms = jnp.mean(x * x, axis=-1, keepdims=True)
        x = x * lax.rsqrt(ms + EPS) * gain_ref[...]
    if rope is not None:
        x = x * rope[0][...] + pltpu.roll(x, HEAD_DIM // 2, 1) * rope[1][...]
    if scale != 1.0:
        x = x * scale
    return x


def _prep_latent_kernel(nq_ref, nk_ref, zqa_ref, zqb_ref, zk_ref, zv_ref, nqg, nkg, gqg, gkg, cos_ref, sin_ref,
                        kin_ref, vin_ref, nq_o, nk_o, gqT_o, k_o, vT_o, *, qscale):
    del kin_ref, vin_ref
    rope = (cos_ref, sin_ref)
    for h in range(NA_HEADS):
        sl = slice(h * HEAD_DIM, (h + 1) * HEAD_DIM)
        nq_o[:, sl] = _head(nq_ref, h, nqg, scale=qscale).astype(BF16)
        nk_o[:, sl] = _head(nk_ref, h, nkg).astype(BF16)
    for h in range(GQA_Q_HEADS):
        src = zqa_ref if h < GQA_Q_HEADS // 2 else zqb_ref
        gqT_o[h] = _head(src, h % (GQA_Q_HEADS // 2), gqg, rope, qscale).T.astype(BF16)
    for h in range(GQA_KV_HEADS):
        k_o[h] = _head(zk_ref, h, gkg, rope).astype(BF16)
        vT_o[h] = _head(zv_ref, h).T.astype(BF16)


def prep_latent(zg, gains, rope, kbuf, vTbuf, qscale):
    m = zg.shape[0]
    tm = 1024
    col = lambda c0, w: pl.BlockSpec((tm, w), lambda i: (i, c0 // w))
    gain = pl.BlockSpec((1, HEAD_DIM), lambda i: (0, 0))
    tab = pl.BlockSpec((tm, HEAD_DIM), lambda i: (i, 0))
    anyspec = pl.BlockSpec(memory_space=pl.ANY)
    return pl.pallas_call(
        functools.partial(_prep_latent_kernel, qscale=qscale),
        out_shape=(jax.ShapeDtypeStruct((m, NA_W), BF16), jax.ShapeDtypeStruct((m, NA_W), BF16),
                   jax.ShapeDtypeStruct((GQA_Q_HEADS, HEAD_DIM, m), BF16),
                   jax.ShapeDtypeStruct(kbuf.shape, BF16), jax.ShapeDtypeStruct(vTbuf.shape, BF16)),
        grid=(m // tm,),
        in_specs=[col(_C_NQ, NA_W), col(_C_NK, NA_W), col(_C_ZQ, GQA_W // 2), col(_C_ZQ + GQA_W // 2, GQA_W // 2),
                  col(_C_ZK, GQA_KV_W), col(_C_ZV, GQA_KV_W), gain, gain, gain, gain, tab, tab,
                  anyspec, anyspec],
        out_specs=(pl.BlockSpec((tm, NA_W), lambda i: (i, 0)), pl.BlockSpec((tm, NA_W), lambda i: (i, 0)),
                   pl.BlockSpec((GQA_Q_HEADS, HEAD_DIM, tm), lambda i: (0, 0, i)),
                   pl.BlockSpec((GQA_KV_HEADS, tm, HEAD_DIM), lambda i: (0, i, 0)),
                   pl.BlockSpec((GQA_KV_HEADS, HEAD_DIM, tm), lambda i: (0, 0, i))),
        input_output_aliases={12: 3, 13: 4},
        compiler_params=_cp("parallel"),
        name="prep_latent",
    )(zg, zg, zg, zg, zg, zg, *gains, *rope, kbuf, vTbuf)


def _prep_ctx_kernel(nq_ref, nk_ref, nv_ref, zqa_ref, zqb_ref, zk_ref, zv_ref, nqg, nkg, gqg, gkg,
                     nk_o, k_o, vT_o, cnqT_o, cnk_o, cnvT_o, cqT_o, ck_o, cvT_o, *, qscale):
    for h in range(NA_HEADS):
        sl = slice(h * HEAD_DIM, (h + 1) * HEAD_DIM)
        kn = _head(nk_ref, h, nkg).astype(BF16)
        nk_o[:, sl] = kn
        cnk_o[h] = kn
        cnqT_o[h] = _head(nq_ref, h, nqg, scale=qscale).T.astype(BF16)
        cnvT_o[h] = _head(nv_ref, h).T.astype(BF16)
    for h in range(GQA_Q_HEADS):
        src = zqa_ref if h < GQA_Q_HEADS // 2 else zqb_ref
        cqT_o[h] = _head(src, h % (GQA_Q_HEADS // 2), gqg, scale=qscale).T.astype(BF16)
    for h in range(GQA_KV_HEADS):
        k = _head(zk_ref, h, gkg).astype(BF16)
        vT = _head(zv_ref, h).T.astype(BF16)
        k_o[h] = k
        ck_o[h] = k
        vT_o[h] = vT
        cvT_o[h] = vT


def prep_ctx(czg, gains, qscale):
    m = czg.shape[0]
    col = lambda c0, w: pl.BlockSpec((m, w), lambda i: (0, c0 // w))
    gain = pl.BlockSpec((1, HEAD_DIM), lambda i: (0, 0))
    full = lambda shape: pl.BlockSpec(shape, lambda i: (0,) * len(shape))
    blk = SEQ // m
    return pl.pallas_call(
        functools.partial(_prep_ctx_kernel, qscale=qscale),
        out_shape=(jax.ShapeDtypeStruct((m, NA_W), BF16),
                   jax.ShapeDtypeStruct((GQA_KV_HEADS, N_KEYS, HEAD_DIM), BF16),
                   jax.ShapeDtypeStruct((GQA_KV_HEADS, HEAD_DIM, N_KEYS), BF16),
                   jax.ShapeDtypeStruct((NA_HEADS, HEAD_DIM, m), BF16),
                   jax.ShapeDtypeStruct((NA_HEADS, m, HEAD_DIM), BF16),
                   jax.ShapeDtypeStruct((NA_HEADS, HEAD_DIM, m), BF16),
                   jax.ShapeDtypeStruct((GQA_Q_HEADS, HEAD_DIM, m), BF16),
                   jax.ShapeDtypeStruct((GQA_KV_HEADS, m, HEAD_DIM), BF16),
                   jax.ShapeDtypeStruct((GQA_KV_HEADS, HEAD_DIM, m), BF16)),
        grid=(1,),
        in_specs=[col(_C_NQ, NA_W), col(_C_NK, NA_W), col(_C_NV, NA_W), col(_C_ZQ, GQA_W // 2),
                  col(_C_ZQ + GQA_W // 2, GQA_W // 2), col(_C_ZK, GQA_KV_W), col(_C_ZV, GQA_KV_W),
                  gain, gain, gain, gain],
        out_specs=(full((m, NA_W)),
                   pl.BlockSpec((GQA_KV_HEADS, m, HEAD_DIM), lambda i: (0, blk, 0)),
                   pl.BlockSpec((GQA_KV_HEADS, HEAD_DIM, m), lambda i: (0, 0, blk)),
                   full((NA_HEADS, HEAD_DIM, m)), full((NA_HEADS, m, HEAD_DIM)), full((NA_HEADS, HEAD_DIM, m)),
                   full((GQA_Q_HEADS, HEAD_DIM, m)), full((GQA_KV_HEADS, m, HEAD_DIM)),
                   full((GQA_KV_HEADS, HEAD_DIM, m))),
        compiler_params=_cp("arbitrary"),
        name="prep_ctx",
    )(czg, czg, czg, czg, czg, czg, czg, *gains)


SCORE_BOUND_SAFE = 60.0
FLASH_TQ = 512
FLASH_UNITS = 16


def _flash_kernel(qT_ref, k_ref, vT_ref, o_ref, kmax_ref, acc_ref, *, group, tk, n_keys):
    tq = min(qT_ref.shape[2], FLASH_TQ)
    nsub = qT_ref.shape[2] // tq
    cols = [(h, sub) for h in range(group) for sub in range(nsub)]
    n_full = n_keys // tk
    rem = n_keys - n_full * tk

    @pl.when(pl.program_id(1) == 0)
    def _():
        def ksq(start, size):
            kb = k_ref[0, pl.ds(start, size), :].astype(F32)
            return jnp.max(jnp.sum(kb * kb, axis=-1, keepdims=True), axis=0, keepdims=True)

        mx = jnp.zeros((1, 1), F32)
        if n_full > 0:
            mx = lax.fori_loop(
                0, n_full, lambda i, c: jnp.maximum(c, ksq(pl.multiple_of(i * tk, tk), tk)), mx)
        if rem > 0:
            mx = jnp.maximum(mx, ksq(n_full * tk, rem))
        kmax_ref[...] = mx

    def run_blocks(step, carry):
        if n_full > 0:
            carry = lax.fori_loop(
                0, n_full, lambda kb, c: step(pl.multiple_of(kb * tk, tk), tk, c), carry)
        if rem > 0:
            carry = step(n_full * tk, rem, carry)
        return carry

    qTs = [qT_ref[h, :, sub * tq:(sub + 1) * tq] for h, sub in cols]
    shifts = []
    for qT in qTs:
        qf = qT.astype(F32)
        qsq = jnp.sum(qf * qf, axis=0, keepdims=True)
        shifts.append(jnp.sqrt(qsq * kmax_ref[...]) * 1.01)
    safe = jnp.max(functools.reduce(jnp.maximum, shifts)) <= SCORE_BOUND_SAFE

    @pl.when(safe)
    def _():
        def step(start, size, l8s, nb=1):
            units = [(j, c) for j in range(nb) for c in range(len(cols))]
            kblks = [k_ref[0, pl.ds(start + j * size, size), :] for j in range(nb)]
            vblks = [vT_ref[0, :, pl.ds(start + j * size, size)] for j in range(nb)]
            l8s = list(l8s)
            s_next = jnp.dot(kblks[0], qTs[0], preferred_element_type=F32)
            for u, (j, c) in enumerate(units):
                s = s_next
                if u + 1 < len(units):
                    jn, cn = units[u + 1]
                    s_next = jnp.dot(kblks[jn], qTs[cn], preferred_element_type=F32)
                p = jnp.exp2(s - shifts[c])
                l8s[c] = l8s[c] + jnp.sum(p.reshape(size // 8, 8, tq), axis=0)
                acc_ref[c] += jnp.dot(vblks[j], p.astype(BF16), preferred_element_type=F32)
            return tuple(l8s)

        acc_ref[...] = jnp.zeros_like(acc_ref)
        l8s = tuple(jnp.zeros((8, tq), F32) for _ in cols)
        unroll = max(1, FLASH_UNITS // len(cols))
        n_body = n_full // unroll
        if n_body > 0:
            l8s = lax.fori_loop(
                0, n_body,
                lambda kb, c: step(pl.multiple_of(kb * (tk * unroll), tk * unroll), tk, c, nb=unroll), l8s)
        for kb in range(n_body * unroll, n_full):
            l8s = step(kb * tk, tk, l8s)
        if rem > 0:
            l8s = step(n_full * tk, rem, l8s)
        for c in range(len(cols)):
            l = jnp.sum(l8s[c], axis=0, keepdims=True)
            acc_ref[c] = acc_ref[c] * (1.0 / l)

    @pl.when(jnp.logical_not(safe))
    def _():
        for c in range(len(cols)):
            def step(start, size, carry, qT=qTs[c]):
                m, l, acc = carry
                kblk = k_ref[0, pl.ds(start, size), :]
                s = jnp.dot(kblk, qT, preferred_element_type=F32)
                m_new = jnp.maximum(m, jnp.max(s, axis=0, keepdims=True))
                alpha = jnp.exp2(m - m_new)
                p = jnp.exp2(s - m_new)
                l = alpha * l + jnp.sum(p, axis=0, keepdims=True)
                vblk = vT_ref[0, :, pl.ds(start, size)]
                acc = alpha * acc + jnp.dot(vblk, p.astype(BF16), preferred_element_type=F32)
                return m_new, l, acc

            carry = (jnp.full((1, tq), NEG_BIG, F32), jnp.zeros((1, tq), F32),
                     jnp.zeros((HEAD_DIM, tq), F32))
            _, l, acc = run_blocks(step, carry)
            acc_ref[c] = acc * (1.0 / l)

    for c, (h, sub) in enumerate(cols):
        o_ref[sub * tq:(sub + 1) * tq, h * HEAD_DIM:(h + 1) * HEAD_DIM] = acc_ref[c].T.astype(o_ref.dtype)


def flash_attention(qT, k, vT, tq):
    hq, _, mq = qT.shape
    hkv, nk, _ = k.shape
    group = hq // hkv
    kern = functools.partial(_flash_kernel, group=group, tk=512, n_keys=nk)
    return pl.pallas_call(
        kern,
        out_shape=jax.ShapeDtypeStruct((mq, hq * HEAD_DIM), BF16),
        grid=(hkv, mq // tq),
        in_specs=[pl.BlockSpec((group, HEAD_DIM, tq), lambda g, i: (g, 0, i)),
                  pl.BlockSpec((1, nk, HEAD_DIM), lambda g, i: (g, 0, 0)),
                  pl.BlockSpec((1, HEAD_DIM, nk), lambda g, i: (g, 0, 0))],
        out_specs=pl.BlockSpec((tq, group * HEAD_DIM), lambda g, i: (i, g)),
        scratch_shapes=[pltpu.VMEM((1, 1), F32),
                        pltpu.VMEM((group * max(1, tq // FLASH_TQ), HEAD_DIM, min(tq, FLASH_TQ)), F32)],
        compiler_params=_cp("parallel", "arbitrary"),
        name="flash_attention",
    )(qT, k, vT)


NA_RB = 8
NA_TOK = NA_RB * GRID_W
NA_PAIR = 2 * GRID_W
NA_SLAB_ROWS = NA_ROWS + 2
NA_SLAB = NA_SLAB_ROWS * GRID_W
NA_PATTERNS = 5


def _na_kernel(q_ref, kp, kc, kn, vp, vc, vn, ck_ref, cv_ref, bias_ref, o_ref, kbuf, vbuf):
    b = pl.program_id(0)
    rows = pl.num_programs(0) * NA_RB
    kbuf[0:NA_TOK, :] = kp[...]
    kbuf[NA_TOK:2 * NA_TOK, :] = kc[...]
    kbuf[2 * NA_TOK:3 * NA_TOK, :] = kn[...]
    vbuf[0:NA_TOK, :] = vp[...]
    vbuf[NA_TOK:2 * NA_TOK, :] = vc[...]
    vbuf[2 * NA_TOK:3 * NA_TOK, :] = vn[...]
    nt = (((1,), (1,)), ((), ()))

    def slab(pair):
        r = b * NA_RB + 2 * pair
        u0 = jnp.clip(r - NA_ROWS // 2, 0, rows - NA_ROWS)
        off = pl.multiple_of((u0 - b * NA_RB + NA_RB) * GRID_W, NA_PAIR)
        pat = jnp.where(r < NA_ROWS // 2, r // 2,
                        jnp.where(r >= rows - NA_ROWS // 2, 3 + (r - (rows - NA_ROWS // 2)) // 2, 2))
        return off, pat

    slabs = [slab(pair) for pair in range(NA_RB // 2)]
    units = [(pair, h) for pair in range(NA_RB // 2) for h in range(NA_HEADS)]

    def scores(pair, h):
        off, pat = slabs[pair]
        sl = slice(h * HEAD_DIM, (h + 1) * HEAD_DIM)
        qh = q_ref[pair * NA_PAIR:(pair + 1) * NA_PAIR, sl]
        s = lax.dot_general(qh, kbuf[pl.ds(off, NA_SLAB), sl], nt, preferred_element_type=F32)
        sc = lax.dot_general(qh, ck_ref[:, sl], nt, preferred_element_type=F32)
        return s + bias_ref[pat, h], sc

    nxt = scores(*units[0])
    for u, (pair, h) in enumerate(units):
        s, sc = nxt
        if u + 1 < len(units):
            nxt = scores(*units[u + 1])
        off, _ = slabs[pair]
        sl = slice(h * HEAD_DIM, (h + 1) * HEAD_DIM)
        m = jnp.maximum(jnp.max(s, axis=-1, keepdims=True), jnp.max(sc, axis=-1, keepdims=True))
        p = jnp.exp2(s - m)
        pc = jnp.exp2(sc - m)
        l = jnp.sum(p, axis=-1, keepdims=True) + jnp.sum(pc, axis=-1, keepdims=True)
        o = (jnp.dot(p.astype(BF16), vbuf[pl.ds(off, NA_SLAB), sl], preferred_element_type=F32)
             + jnp.dot(pc.astype(BF16), cv_ref[:, sl], preferred_element_type=F32))
        o_ref[pair * NA_PAIR:(pair + 1) * NA_PAIR, sl] = (o * (1.0 / l)).astype(o_ref.dtype)


def neighborhood_attention(qn, kn, zg, ckn, czg, bias):
    m = qn.shape[0]
    nb = m // NA_TOK
    vcol = _C_NV // NA_W
    blk = (NA_TOK, NA_W)
    return pl.pallas_call(
        _na_kernel,
        out_shape=jax.ShapeDtypeStruct((m, NA_W), BF16),
        grid=(nb,),
        in_specs=[pl.BlockSpec(blk, lambda b: (b, 0)),
                  pl.BlockSpec(blk, lambda b: (jnp.maximum(b - 1, 0), 0)),
                  pl.BlockSpec(blk, lambda b: (b, 0)),
                  pl.BlockSpec(blk, lambda b: (jnp.minimum(b + 1, nb - 1), 0)),
                  pl.BlockSpec(blk, lambda b: (jnp.maximum(b - 1, 0), vcol)),
                  pl.BlockSpec(blk, lambda b: (b, vcol)),
                  pl.BlockSpec(blk, lambda b: (jnp.minimum(b + 1, nb - 1), vcol)),
                  pl.BlockSpec((CTX_LEN, NA_W), lambda b: (0, 0)),
                  pl.BlockSpec((CTX_LEN, NA_W), lambda b: (0, vcol)),
                  pl.BlockSpec((NA_PATTERNS, NA_HEADS, NA_PAIR, NA_SLAB), lambda b: (0, 0, 0, 0))],
        out_specs=pl.BlockSpec(blk, lambda b: (b, 0)),
        scratch_shapes=[pltpu.VMEM((3 * NA_TOK, NA_W), BF16), pltpu.VMEM((3 * NA_TOK, NA_W), BF16)],
        compiler_params=_cp("arbitrary"),
        name="neighborhood_attention",
    )(qn, kn, kn, kn, zg, zg, zg, ckn, czg, bias)


def na_bias_table(rpb):
    col = np.arange(GRID_W)
    c0 = np.clip(col - NA_COLS // 2, 0, GRID_W - NA_COLS)
    kc = np.arange(GRID_W)
    inside = (kc[None, :] >= c0[:, None]) & (kc[None, :] < c0[:, None] + NA_COLS)
    dc = kc[None, :] - col[:, None] + NA_COLS - 1
    onehot = (dc[None] == np.arange(2 * NA_COLS - 1)[:, None, None]) & inside[None]
    t = jnp.einsum('hrd,dck->hrck', rpb.astype(F32) * LOG2E, jnp.asarray(onehot, F32),
                   precision=lax.Precision.HIGHEST)
    t = jnp.where(inside[None, None], t, NEG_BIG)
    masked = jnp.full((NA_HEADS, GRID_W, GRID_W), NEG_BIG, F32)
    patterns = [(0, 0), (2, 0), (4, 1), (4, 0), (6, 0)]
    tabs = []
    for a, e in patterns:
        per_q = []
        for start, dr0 in ((0, NA_ROWS - 1 - a), (e, NA_ROWS - 2 - a)):
            blocks = [t[:, w + dr0] if start <= w < start + NA_ROWS else masked for w in range(NA_SLAB_ROWS)]
            per_q.append(jnp.stack(blocks, axis=2).reshape(NA_HEADS, GRID_W, NA_SLAB))
        tabs.append(jnp.concatenate(per_q, axis=1))
    return jnp.stack(tabs, axis=0)


def _dft_mats(n):
    a = 2.0 * np.pi * np.outer(np.arange(n), np.arange(n)) / n
    return np.cos(a), np.sin(a)


def _chan_dft_kernel(z_ref, w_ref, o_ref):
    for g in range(FOURIER_GROUPS):
        x = z_ref[:, g * HEAD_DIM:(g + 1) * HEAD_DIM]
        r = jnp.dot(x, w_ref[...], preferred_element_type=F32)
        o_ref[0, :, g * HEAD_DIM:(g + 1) * HEAD_DIM] = r[:, :HEAD_DIM].astype(o_ref.dtype)
        o_ref[1, :, g * HEAD_DIM:(g + 1) * HEAD_DIM] = r[:, HEAD_DIM:].astype(o_ref.dtype)


def chan_dft(zg, wch):
    m = zg.shape[0]
    tm = min(m, 1024)
    return pl.pallas_call(
        _chan_dft_kernel,
        out_shape=jax.ShapeDtypeStruct((2, m, FOURIER_W), BF16),
        grid=(m // tm,),
        in_specs=[pl.BlockSpec((tm, FOURIER_W), lambda i: (i, _C_ZF // FOURIER_W)),
                  pl.BlockSpec((HEAD_DIM, 2 * HEAD_DIM), lambda i: (0, 0))],
        out_specs=pl.BlockSpec((2, tm, FOURIER_W), lambda i: (0, i, 0)),
        compiler_params=_cp("parallel"),
        name="chan_dft",
    )(zg, wch)


def _left_mm_kernel(l_ref, x_ref, o_ref, *, n_out):
    x = jnp.concatenate([x_ref[0], x_ref[1]], axis=0)
    r = jnp.dot(l_ref[...], x, preferred_element_type=F32)
    if n_out == 1:
        o_ref[...] = r.astype(o_ref.dtype)
    else:
        half = r.shape[0] // 2
        o_ref[0] = r[:half].astype(o_ref.dtype)
        o_ref[1] = r[half:].astype(o_ref.dtype)


def dft_stage1(zc, lmat):
    _, r, n = zc.shape
    tn = 4096
    return pl.pallas_call(
        functools.partial(_left_mm_kernel, n_out=2),
        out_shape=jax.ShapeDtypeStruct((2, r, n), BF16),
        grid=(n // tn,),
        in_specs=[pl.BlockSpec((2 * r, 2 * r), lambda j: (0, 0)),
                  pl.BlockSpec((2, r, tn), lambda j: (0, 0, j))],
        out_specs=pl.BlockSpec((2, r, tn), lambda j: (0, 0, j)),
        compiler_params=_cp("parallel"),
        name="dft_stage1",
    )(lmat, zc)


def _dft_stage3_kernel(l_ref, x_ref, o_ref, *, kb):
    for i in range(kb):
        x = jnp.concatenate([x_ref[0, i], x_ref[1, i]], axis=0)
        o_ref[i] = jnp.dot(l_ref[i], x, preferred_element_type=F32).astype(o_ref.dtype)


def dft_stage3(a, tables):
    _, n1, n2, c = a.shape
    kb = 8
    return pl.pallas_call(
        functools.partial(_dft_stage3_kernel, kb=kb),
        out_shape=jax.ShapeDtypeStruct((n1, n2, c), BF16),
        grid=(n1 // kb,),
        in_specs=[pl.BlockSpec((kb, n2, 2 * n2), lambda j: (j, 0, 0)),
                  pl.BlockSpec((2, kb, n2, c), lambda j: (0, j, 0, 0))],
        out_specs=pl.BlockSpec((kb, n2, c), lambda j: (j, 0, 0)),
        compiler_params=_cp("parallel"),
        name="dft_stage3",
    )(tables, a)


def dft_direct(zc, lmat):
    _, t, c = zc.shape
    return pl.pallas_call(
        functools.partial(_left_mm_kernel, n_out=1),
        out_shape=jax.ShapeDtypeStruct((t, c), BF16),
        grid=(1,),
        in_specs=[pl.BlockSpec((t, 2 * t), lambda j: (0, 0)),
                  pl.BlockSpec((2, t, c), lambda j: (0, 0, 0))],
        out_specs=pl.BlockSpec((t, c), lambda j: (0, 0)),
        compiler_params=_cp("arbitrary"),
        name="dft_direct",
    )(lmat, zc)


def _fourier_tables():
    c128, s128 = _dft_mats(HEAD_DIM)
    wch = np.concatenate([c128, -s128], axis=1)
    n1 = SEQ // HEAD_DIM
    l1 = np.block([[c128, s128], [-s128, c128]])
    k1 = np.arange(n1)[:, None, None]
    k2 = np.arange(n1)[None, :, None]
    t2 = np.arange(n1)[None, None, :]
    ang = 2.0 * np.pi * ((n1 * k2 + k1) * t2 % SEQ) / SEQ
    norm = 1.0 / math.sqrt(SEQ * HEAD_DIM)
    l3 = np.concatenate([np.cos(ang), np.sin(ang)], axis=2) * norm
    cc, sc = _dft_mats(CTX_LEN)
    lc = np.concatenate([cc, sc], axis=1) / math.sqrt(CTX_LEN * HEAD_DIM)
    as_bf = lambda a: jnp.asarray(a, F32).astype(BF16)
    return as_bf(wch), as_bf(l1), as_bf(l3), as_bf(lc)


def fourier_latent(zg, tabs):
    wch, l1, l3, _ = tabs
    n1 = SEQ // HEAD_DIM
    zc = chan_dft(zg, wch)
    a = dft_stage1(zc.reshape(2, n1, n1 * FOURIER_W), l1)
    o3 = dft_stage3(a.reshape(2, n1, n1, FOURIER_W), l3)
    return o3.transpose(1, 0, 2).reshape(SEQ, FOURIER_W)


def fourier_ctx(czg, tabs):
    wch, _, _, lc = tabs
    return dft_direct(chan_dft(czg, wch), lc)


def _rope_tables(n_tok):
    t = jnp.arange(n_tok)
    row = (t // GRID_W).astype(F32)
    col = (t % GRID_W).astype(F32)
    n_freq = HEAD_DIM // 4
    inv_freq = ROPE_THETA ** (-jnp.arange(n_freq, dtype=F32) / n_freq)
    ang = jnp.concatenate([row[:, None] * inv_freq, col[:, None] * inv_freq], axis=-1)
    cos, sin = jnp.cos(ang), jnp.sin(ang)
    return jnp.concatenate([cos, cos], axis=-1), jnp.concatenate([-sin, sin], axis=-1)


def kernel(x, c, ctx, c_ctx, w_ada, b_ada, norm1, w_in, conv_w, na_q_gain, na_k_gain, na_rpb, gqa_q_gain, gqa_k_gain, w_conv_out, w_fourier_out, w_na_out, w_gqa_out, w_gate, b_gate, w_o, norm2, w_up, ffn_conv_w, w_down):
    d = D_MODEL
    h = x[0]
    hc = ctx[0]
    rope = _rope_tables(SEQ)
    ftabs = _fourier_tables()
    qscale2 = HEAD_DIM ** -0.5 * LOG2E
    cvec = jnp.zeros((8, d), F32).at[0].set(c[0]).at[1].set(c_ctx)
    row = lambda v: v.reshape(1, -1)

    w_in_b, w_gate_b = w_in.astype(BF16), w_gate.astype(BF16)
    merge_w = [w.astype(BF16) for w in (w_conv_out, w_fourier_out, w_na_out, w_gqa_out)]
    w_o_b, w_up_b, w_down_b = w_o.astype(BF16), w_up.astype(BF16), w_down.astype(BF16)
    b_ada3 = b_ada.reshape(DEPTH, 1, -1)
    b_gate3 = b_gate.reshape(DEPTH, 1, -1)

    for i in range(DEPTH):
        last = i == DEPTH - 1
        bias_tab = na_bias_table(na_rpb[i])
        gains = (row(na_q_gain[i]), row(na_k_gain[i]), row(gqa_q_gain[i]), row(gqa_k_gain[i]))

        mods = ada_mod(cvec, w_ada, b_ada3, i)
        lat = [mods[0:1, k * d:(k + 1) * d] for k in range(6)]
        cm = [mods[1:2, k * d:(k + 1) * d] for k in range(6)]

        zg = in_proj(h, row(norm1[i]), lat[0], lat[1], w_in_b, w_gate_b, b_gate3, i, "in_proj")
        czg = in_proj(hc, row(norm1[i]), cm[0], cm[1], w_in_b, w_gate_b, b_gate3, i, "in_proj_ctx")

        cnk_rows, kbuf, vTbuf, cnqT, cnk, cnvT, cqT, ck, cvT = prep_ctx(czg, gains, qscale2)
        nq, nk, gqT, k_all, vT_all = prep_latent(zg, gains, rope, kbuf, vTbuf, qscale2)

        y_conv = conv_mixer(zg, conv_w, i)
        y_four = fourier_latent(zg, ftabs)
        y_na = neighborhood_attention(nq, nk, zg, cnk_rows, czg, bias_tab)
        y_gqa = flash_attention(gqT, k_all, vT_all, tq=1024)

        merged = merge_branches([y_conv, y_four, y_na, y_gqa], merge_w, i, zg)
        h = mm_residual(merged, w_o_b, i, h, lat[2], "out_proj", in_place=i > 0)
        act = ffn_up_act(h, row(norm2[i]), lat[3], lat[4], w_up_b, ffn_conv_w, i, "ffn_up_act")
        h = mm_residual(act, w_down_b, i, h, lat[5], "ffn_down", in_place=True)

        if not last:
            cy_conv = conv_mixer(czg, conv_w, i)
            cy_four = fourier_ctx(czg, ftabs)
            cy_na = flash_attention(cnqT, cnk, cnvT, tq=CTX_LEN)
            cy_gqa = flash_attention(cqT, ck, cvT, tq=CTX_LEN)
            cmerged = merge_branches([cy_conv, cy_four, cy_na, cy_gqa], merge_w, i, czg)
            hc = mm_residual(cmerged, w_o_b, i, hc, cm[2], "out_proj_ctx", in_place=i > 0)
            cact = ffn_up_act(hc, row(norm2[i]), cm[3], cm[4], w_up_b, ffn_conv_w, i, "ffn_up_act_ctx")
            hc = mm_residual(cact, w_down_b, i, hc, cm[5], "ffn_down_ctx", in_place=True)

    return h[None]
```

```python
import functools
import math

import jax
import jax.numpy as jnp
import numpy as np
from jax import lax
from jax.experimental import pallas as pl
from jax.experimental.pallas import tpu as pltpu

D_MODEL = 2048
SEQ = 16384
DEPTH = 4
GRID_W = 64
CTX_LEN = 256
HEAD_DIM = 128
CONV_W = 512
FOURIER_GROUPS = 4
FOURIER_W = 512
NA_HEADS = 4
NA_W = 512
NA_ROWS = 8
NA_COLS = 16
GQA_Q_HEADS = 8
GQA_KV_HEADS = 2
GQA_GROUP = 4
GQA_W = 1024
GQA_KV_W = 256
N_BRANCH = 4
N_IN = 5120
D_FF = 5632
ROPE_THETA = 10000.0
EPS = 1e-6

_C_XA, _C_BG, _C_CG = 0, 512, 1024
_C_ZF = 1536
_C_NQ, _C_NK, _C_NV = 2048, 2560, 3072
_C_ZQ, _C_ZK, _C_ZV = 3584, 4608, 4864
_C_GATE = N_IN

BF16 = jnp.bfloat16
F32 = jnp.float32
V7X_VMEM_LIMIT = 52 * 1024 * 1024
NEG_BIG = -1e30
HALO = 16
NORM_CHUNK = 256
LOG2E = 1.4426950408889634


def _cp(*sem):
    return pltpu.CompilerParams(dimension_semantics=sem, vmem_limit_bytes=V7X_VMEM_LIMIT)


def _sigmoid(t):
    return 1.0 / (1.0 + jnp.exp(-t))


def _norm_mod(x, g, sh, sc):
    ms = jnp.mean(x * x, axis=-1, keepdims=True)
    return (x * lax.rsqrt(ms + EPS) * g) * (1.0 + sc) + sh


def _ada_kernel(c_ref, w_ref, b_ref, o_ref):
    c = c_ref[...]
    s = c * _sigmoid(c)
    o_ref[...] = jnp.dot(s.astype(BF16), w_ref[...].astype(BF16),
                         preferred_element_type=F32) + b_ref[...]


def ada_mod(cvec, w, b, layer):
    n = w.shape[2]
    tn = 1024
    return pl.pallas_call(
        _ada_kernel,
        out_shape=jax.ShapeDtypeStruct((8, n), F32),
        grid=(n // tn,),
        in_specs=[pl.BlockSpec((8, D_MODEL), lambda j: (0, 0)),
                  pl.BlockSpec((None, D_MODEL, tn), lambda j: (layer, 0, j)),
                  pl.BlockSpec((None, 1, tn), lambda j: (layer, 0, j))],
        out_specs=pl.BlockSpec((8, tn), lambda j: (0, j)),
        compiler_params=_cp("parallel"),
        name="ada_mod",
    )(cvec, w, b)


def _in_proj_kernel(h_ref, g_ref, sh_ref, sc_ref, win_ref, wgate_ref, b_ref, o_ref, xn_ref, *, n_plain):
    j = pl.program_id(1)
    tm = h_ref.shape[0]
    cm = min(tm, NORM_CHUNK)

    @pl.when(j == 0)
    def _():
        for c in range(tm // cm):
            rows = slice(c * cm, (c + 1) * cm)
            xc = _norm_mod(h_ref[rows, :], g_ref[...], sh_ref[...], sc_ref[...]).astype(BF16)
            xn_ref[rows, :] = xc
            o_ref[rows, :] = jnp.dot(xc, win_ref[...], preferred_element_type=F32).astype(o_ref.dtype)

    @pl.when(jnp.logical_and(j > 0, j < n_plain))
    def _():
        o_ref[...] = jnp.dot(xn_ref[...], win_ref[...], preferred_element_type=F32).astype(o_ref.dtype)

    @pl.when(j >= n_plain)
    def _():
        t = jnp.dot(xn_ref[...], wgate_ref[...], preferred_element_type=F32) + b_ref[...]
        o_ref[...] = _sigmoid(t).astype(o_ref.dtype)


def in_proj(h, gain, shift, scale, w_in, w_gate, b_gate, layer, name):
    m, d = h.shape
    n_in, n_gate = w_in.shape[2], w_gate.shape[2]
    tm = min(m, 1024)
    tn = 1024
    n_plain = n_in // tn
    vec = pl.BlockSpec((1, d), lambda i, j: (0, 0))
    return pl.pallas_call(
        functools.partial(_in_proj_kernel, n_plain=n_plain),
        out_shape=jax.ShapeDtypeStruct((m, n_in + n_gate), BF16),
        grid=(m // tm, (n_in + n_gate) // tn),
        in_specs=[pl.BlockSpec((tm, d), lambda i, j: (i, 0)), vec, vec, vec,
                  pl.BlockSpec((None, d, tn), lambda i, j: (layer, 0, jnp.minimum(j, n_plain - 1))),
                  pl.BlockSpec((None, d, tn), lambda i, j: (layer, 0, jnp.maximum(j - n_plain, 0))),
                  pl.BlockSpec((None, 1, tn), lambda i, j: (layer, 0, jnp.maximum(j - n_plain, 0)))],
        out_specs=pl.BlockSpec((tm, tn), lambda i, j: (i, j)),
        scratch_shapes=[pltpu.VMEM((tm, d), BF16)],
        compiler_params=_cp("parallel", "arbitrary"),
        name=name,
    )(h, gain, shift, scale, w_in, w_gate, b_gate)


def _mm_res_kernel(x_ref, w_ref, h_ref, g_ref, o_ref):
    acc = jnp.dot(x_ref[...], w_ref[...], preferred_element_type=F32)
    o_ref[...] = h_ref[...] + g_ref[...] * acc


def mm_residual(x, w, layer, h, gate, name, in_place):
    m, k = x.shape
    n = w.shape[2]
    tm = min(m, 1024)
    tn = 512
    return pl.pallas_call(
        _mm_res_kernel,
        out_shape=jax.ShapeDtypeStruct((m, n), F32),
        grid=(m // tm, n // tn),
        in_specs=[pl.BlockSpec((tm, k), lambda i, j: (i, 0)),
                  pl.BlockSpec((None, k, tn), lambda i, j: (layer, 0, j)),
                  pl.BlockSpec((tm, tn), lambda i, j: (i, j)),
                  pl.BlockSpec((1, tn), lambda i, j: (0, j))],
        out_specs=pl.BlockSpec((tm, tn), lambda i, j: (i, j)),
        input_output_aliases={2: 0} if in_place else {},
        compiler_params=_cp("parallel", "arbitrary"),
        name=name,
    )(x, w, h, gate)


def _merge_kernel(yc, yf, yn, yg, wc, wf, wn, wg, g0, g1, g2, g3, o_ref):
    def term(y, w, g):
        return g[...].astype(F32) * jnp.dot(y[...], w[...], preferred_element_type=F32)

    o_ref[...] = (term(yc, wc, g0) + term(yf, wf, g1) + term(yn, wn, g2)
                  + term(yg, wg, g3)).astype(o_ref.dtype)


def merge_branches(ys, ws, layer, zg):
    m = zg.shape[0]
    tm = min(m, 1024)
    tn = 512
    y_specs = [pl.BlockSpec((tm, y.shape[1]), lambda i, j: (i, 0)) for y in ys]
    w_specs = [pl.BlockSpec((None, w.shape[1], tn), lambda i, j: (layer, 0, j)) for w in ws]
    g_specs = [pl.BlockSpec((tm, tn), functools.partial(
        lambda i, j, b: (i, (_C_GATE + b * D_MODEL) // tn + j), b=b)) for b in range(N_BRANCH)]
    return pl.pallas_call(
        _merge_kernel,
        out_shape=jax.ShapeDtypeStruct((m, D_MODEL), BF16),
        grid=(m // tm, D_MODEL // tn),
        in_specs=y_specs + w_specs + g_specs,
        out_specs=pl.BlockSpec((tm, tn), lambda i, j: (i, j)),
        compiler_params=_cp("parallel", "arbitrary"),
        name="merge",
    )(*ys, *ws, zg, zg, zg, zg)


def _conv3(p, prev_row, next_row, w_ref):
    tm = p.shape[0]
    row = lax.broadcasted_iota(jnp.int32, p.shape, 0)
    up = jnp.where(row == 0, prev_row, pltpu.roll(p, 1, 0))
    dn = jnp.where(row == tm - 1, next_row, pltpu.roll(p, tm - 1, 0))
    return up * w_ref[0:1, :] + p * w_ref[1:2, :] + dn * w_ref[2:3, :]


def _halo_specs(tm, tc, m, col):
    nb = m // HALO
    per = tm // HALO
    main = pl.BlockSpec((tm, tc), lambda i: (i, col))
    prev = pl.BlockSpec((HALO, tc), lambda i: (jnp.maximum(i * per - 1, 0), col))
    nxt = pl.BlockSpec((HALO, tc), lambda i: (jnp.minimum((i + 1) * per, nb - 1), col))
    return main, prev, nxt


def _convmix_kernel(xa, xap, xan, cg, cgp, cgn, bg, w_ref, o_ref):
    i = pl.program_id(0)
    last = pl.num_programs(0) - 1
    p = cg[...].astype(F32) * xa[...].astype(F32)
    pp = cgp[...].astype(F32)[HALO - 1:HALO, :] * xap[...].astype(F32)[HALO - 1:HALO, :]
    pn = cgn[...].astype(F32)[0:1, :] * xan[...].astype(F32)[0:1, :]
    pp = jnp.where(i > 0, pp, 0.0)
    pn = jnp.where(i < last, pn, 0.0)
    o_ref[...] = (bg[...].astype(F32) * _conv3(p, pp, pn, w_ref)).astype(o_ref.dtype)


def conv_mixer(zg, w, layer):
    m = zg.shape[0]
    tm = min(m, 1024)
    tc = CONV_W
    xa = _halo_specs(tm, tc, m, _C_XA // tc)
    cg = _halo_specs(tm, tc, m, _C_CG // tc)
    bg = pl.BlockSpec((tm, tc), lambda i: (i, _C_BG // tc))
    return pl.pallas_call(
        _convmix_kernel,
        out_shape=jax.ShapeDtypeStruct((m, CONV_W), BF16),
        grid=(m // tm,),
        in_specs=[*xa, *cg, bg, pl.BlockSpec((None, 3, tc), lambda i: (layer, 0, 0))],
        out_specs=pl.BlockSpec((tm, tc), lambda i: (i, 0)),
        compiler_params=_cp("parallel"),
        name="conv_mixer",
    )(zg, zg, zg, zg, zg, zg, zg, w)


def _ffn_up_act_kernel(h_ref, hp_ref, hn_ref, g_ref, sh_ref, sc_ref, wa_ref, wg_ref, cwa_ref, cwg_ref,
                       o_ref, xn_ref):
    i = pl.program_id(0)
    j = pl.program_id(1)
    last = pl.num_programs(0) - 1
    tm = h_ref.shape[0]
    rows = tm + 2 * HALO

    cm = min(tm, NORM_CHUNK)
    nch = tm // cm

    def finish(ua, ug):
        def conv(u, cw_ref):
            c = (pltpu.roll(u, 1, 0) * cw_ref[0:1, :] + u * cw_ref[1:2, :]
                 + pltpu.roll(u, rows - 1, 0) * cw_ref[2:3, :])
            return c[HALO:HALO + tm]

        ca = conv(ua, cwa_ref)
        cgt = conv(ug, cwg_ref)
        o_ref[...] = (ca * _sigmoid(ca) * cgt).astype(o_ref.dtype)

    up = lambda x, w_ref: jnp.dot(x, w_ref[...], preferred_element_type=F32)

    @pl.when(j == 0)
    def _():
        nm = lambda x: _norm_mod(x, g_ref[...], sh_ref[...], sc_ref[...])
        xn_ref[0:HALO, :] = jnp.where(i > 0, nm(hp_ref[...]), 0.0).astype(BF16)
        uas, ugs = [], []
        for c in range(nch):
            lo, hi = c * cm, (c + 1) * cm
            xn_ref[HALO + lo:HALO + hi, :] = nm(h_ref[lo:hi, :]).astype(BF16)
            if c == nch - 1:
                xn_ref[HALO + tm:rows, :] = jnp.where(i < last, nm(hn_ref[...]), 0.0).astype(BF16)
            r0 = 0 if c == 0 else HALO + lo
            r1 = rows if c == nch - 1 else HALO + hi
            xc = xn_ref[r0:r1, :]
            uas.append(up(xc, wa_ref))
            ugs.append(up(xc, wg_ref))
        cat = lambda parts: parts[0] if len(parts) == 1 else jnp.concatenate(parts, axis=0)
        finish(cat(uas), cat(ugs))

    @pl.when(j > 0)
    def _():
        xn = xn_ref[...]
        finish(up(xn, wa_ref), up(xn, wg_ref))


def ffn_up_act(h, gain, shift, scale, w_up, conv_w, layer, name):
    m, d = h.shape
    tm = min(m, 1024)
    tn = 512
    nj = D_FF // tn
    nb = m // HALO
    per = tm // HALO
    vec = pl.BlockSpec((1, d), lambda i, j: (0, 0))
    return pl.pallas_call(
        _ffn_up_act_kernel,
        out_shape=jax.ShapeDtypeStruct((m, D_FF), BF16),
        grid=(m // tm, nj),
        in_specs=[pl.BlockSpec((tm, d), lambda i, j: (i, 0)),
                  pl.BlockSpec((HALO, d), lambda i, j: (jnp.maximum(i * per - 1, 0), 0)),
                  pl.BlockSpec((HALO, d), lambda i, j: (jnp.minimum((i + 1) * per, nb - 1), 0)),
                  vec, vec, vec,
                  pl.BlockSpec((None, d, tn), lambda i, j: (layer, 0, j)),
                  pl.BlockSpec((None, d, tn), lambda i, j: (layer, 0, nj + j)),
                  pl.BlockSpec((None, 3, tn), lambda i, j: (layer, 0, j)),
                  pl.BlockSpec((None, 3, tn), lambda i, j: (layer, 0, nj + j))],
        out_specs=pl.BlockSpec((tm, tn), lambda i, j: (i, j)),
        scratch_shapes=[pltpu.VMEM((tm + 2 * HALO, d), BF16)],
        compiler_params=_cp("parallel", "arbitrary"),
        name=name,
    )(h, h, h, gain, shift, scale, w_up, w_up, conv_w, conv_w)


def _head(z_ref, h, gain_ref=None, rope=None, scale=1.0):
    x = z_ref[:, h * HEAD_DIM:(h + 1) * HEAD_DIM].astype(F32)
    if gain_ref is not None:
        ms = jnp.mean(x * x, axis=-1, keepdims=True)
        x = x * lax.rsqrt(ms + EPS) * gain_ref[...]
    if rope is not None:
        x = x * rope[0][...] + pltpu.roll(x, HEAD_DIM // 2, 1) * rope[1][...]
    if scale != 1.0:
        x = x * scale
    return x


def _prep_latent_kernel(nq_ref, nk_ref, zqa_ref, zqb_ref, zk_ref, zv_ref, nqg, nkg, gqg, gkg, cos_ref, sin_ref,
                        nq_o, nk_o, gqT_o, k_o, vT_o, *, qscale):
    rope = (cos_ref, sin_ref)
    for h in range(NA_HEADS):
        sl = slice(h * HEAD_DIM, (h + 1) * HEAD_DIM)
        nq_o[:, sl] = _head(nq_ref, h, nqg, scale=qscale).astype(BF16)
        nk_o[:, sl] = _head(nk_ref, h, nkg).astype(BF16)
    for h in range(GQA_Q_HEADS):
        src = zqa_ref if h < GQA_Q_HEADS // 2 else zqb_ref
        gqT_o[h] = _head(src, h % (GQA_Q_HEADS // 2), gqg, rope, qscale).T.astype(BF16)
    for h in range(GQA_KV_HEADS):
        k_o[h] = _head(zk_ref, h, gkg, rope).astype(BF16)
        vT_o[h] = _head(zv_ref, h).T.astype(BF16)


def prep_latent(zg, gains, rope, qscale):
    m = zg.shape[0]
    tm = 1024
    col = lambda c0, w: pl.BlockSpec((tm, w), lambda i: (i, c0 // w))
    gain = pl.BlockSpec((1, HEAD_DIM), lambda i: (0, 0))
    tab = pl.BlockSpec((tm, HEAD_DIM), lambda i: (i, 0))
    return pl.pallas_call(
        functools.partial(_prep_latent_kernel, qscale=qscale),
        out_shape=(jax.ShapeDtypeStruct((m, NA_W), BF16), jax.ShapeDtypeStruct((m, NA_W), BF16),
                   jax.ShapeDtypeStruct((GQA_Q_HEADS, HEAD_DIM, m), BF16),
                   jax.ShapeDtypeStruct((GQA_KV_HEADS, m, HEAD_DIM), BF16),
                   jax.ShapeDtypeStruct((GQA_KV_HEADS, HEAD_DIM, m), BF16)),
        grid=(m // tm,),
        in_specs=[col(_C_NQ, NA_W), col(_C_NK, NA_W), col(_C_ZQ, GQA_W // 2), col(_C_ZQ + GQA_W // 2, GQA_W // 2),
                  col(_C_ZK, GQA_KV_W), col(_C_ZV, GQA_KV_W), gain, gain, gain, gain, tab, tab],
        out_specs=(pl.BlockSpec((tm, NA_W), lambda i: (i, 0)), pl.BlockSpec((tm, NA_W), lambda i: (i, 0)),
                   pl.BlockSpec((GQA_Q_HEADS, HEAD_DIM, tm), lambda i: (0, 0, i)),
                   pl.BlockSpec((GQA_KV_HEADS, tm, HEAD_DIM), lambda i: (0, i, 0)),
                   pl.BlockSpec((GQA_KV_HEADS, HEAD_DIM, tm), lambda i: (0, 0, i))),
        compiler_params=_cp("parallel"),
        name="prep_latent",
    )(zg, zg, zg, zg, zg, zg, *gains, *rope)


def _prep_ctx_kernel(nq_ref, nk_ref, nv_ref, zqa_ref, zqb_ref, zk_ref, zv_ref, nqg, nkg, gqg, gkg,
                     nk_o, cnqT_o, cnk_o, cnvT_o, cqT_o, ck_o, cvT_o, *, qscale):
    for h in range(NA_HEADS):
        sl = slice(h * HEAD_DIM, (h + 1) * HEAD_DIM)
        kn = _head(nk_ref, h, nkg).astype(BF16)
        nk_o[:, sl] = kn
        cnk_o[h] = kn
        cnqT_o[h] = _head(nq_ref, h, nqg, scale=qscale).T.astype(BF16)
        cnvT_o[h] = _head(nv_ref, h).T.astype(BF16)
    for h in range(GQA_Q_HEADS):
        src = zqa_ref if h < GQA_Q_HEADS // 2 else zqb_ref
        cqT_o[h] = _head(src, h % (GQA_Q_HEADS // 2), gqg, scale=qscale).T.astype(BF16)
    for h in range(GQA_KV_HEADS):
        ck_o[h] = _head(zk_ref, h, gkg).astype(BF16)
        cvT_o[h] = _head(zv_ref, h).T.astype(BF16)


def prep_ctx(czg, gains, qscale):
    m = czg.shape[0]
    col = lambda c0, w: pl.BlockSpec((m, w), lambda i: (0, c0 // w))
    gain = pl.BlockSpec((1, HEAD_DIM), lambda i: (0, 0))
    full = lambda shape: pl.BlockSpec(shape, lambda i: (0,) * len(shape))
    return pl.pallas_call(
        functools.partial(_prep_ctx_kernel, qscale=qscale),
        out_shape=(jax.ShapeDtypeStruct((m, NA_W), BF16),
                   jax.ShapeDtypeStruct((NA_HEADS, HEAD_DIM, m), BF16),
                   jax.ShapeDtypeStruct((NA_HEADS, m, HEAD_DIM), BF16),
                   jax.ShapeDtypeStruct((NA_HEADS, HEAD_DIM, m), BF16),
                   jax.ShapeDtypeStruct((GQA_Q_HEADS, HEAD_DIM, m), BF16),
                   jax.ShapeDtypeStruct((GQA_KV_HEADS, m, HEAD_DIM), BF16),
                   jax.ShapeDtypeStruct((GQA_KV_HEADS, HEAD_DIM, m), BF16)),
        grid=(1,),
        in_specs=[col(_C_NQ, NA_W), col(_C_NK, NA_W), col(_C_NV, NA_W), col(_C_ZQ, GQA_W // 2),
                  col(_C_ZQ + GQA_W // 2, GQA_W // 2), col(_C_ZK, GQA_KV_W), col(_C_ZV, GQA_KV_W),
                  gain, gain, gain, gain],
        out_specs=(full((m, NA_W)),
                   full((NA_HEADS, HEAD_DIM, m)), full((NA_HEADS, m, HEAD_DIM)), full((NA_HEADS, HEAD_DIM, m)),
                   full((GQA_Q_HEADS, HEAD_DIM, m)), full((GQA_KV_HEADS, m, HEAD_DIM)),
                   full((GQA_KV_HEADS, HEAD_DIM, m))),
        compiler_params=_cp("arbitrary"),
        name="prep_ctx",
    )(czg, czg, czg, czg, czg, czg, czg, *gains)


SCORE_BOUND_SAFE = 60.0
FLASH_TQ = 512
FLASH_UNITS = 16


def _flash_kernel(*refs, group, tk, has_extra):
    if has_extra:
        qT_ref, k_ref, vT_ref, k2_ref, vT2_ref, o_ref, kmax_ref, acc_ref = refs
    else:
        qT_ref, k_ref, vT_ref, o_ref, kmax_ref, acc_ref = refs
        k2_ref = vT2_ref = None
    tq = min(qT_ref.shape[2], FLASH_TQ)
    nsub = qT_ref.shape[2] // tq
    cols = [(h, sub) for h in range(group) for sub in range(nsub)]
    n_keys = k_ref.shape[1]
    n_full = n_keys // tk
    rem = n_keys - n_full * tk
    tails = ([(k_ref, vT_ref, n_full * tk, rem)] if rem > 0 else []) + (
        [(k2_ref, vT2_ref, 0, k2_ref.shape[1])] if has_extra else [])

    @pl.when(pl.program_id(1) == 0)
    def _():
        def ksq(kr, start, size):
            kb = kr[0, pl.ds(start, size), :].astype(F32)
            return jnp.max(jnp.sum(kb * kb, axis=-1, keepdims=True), axis=0, keepdims=True)

        mx = jnp.zeros((1, 1), F32)
        if n_full > 0:
            mx = lax.fori_loop(
                0, n_full, lambda i, c: jnp.maximum(c, ksq(k_ref, pl.multiple_of(i * tk, tk), tk)), mx)
        for kr, _, start, size in tails:
            mx = jnp.maximum(mx, ksq(kr, start, size))
        kmax_ref[...] = mx

    def run_blocks(step, carry):
        if n_full > 0:
            carry = lax.fori_loop(
                0, n_full, lambda kb, c: step(k_ref, vT_ref, pl.multiple_of(kb * tk, tk), tk, c), carry)
        for kr, vr, start, size in tails:
            carry = step(kr, vr, start, size, carry)
        return carry

    qTs = [qT_ref[h, :, sub * tq:(sub + 1) * tq] for h, sub in cols]
    shifts = []
    for qT in qTs:
        qf = qT.astype(F32)
        qsq = jnp.sum(qf * qf, axis=0, keepdims=True)
        shifts.append(jnp.sqrt(qsq * kmax_ref[...]) * 1.01)
    safe = jnp.max(functools.reduce(jnp.maximum, shifts)) <= SCORE_BOUND_SAFE

    @pl.when(safe)
    def _():
        def step(kr, vr, start, size, l8s, nb=1):
            units = [(j, c) for j in range(nb) for c in range(len(cols))]
            kblks = [kr[0, pl.ds(start + j * size, size), :] for j in range(nb)]
            vblks = [vr[0, :, pl.ds(start + j * size, size)] for j in range(nb)]
            l8s = list(l8s)
            s_next = jnp.dot(kblks[0], qTs[0], preferred_element_type=F32)
            for u, (j, c) in enumerate(units):
                s = s_next
                if u + 1 < len(units):
                    jn, cn = units[u + 1]
                    s_next = jnp.dot(kblks[jn], qTs[cn], preferred_element_type=F32)
                p = jnp.exp2(s - shifts[c])
                l8s[c] = l8s[c] + jnp.sum(p.reshape(size // 8, 8, tq), axis=0)
                acc_ref[c] += jnp.dot(vblks[j], p.astype(BF16), preferred_element_type=F32)
            return tuple(l8s)

        acc_ref[...] = jnp.zeros_like(acc_ref)
        l8s = tuple(jnp.zeros((8, tq), F32) for _ in cols)
        unroll = max(1, FLASH_UNITS // len(cols))
        n_body = n_full // unroll
        if n_body > 0:
            l8s = lax.fori_loop(
                0, n_body,
                lambda kb, c: step(k_ref, vT_ref, pl.multiple_of(kb * (tk * unroll), tk * unroll), tk, c,
                                   nb=unroll), l8s)
        for kb in range(n_body * unroll, n_full):
            l8s = step(k_ref, vT_ref, kb * tk, tk, l8s)
        for kr, vr, start, size in tails:
            l8s = step(kr, vr, start, size, l8s)
        for c in range(len(cols)):
            l = jnp.sum(l8s[c], axis=0, keepdims=True)
            acc_ref[c] = acc_ref[c] * (1.0 / l)

    @pl.when(jnp.logical_not(safe))
    def _():
        for c in range(len(cols)):
            def step(kr, vr, start, size, carry, qT=qTs[c]):
                m, l, acc = carry
                kblk = kr[0, pl.ds(start, size), :]
                s = jnp.dot(kblk, qT, preferred_element_type=F32)
                m_new = jnp.maximum(m, jnp.max(s, axis=0, keepdims=True))
                alpha = jnp.exp2(m - m_new)
                p = jnp.exp2(s - m_new)
                l = alpha * l + jnp.sum(p, axis=0, keepdims=True)
                vblk = vr[0, :, pl.ds(start, size)]
                acc = alpha * acc + jnp.dot(vblk, p.astype(BF16), preferred_element_type=F32)
                return m_new, l, acc

            carry = (jnp.full((1, tq), NEG_BIG, F32), jnp.zeros((1, tq), F32),
                     jnp.zeros((HEAD_DIM, tq), F32))
            _, l, acc = run_blocks(step, carry)
            acc_ref[c] = acc * (1.0 / l)

    for c, (h, sub) in enumerate(cols):
        o_ref[sub * tq:(sub + 1) * tq, h * HEAD_DIM:(h + 1) * HEAD_DIM] = acc_ref[c].T.astype(o_ref.dtype)


def flash_attention(qT, k, vT, tq, extra=None):
    hq, _, mq = qT.shape
    hkv = k.shape[0]
    group = hq // hkv
    kv_specs = lambda kk, vv: [pl.BlockSpec((1, kk.shape[1], HEAD_DIM), lambda g, i: (g, 0, 0)),
                               pl.BlockSpec((1, HEAD_DIM, vv.shape[2]), lambda g, i: (g, 0, 0))]
    args, specs = [k, vT], kv_specs(k, vT)
    if extra is not None:
        args += list(extra)
        specs += kv_specs(*extra)
    return pl.pallas_call(
        functools.partial(_flash_kernel, group=group, tk=512, has_extra=extra is not None),
        out_shape=jax.ShapeDtypeStruct((mq, hq * HEAD_DIM), BF16),
        grid=(hkv, mq // tq),
        in_specs=[pl.BlockSpec((group, HEAD_DIM, tq), lambda g, i: (g, 0, i))] + specs,
        out_specs=pl.BlockSpec((tq, group * HEAD_DIM), lambda g, i: (i, g)),
        scratch_shapes=[pltpu.VMEM((1, 1), F32),
                        pltpu.VMEM((group * max(1, tq // FLASH_TQ), HEAD_DIM, min(tq, FLASH_TQ)), F32)],
        compiler_params=_cp("parallel", "arbitrary"),
        name="flash_attention",
    )(qT, *args)


NA_RB = 8
NA_TOK = NA_RB * GRID_W
NA_PAIR = 2 * GRID_W
NA_SLAB_ROWS = NA_ROWS + 2
NA_SLAB = NA_SLAB_ROWS * GRID_W
NA_PATTERNS = 5


def _na_kernel(q_ref, kp, kc, kn, vp, vc, vn, ck_ref, cv_ref, bias_ref, o_ref, kbuf, vbuf):
    b = pl.program_id(0)
    rows = pl.num_programs(0) * NA_RB
    kbuf[0:NA_TOK, :] = kp[...]
    kbuf[NA_TOK:2 * NA_TOK, :] = kc[...]
    kbuf[2 * NA_TOK:3 * NA_TOK, :] = kn[...]
    vbuf[0:NA_TOK, :] = vp[...]
    vbuf[NA_TOK:2 * NA_TOK, :] = vc[...]
    vbuf[2 * NA_TOK:3 * NA_TOK, :] = vn[...]
    nt = (((1,), (1,)), ((), ()))

    def slab(pair):
        r = b * NA_RB + 2 * pair
        u0 = jnp.clip(r - NA_ROWS // 2, 0, rows - NA_ROWS)
        off = pl.multiple_of((u0 - b * NA_RB + NA_RB) * GRID_W, NA_PAIR)
        pat = jnp.where(r < NA_ROWS // 2, r // 2,
                        jnp.where(r >= rows - NA_ROWS // 2, 3 + (r - (rows - NA_ROWS // 2)) // 2, 2))
        return off, pat

    slabs = [slab(pair) for pair in range(NA_RB // 2)]
    units = [(pair, h) for pair in range(NA_RB // 2) for h in range(NA_HEADS)]

    def scores(pair, h):
        off, pat = slabs[pair]
        sl = slice(h * HEAD_DIM, (h + 1) * HEAD_DIM)
        qh = q_ref[pair * NA_PAIR:(pair + 1) * NA_PAIR, sl]
        s = lax.dot_general(qh, kbuf[pl.ds(off, NA_SLAB), sl], nt, preferred_element_type=F32)
        sc = lax.dot_general(qh, ck_ref[:, sl], nt, preferred_element_type=F32)
        return s + bias_ref[pat, h], sc

    nxt = scores(*units[0])
    for u, (pair, h) in enumerate(units):
        s, sc = nxt
        if u + 1 < len(units):
            nxt = scores(*units[u + 1])
        off, _ = slabs[pair]
        sl = slice(h * HEAD_DIM, (h + 1) * HEAD_DIM)
        m = jnp.maximum(jnp.max(s, axis=-1, keepdims=True), jnp.max(sc, axis=-1, keepdims=True))
        p = jnp.exp2(s - m)
        pc = jnp.exp2(sc - m)
        l = jnp.sum(p, axis=-1, keepdims=True) + jnp.sum(pc, axis=-1, keepdims=True)
        o = (jnp.dot(p.astype(BF16), vbuf[pl.ds(off, NA_SLAB), sl], preferred_element_type=F32)
             + jnp.dot(pc.astype(BF16), cv_ref[:, sl], preferred_element_type=F32))
        o_ref[pair * NA_PAIR:(pair + 1) * NA_PAIR, sl] = (o * (1.0 / l)).astype(o_ref.dtype)


def neighborhood_attention(qn, kn, zg, ckn, czg, bias):
    m = qn.shape[0]
    nb = m // NA_TOK
    vcol = _C_NV // NA_W
    blk = (NA_TOK, NA_W)
    return pl.pallas_call(
        _na_kernel,
        out_shape=jax.ShapeDtypeStruct((m, NA_W), BF16),
        grid=(nb,),
        in_specs=[pl.BlockSpec(blk, lambda b: (b, 0)),
                  pl.BlockSpec(blk, lambda b: (jnp.maximum(b - 1, 0), 0)),
                  pl.BlockSpec(blk, lambda b: (b, 0)),
                  pl.BlockSpec(blk, lambda b: (jnp.minimum(b + 1, nb - 1), 0)),
                  pl.BlockSpec(blk, lambda b: (jnp.maximum(b - 1, 0), vcol)),
                  pl.BlockSpec(blk, lambda b: (b, vcol)),
                  pl.BlockSpec(blk, lambda b: (jnp.minimum(b + 1, nb - 1), vcol)),
                  pl.BlockSpec((CTX_LEN, NA_W), lambda b: (0, 0)),
                  pl.BlockSpec((CTX_LEN, NA_W), lambda b: (0, vcol)),
                  pl.BlockSpec((NA_PATTERNS, NA_HEADS, NA_PAIR, NA_SLAB), lambda b: (0, 0, 0, 0))],
        out_specs=pl.BlockSpec(blk, lambda b: (b, 0)),
        scratch_shapes=[pltpu.VMEM((3 * NA_TOK, NA_W), BF16), pltpu.VMEM((3 * NA_TOK, NA_W), BF16)],
        compiler_params=_cp("arbitrary"),
        name="neighborhood_attention",
    )(qn, kn, kn, kn, zg, zg, zg, ckn, czg, bias)


def na_bias_table(rpb):
    col = np.arange(GRID_W)
    c0 = np.clip(col - NA_COLS // 2, 0, GRID_W - NA_COLS)
    kc = np.arange(GRID_W)
    inside = (kc[None, :] >= c0[:, None]) & (kc[None, :] < c0[:, None] + NA_COLS)
    dc = kc[None, :] - col[:, None] + NA_COLS - 1
    onehot = (dc[None] == np.arange(2 * NA_COLS - 1)[:, None, None]) & inside[None]
    t = jnp.einsum('hrd,dck->hrck', rpb.astype(F32) * LOG2E, jnp.asarray(onehot, F32),
                   precision=lax.Precision.HIGHEST)
    t = jnp.where(inside[None, None], t, NEG_BIG)
    masked = jnp.full((NA_HEADS, GRID_W, GRID_W), NEG_BIG, F32)
    patterns = [(0, 0), (2, 0), (4, 1), (4, 0), (6, 0)]
    tabs = []
    for a, e in patterns:
        per_q = []
        for start, dr0 in ((0, NA_ROWS - 1 - a), (e, NA_ROWS - 2 - a)):
            blocks = [t[:, w + dr0] if start <= w < start + NA_ROWS else masked for w in range(NA_SLAB_ROWS)]
            per_q.append(jnp.stack(blocks, axis=2).reshape(NA_HEADS, GRID_W, NA_SLAB))
        tabs.append(jnp.concatenate(per_q, axis=1))
    return jnp.stack(tabs, axis=0)


def _dft_mats(n):
    a = 2.0 * np.pi * np.outer(np.arange(n), np.arange(n)) / n
    return np.cos(a), np.sin(a)


def _chan_dft_kernel(z_ref, w_ref, o_ref):
    for g in range(FOURIER_GROUPS):
        x = z_ref[:, g * HEAD_DIM:(g + 1) * HEAD_DIM]
        r = jnp.dot(x, w_ref[...], preferred_element_type=F32)
        o_ref[0, :, g * HEAD_DIM:(g + 1) * HEAD_DIM] = r[:, :HEAD_DIM].astype(o_ref.dtype)
        o_ref[1, :, g * HEAD_DIM:(g + 1) * HEAD_DIM] = r[:, HEAD_DIM:].astype(o_ref.dtype)


def chan_dft(zg, wch):
    m = zg.shape[0]
    tm = min(m, 1024)
    return pl.pallas_call(
        _chan_dft_kernel,
        out_shape=jax.ShapeDtypeStruct((2, m, FOURIER_W), BF16),
        grid=(m // tm,),
        in_specs=[pl.BlockSpec((tm, FOURIER_W), lambda i: (i, _C_ZF // FOURIER_W)),
                  pl.BlockSpec((HEAD_DIM, 2 * HEAD_DIM), lambda i: (0, 0))],
        out_specs=pl.BlockSpec((2, tm, FOURIER_W), lambda i: (0, i, 0)),
        compiler_params=_cp("parallel"),
        name="chan_dft",
    )(zg, wch)


def _left_mm_kernel(l_ref, x_ref, o_ref, *, n_out):
    x = jnp.concatenate([x_ref[0], x_ref[1]], axis=0)
    r = jnp.dot(l_ref[...], x, preferred_element_type=F32)
    if n_out == 1:
        o_ref[...] = r.astype(o_ref.dtype)
    else:
        half = r.shape[0] // 2
        o_ref[0] = r[:half].astype(o_ref.dtype)
        o_ref[1] = r[half:].astype(o_ref.dtype)


def dft_stage1(zc, lmat):
    _, r, n = zc.shape
    tn = 4096
    return pl.pallas_call(
        functools.partial(_left_mm_kernel, n_out=2),
        out_shape=jax.ShapeDtypeStruct((2, r, n), BF16),
        grid=(n // tn,),
        in_specs=[pl.BlockSpec((2 * r, 2 * r), lambda j: (0, 0)),
                  pl.BlockSpec((2, r, tn), lambda j: (0, 0, j))],
        out_specs=pl.BlockSpec((2, r, tn), lambda j: (0, 0, j)),
        compiler_params=_cp("parallel"),
        name="dft_stage1",
    )(lmat, zc)


def _dft_stage3_kernel(l_ref, x_ref, o_ref, *, kb):
    for i in range(kb):
        x = jnp.concatenate([x_ref[0, i], x_ref[1, i]], axis=0)
        o_ref[i] = jnp.dot(l_ref[i], x, preferred_element_type=F32).astype(o_ref.dtype)


def dft_stage3(a, tables):
    _, n1, n2, c = a.shape
    kb = 8
    return pl.pallas_call(
        functools.partial(_dft_stage3_kernel, kb=kb),
        out_shape=jax.ShapeDtypeStruct((n1, n2, c), BF16),
        grid=(n1 // kb,),
        in_specs=[pl.BlockSpec((kb, n2, 2 * n2), lambda j: (j, 0, 0)),
                  pl.BlockSpec((2, kb, n2, c), lambda j: (0, j, 0, 0))],
        out_specs=pl.BlockSpec((kb, n2, c), lambda j: (j, 0, 0)),
        compiler_params=_cp("parallel"),
        name="dft_stage3",
    )(tables, a)


def dft_direct(zc, lmat):
    _, t, c = zc.shape
    return pl.pallas_call(
        functools.partial(_left_mm_kernel, n_out=1),
        out_shape=jax.ShapeDtypeStruct((t, c), BF16),
        grid=(1,),
        in_specs=[pl.BlockSpec((t, 2 * t), lambda j: (0, 0)),
                  pl.BlockSpec((2, t, c), lambda j: (0, 0, 0))],
        out_specs=pl.BlockSpec((t, c), lambda j: (0, 0)),
        compiler_params=_cp("arbitrary"),
        name="dft_direct",
    )(lmat, zc)


def _fourier_tables():
    c128, s128 = _dft_mats(HEAD_DIM)
    wch = np.concatenate([c128, -s128], axis=1)
    n1 = SEQ // HEAD_DIM
    l1 = np.block([[c128, s128], [-s128, c128]])
    k1 = np.arange(n1)[:, None, None]
    k2 = np.arange(n1)[None, :, None]
    t2 = np.arange(n1)[None, None, :]
    ang = 2.0 * np.pi * ((n1 * k2 + k1) * t2 % SEQ) / SEQ
    norm = 1.0 / math.sqrt(SEQ * HEAD_DIM)
    l3 = np.concatenate([np.cos(ang), np.sin(ang)], axis=2) * norm
    cc, sc = _dft_mats(CTX_LEN)
    lc = np.concatenate([cc, sc], axis=1) / math.sqrt(CTX_LEN * HEAD_DIM)
    as_bf = lambda a: jnp.asarray(a, F32).astype(BF16)
    return as_bf(wch), as_bf(l1), as_bf(l3), as_bf(lc)


def fourier_latent(zg, tabs):
    wch, l1, l3, _ = tabs
    n1 = SEQ // HEAD_DIM
    zc = chan_dft(zg, wch)
    a = dft_stage1(zc.reshape(2, n1, n1 * FOURIER_W), l1)
    o3 = dft_stage3(a.reshape(2, n1, n1, FOURIER_W), l3)
    return o3.transpose(1, 0, 2).reshape(SEQ, FOURIER_W)


def fourier_ctx(czg, tabs):
    wch, _, _, lc = tabs
    return dft_direct(chan_dft(czg, wch), lc)


def _rope_tables(n_tok):
    t = jnp.arange(n_tok)
    row = (t // GRID_W).astype(F32)
    col = (t % GRID_W).astype(F32)
    n_freq = HEAD_DIM // 4
    inv_freq = ROPE_THETA ** (-jnp.arange(n_freq, dtype=F32) / n_freq)
    ang = jnp.concatenate([row[:, None] * inv_freq, col[:, None] * inv_freq], axis=-1)
    cos, sin = jnp.cos(ang), jnp.sin(ang)
    return jnp.concatenate([cos, cos], axis=-1), jnp.concatenate([-sin, sin], axis=-1)


def kernel(x, c, ctx, c_ctx, w_ada, b_ada, norm1, w_in, conv_w, na_q_gain, na_k_gain, na_rpb, gqa_q_gain, gqa_k_gain, w_conv_out, w_fourier_out, w_na_out, w_gqa_out, w_gate, b_gate, w_o, norm2, w_up, ffn_conv_w, w_down):
    d = D_MODEL
    h = x[0]
    hc = ctx[0]
    rope = _rope_tables(SEQ)
    ftabs = _fourier_tables()
    qscale2 = HEAD_DIM ** -0.5 * LOG2E
    cvec = jnp.zeros((8, d), F32).at[0].set(c[0]).at[1].set(c_ctx)
    row = lambda v: v.reshape(1, -1)

    w_in_b, w_gate_b = w_in.astype(BF16), w_gate.astype(BF16)
    merge_w = [w.astype(BF16) for w in (w_conv_out, w_fourier_out, w_na_out, w_gqa_out)]
    w_o_b, w_up_b, w_down_b = w_o.astype(BF16), w_up.astype(BF16), w_down.astype(BF16)
    b_ada3 = b_ada.reshape(DEPTH, 1, -1)
    b_gate3 = b_gate.reshape(DEPTH, 1, -1)

    for i in range(DEPTH):
        last = i == DEPTH - 1
        bias_tab = na_bias_table(na_rpb[i])
        gains = (row(na_q_gain[i]), row(na_k_gain[i]), row(gqa_q_gain[i]), row(gqa_k_gain[i]))

        mods = ada_mod(cvec, w_ada, b_ada3, i)
        lat = [mods[0:1, k * d:(k + 1) * d] for k in range(6)]
        cm = [mods[1:2, k * d:(k + 1) * d] for k in range(6)]

        zg = in_proj(h, row(norm1[i]), lat[0], lat[1], w_in_b, w_gate_b, b_gate3, i, "in_proj")
        czg = in_proj(hc, row(norm1[i]), cm[0], cm[1], w_in_b, w_gate_b, b_gate3, i, "in_proj_ctx")

        cnk_rows, cnqT, cnk, cnvT, cqT, ck, cvT = prep_ctx(czg, gains, qscale2)
        nq, nk, gqT, gk, gvT = prep_latent(zg, gains, rope, qscale2)

        y_conv = conv_mixer(zg, conv_w, i)
        y_four = fourier_latent(zg, ftabs)
        y_na = neighborhood_attention(nq, nk, zg, cnk_rows, czg, bias_tab)
        y_gqa = flash_attention(gqT, gk, gvT, tq=1024, extra=(ck, cvT))

        merged = merge_branches([y_conv, y_four, y_na, y_gqa], merge_w, i, zg)
        h = mm_residual(merged, w_o_b, i, h, lat[2], "out_proj", in_place=i > 0)
        act = ffn_up_act(h, row(norm2[i]), lat[3], lat[4], w_up_b, ffn_conv_w, i, "ffn_up_act")
        h = mm_residual(act, w_down_b, i, h, lat[5], "ffn_down", in_place=True)

        if not last:
            cy_conv = conv_mixer(czg, conv_w, i)
            cy_four = fourier_ctx(czg, ftabs)
            cy_na = flash_attention(cnqT, cnk, cnvT, tq=CTX_LEN)
            cy_gqa = flash_attention(cqT, ck, cvT, tq=CTX_LEN)
            cmerged = merge_branches([cy_conv, cy_four, cy_na, cy_gqa], merge_w, i, czg)
            hc = mm_residual(cmerged, w_o_b, i, hc, cm[2], "out_proj_ctx", in_place=i > 0)
            cact = ffn_up_act(hc, row(norm2[i]), cm[3], cm[4], w_up_b, ffn_conv_w, i, "ffn_up_act_ctx")
            hc = mm_residual(cact, w_down_b, i, hc, cm[5], "ffn_down_ctx", in_place=True)

    return h[None]
```

```python
import functools
import math

import jax
import jax.numpy as jnp
import numpy as np
from jax import lax
from jax.experimental import pallas as pl
from jax.experimental.pallas import tpu as pltpu

D_MODEL = 2048
SEQ = 16384
DEPTH = 4
GRID_W = 64
CTX_LEN = 256
HEAD_DIM = 128
CONV_W = 512
FOURIER_GROUPS = 4
FOURIER_W = 512
NA_HEADS = 4
NA_W = 512
NA_ROWS = 8
NA_COLS = 16
GQA_Q_HEADS = 8
GQA_KV_HEADS = 2
GQA_GROUP = 4
GQA_W = 1024
GQA_KV_W = 256
N_BRANCH = 4
N_IN = 5120
D_FF = 5632
ROPE_THETA = 10000.0
EPS = 1e-6

_C_XA, _C_BG, _C_CG = 0, 512, 1024
_C_ZF = 1536
_C_NQ, _C_NK, _C_NV = 2048, 2560, 3072
_C_ZQ, _C_ZK, _C_ZV = 3584, 4608, 4864
_C_GATE = N_IN

BF16 = jnp.bfloat16
F32 = jnp.float32
V7X_VMEM_LIMIT = 52 * 1024 * 1024
NEG_BIG = -1e30
HALO = 16
NORM_CHUNK = 256
LOG2E = 1.4426950408889634


def _cp(*sem):
    return pltpu.CompilerParams(dimension_semantics=sem, vmem_limit_bytes=V7X_VMEM_LIMIT)


def _sigmoid(t):
    return 0.5 * jnp.tanh(0.5 * t) + 0.5


def _norm_mod(x, g, sh, sc):
    ms = jnp.mean(x * x, axis=-1, keepdims=True)
    return (x * lax.rsqrt(ms + EPS) * g) * (1.0 + sc) + sh


def _ada_kernel(c_ref, w_ref, b_ref, o_ref):
    c = c_ref[...]
    s = c * _sigmoid(c)
    o_ref[...] = jnp.dot(s.astype(BF16), w_ref[...].astype(BF16),
                         preferred_element_type=F32) + b_ref[...]


def ada_mod(cvec, w, b, layer):
    n = w.shape[2]
    tn = 1024
    return pl.pallas_call(
        _ada_kernel,
        out_shape=jax.ShapeDtypeStruct((8, n), F32),
        grid=(n // tn,),
        in_specs=[pl.BlockSpec((8, D_MODEL), lambda j: (0, 0)),
                  pl.BlockSpec((None, D_MODEL, tn), lambda j: (layer, 0, j)),
                  pl.BlockSpec((None, 1, tn), lambda j: (layer, 0, j))],
        out_specs=pl.BlockSpec((8, tn), lambda j: (0, j)),
        compiler_params=_cp("parallel"),
        name="ada_mod",
    )(cvec, w, b)


def _in_proj_kernel(h_ref, g_ref, sh_ref, sc_ref, win_ref, wgate_ref, b_ref, o_ref, xn_ref, *, n_plain):
    j = pl.program_id(1)
    tm = h_ref.shape[0]
    cm = min(tm, NORM_CHUNK)

    @pl.when(j == 0)
    def _():
        for c in range(tm // cm):
            rows = slice(c * cm, (c + 1) * cm)
            xc = _norm_mod(h_ref[rows, :], g_ref[...], sh_ref[...], sc_ref[...]).astype(BF16)
            xn_ref[rows, :] = xc
            o_ref[rows, :] = jnp.dot(xc, win_ref[...], preferred_element_type=F32).astype(o_ref.dtype)

    @pl.when(jnp.logical_and(j > 0, j < n_plain))
    def _():
        o_ref[...] = jnp.dot(xn_ref[...], win_ref[...], preferred_element_type=F32).astype(o_ref.dtype)

    @pl.when(j >= n_plain)
    def _():
        t = jnp.dot(xn_ref[...], wgate_ref[...], preferred_element_type=F32) + b_ref[...]
        o_ref[...] = _sigmoid(t).astype(o_ref.dtype)


def in_proj(h, gain, shift, scale, w_in, w_gate, b_gate, layer, name):
    m, d = h.shape
    n_in, n_gate = w_in.shape[2], w_gate.shape[2]
    tm = min(m, 1024)
    tn = 1024
    n_plain = n_in // tn
    vec = pl.BlockSpec((1, d), lambda i, j: (0, 0))
    return pl.pallas_call(
        functools.partial(_in_proj_kernel, n_plain=n_plain),
        out_shape=jax.ShapeDtypeStruct((m, n_in + n_gate), BF16),
        grid=(m // tm, (n_in + n_gate) // tn),
        in_specs=[pl.BlockSpec((tm, d), lambda i, j: (i, 0)), vec, vec, vec,
                  pl.BlockSpec((None, d, tn), lambda i, j: (layer, 0, jnp.minimum(j, n_plain - 1))),
                  pl.BlockSpec((None, d, tn), lambda i, j: (layer, 0, jnp.maximum(j - n_plain, 0))),
                  pl.BlockSpec((None, 1, tn), lambda i, j: (layer, 0, jnp.maximum(j - n_plain, 0)))],
        out_specs=pl.BlockSpec((tm, tn), lambda i, j: (i, j)),
        scratch_shapes=[pltpu.VMEM((tm, d), BF16)],
        compiler_params=_cp("parallel", "arbitrary"),
        name=name,
    )(h, gain, shift, scale, w_in, w_gate, b_gate)


def _mm_res_kernel(x_ref, w_ref, h_ref, g_ref, o_ref):
    acc = jnp.dot(x_ref[...], w_ref[...], preferred_element_type=F32)
    o_ref[...] = h_ref[...] + g_ref[...] * acc


def mm_residual(x, w, layer, h, gate, name, in_place):
    m, k = x.shape
    n = w.shape[2]
    tm = min(m, 1024)
    tn = 1024 if k <= D_MODEL else 512
    return pl.pallas_call(
        _mm_res_kernel,
        out_shape=jax.ShapeDtypeStruct((m, n), F32),
        grid=(m // tm, n // tn),
        in_specs=[pl.BlockSpec((tm, k), lambda i, j: (i, 0)),
                  pl.BlockSpec((None, k, tn), lambda i, j: (layer, 0, j)),
                  pl.BlockSpec((tm, tn), lambda i, j: (i, j)),
                  pl.BlockSpec((1, tn), lambda i, j: (0, j))],
        out_specs=pl.BlockSpec((tm, tn), lambda i, j: (i, j)),
        input_output_aliases={2: 0} if in_place else {},
        compiler_params=_cp("parallel", "arbitrary"),
        name=name,
    )(x, w, h, gate)


def _merge_kernel(yc, yf, yn, yg, wc, wf, wn, wg, g0, g1, g2, g3, o_ref):
    def term(y, w, g):
        return g[...].astype(F32) * jnp.dot(y[...], w[...], preferred_element_type=F32)

    o_ref[...] = (term(yc, wc, g0) + term(yf, wf, g1) + term(yn, wn, g2)
                  + term(yg, wg, g3)).astype(o_ref.dtype)


def merge_branches(ys, ws, layer, zg):
    m = zg.shape[0]
    tm = min(m, 1024)
    tn = 1024
    y_specs = [pl.BlockSpec((tm, y.shape[1]), lambda i, j: (i, 0)) for y in ys]
    w_specs = [pl.BlockSpec((None, w.shape[1], tn), lambda i, j: (layer, 0, j)) for w in ws]
    g_specs = [pl.BlockSpec((tm, tn), functools.partial(
        lambda i, j, b: (i, (_C_GATE + b * D_MODEL) // tn + j), b=b)) for b in range(N_BRANCH)]
    return pl.pallas_call(
        _merge_kernel,
        out_shape=jax.ShapeDtypeStruct((m, D_MODEL), BF16),
        grid=(m // tm, D_MODEL // tn),
        in_specs=y_specs + w_specs + g_specs,
        out_specs=pl.BlockSpec((tm, tn), lambda i, j: (i, j)),
        compiler_params=_cp("parallel", "arbitrary"),
        name="merge",
    )(*ys, *ws, zg, zg, zg, zg)


def _conv3(p, prev_row, next_row, w_ref):
    tm = p.shape[0]
    row = lax.broadcasted_iota(jnp.int32, p.shape, 0)
    up = jnp.where(row == 0, prev_row, pltpu.roll(p, 1, 0))
    dn = jnp.where(row == tm - 1, next_row, pltpu.roll(p, tm - 1, 0))
    return up * w_ref[0:1, :] + p * w_ref[1:2, :] + dn * w_ref[2:3, :]


def _halo_specs(tm, tc, m, col):
    nb = m // HALO
    per = tm // HALO
    main = pl.BlockSpec((tm, tc), lambda i: (i, col))
    prev = pl.BlockSpec((HALO, tc), lambda i: (jnp.maximum(i * per - 1, 0), col))
    nxt = pl.BlockSpec((HALO, tc), lambda i: (jnp.minimum((i + 1) * per, nb - 1), col))
    return main, prev, nxt


def _convmix_kernel(xa, xap, xan, cg, cgp, cgn, bg, w_ref, o_ref):
    i = pl.program_id(0)
    last = pl.num_programs(0) - 1
    p = cg[...].astype(F32) * xa[...].astype(F32)
    pp = cgp[...].astype(F32)[HALO - 1:HALO, :] * xap[...].astype(F32)[HALO - 1:HALO, :]
    pn = cgn[...].astype(F32)[0:1, :] * xan[...].astype(F32)[0:1, :]
    pp = jnp.where(i > 0, pp, 0.0)
    pn = jnp.where(i < last, pn, 0.0)
    o_ref[...] = (bg[...].astype(F32) * _conv3(p, pp, pn, w_ref)).astype(o_ref.dtype)


def conv_mixer(zg, w, layer):
    m = zg.shape[0]
    tm = min(m, 1024)
    tc = CONV_W
    xa = _halo_specs(tm, tc, m, _C_XA // tc)
    cg = _halo_specs(tm, tc, m, _C_CG // tc)
    bg = pl.BlockSpec((tm, tc), lambda i: (i, _C_BG // tc))
    return pl.pallas_call(
        _convmix_kernel,
        out_shape=jax.ShapeDtypeStruct((m, CONV_W), BF16),
        grid=(m // tm,),
        in_specs=[*xa, *cg, bg, pl.BlockSpec((None, 3, tc), lambda i: (layer, 0, 0))],
        out_specs=pl.BlockSpec((tm, tc), lambda i: (i, 0)),
        compiler_params=_cp("parallel"),
        name="conv_mixer",
    )(zg, zg, zg, zg, zg, zg, zg, w)


def _ffn_up_act_kernel(h_ref, hp_ref, hn_ref, g_ref, sh_ref, sc_ref, wa_ref, wg_ref, cwa_ref, cwg_ref,
                       o_ref, xn_ref):
    i = pl.program_id(0)
    j = pl.program_id(1)
    last = pl.num_programs(0) - 1
    tm = h_ref.shape[0]
    rows = tm + 2 * HALO

    cm = min(tm, NORM_CHUNK)
    nch = tm // cm

    def finish(ua, ug):
        def conv(u, cw_ref):
            c = (pltpu.roll(u, 1, 0) * cw_ref[0:1, :] + u * cw_ref[1:2, :]
                 + pltpu.roll(u, rows - 1, 0) * cw_ref[2:3, :])
            return c[HALO:HALO + tm]

        ca = conv(ua, cwa_ref)
        cgt = conv(ug, cwg_ref)
        o_ref[...] = (ca * _sigmoid(ca) * cgt).astype(o_ref.dtype)

    up = lambda x, w_ref: jnp.dot(x, w_ref[...], preferred_element_type=F32)

    @pl.when(j == 0)
    def _():
        nm = lambda x: _norm_mod(x, g_ref[...], sh_ref[...], sc_ref[...])
        xn_ref[0:HALO, :] = jnp.where(i > 0, nm(hp_ref[...]), 0.0).astype(BF16)
        uas, ugs = [], []
        for c in range(nch):
            lo, hi = c * cm, (c + 1) * cm
            xn_ref[HALO + lo:HALO + hi, :] = nm(h_ref[lo:hi, :]).astype(BF16)
            if c == nch - 1:
                xn_ref[HALO + tm:rows, :] = jnp.where(i < last, nm(hn_ref[...]), 0.0).astype(BF16)
            r0 = 0 if c == 0 else HALO + lo
            r1 = rows if c == nch - 1 else HALO + hi
            xc = xn_ref[r0:r1, :]
            uas.append(up(xc, wa_ref))
            ugs.append(up(xc, wg_ref))
        cat = lambda parts: parts[0] if len(parts) == 1 else jnp.concatenate(parts, axis=0)
        finish(cat(uas), cat(ugs))

    @pl.when(j > 0)
    def _():
        xn = xn_ref[...]
        finish(up(xn, wa_ref), up(xn, wg_ref))


def ffn_up_act(h, gain, shift, scale, w_up, conv_w, layer, name):
    m, d = h.shape
    tm = min(m, 1024)
    tn = 512
    nj = D_FF // tn
    nb = m // HALO
    per = tm // HALO
    vec = pl.BlockSpec((1, d), lambda i, j: (0, 0))
    return pl.pallas_call(
        _ffn_up_act_kernel,
        out_shape=jax.ShapeDtypeStruct((m, D_FF), BF16),
        grid=(m // tm, nj),
        in_specs=[pl.BlockSpec((tm, d), lambda i, j: (i, 0)),
                  pl.BlockSpec((HALO, d), lambda i, j: (jnp.maximum(i * per - 1, 0), 0)),
                  pl.BlockSpec((HALO, d), lambda i, j: (jnp.minimum((i + 1) * per, nb - 1), 0)),
                  vec, vec, vec,
                  pl.BlockSpec((None, d, tn), lambda i, j: (layer, 0, j)),
                  pl.BlockSpec((None, d, tn), lambda i, j: (layer, 0, nj + j)),
                  pl.BlockSpec((None, 3, tn), lambda i, j: (layer, 0, j)),
                  pl.BlockSpec((None, 3, tn), lambda i, j: (layer, 0, nj + j))],
        out_specs=pl.BlockSpec((tm, tn), lambda i, j: (i, j)),
        scratch_shapes=[pltpu.VMEM((tm + 2 * HALO, d), BF16)],
        compiler_params=_cp("parallel", "arbitrary"),
        name=name,
    )(h, h, h, gain, shift, scale, w_up, w_up, conv_w, conv_w)


def _head(z_ref, h, gain_ref=None, rope=None, scale=1.0):
    x = z_ref[:, h * HEAD_DIM:(h + 1) * HEAD_DIM].astype(F32)
    if gain_ref is not None:
        ms = jnp.mean(x * x, axis=-1, keepdims=True)
        x = x * lax.rsqrt(ms + EPS) * gain_ref[...]
    if rope is not None:
        x = x * rope[0][...] + pltpu.roll(x, HEAD_DIM // 2, 1) * rope[1][...]
    if scale != 1.0:
        x = x * scale
    return x


def _prep_latent_kernel(nq_ref, nk_ref, zqa_ref, zqb_ref, zk_ref, zv_ref, nqg, nkg, gqg, gkg, cos_ref, sin_ref,
                        nq_o, nk_o, gqT_o, k_o, vT_o, *, qscale):
    rope = (cos_ref, sin_ref)
    for h in range(NA_HEADS):
        sl = slice(h * HEAD_DIM, (h + 1) * HEAD_DIM)
        nq_o[:, sl] = _head(nq_ref, h, nqg, scale=qscale).astype(BF16)
        nk_o[:, sl] = _head(nk_ref, h, nkg).astype(BF16)
    for h in range(GQA_Q_HEADS):
        src = zqa_ref if h < GQA_Q_HEADS // 2 else zqb_ref
        gqT_o[h] = _head(src, h % (GQA_Q_HEADS // 2), gqg, rope, qscale).T.astype(BF16)
    for h in range(GQA_KV_HEADS):
        k_o[h] = _head(zk_ref, h, gkg, rope).astype(BF16)
        vT_o[h] = _head(zv_ref, h).T.astype(BF16)


def prep_latent(zg, gains, rope, qscale):
    m = zg.shape[0]
    tm = 1024
    col = lambda c0, w: pl.BlockSpec((tm, w), lambda i: (i, c0 // w))
    gain = pl.BlockSpec((1, HEAD_DIM), lambda i: (0, 0))
    tab = pl.BlockSpec((tm, HEAD_DIM), lambda i: (i, 0))
    return pl.pallas_call(
        functools.partial(_prep_latent_kernel, qscale=qscale),
        out_shape=(jax.ShapeDtypeStruct((m, NA_W), BF16), jax.ShapeDtypeStruct((m, NA_W), BF16),
                   jax.ShapeDtypeStruct((GQA_Q_HEADS, HEAD_DIM, m), BF16),
                   jax.ShapeDtypeStruct((GQA_KV_HEADS, m, HEAD_DIM), BF16),
                   jax.ShapeDtypeStruct((GQA_KV_HEADS, HEAD_DIM, m), BF16)),
        grid=(m // tm,),
        in_specs=[col(_C_NQ, NA_W), col(_C_NK, NA_W), col(_C_ZQ, GQA_W // 2), col(_C_ZQ + GQA_W // 2, GQA_W // 2),
                  col(_C_ZK, GQA_KV_W), col(_C_ZV, GQA_KV_W), gain, gain, gain, gain, tab, tab],
        out_specs=(pl.BlockSpec((tm, NA_W), lambda i: (i, 0)), pl.BlockSpec((tm, NA_W), lambda i: (i, 0)),
                   pl.BlockSpec((GQA_Q_HEADS, HEAD_DIM, tm), lambda i: (0, 0, i)),
                   pl.BlockSpec((GQA_KV_HEADS, tm, HEAD_DIM), lambda i: (0, i, 0)),
                   pl.BlockSpec((GQA_KV_HEADS, HEAD_DIM, tm), lambda i: (0, 0, i))),
        compiler_params=_cp("parallel"),
        name="prep_latent",
    )(zg, zg, zg, zg, zg, zg, *gains, *rope)


def _prep_ctx_kernel(nq_ref, nk_ref, nv_ref, zqa_ref, zqb_ref, zk_ref, zv_ref, nqg, nkg, gqg, gkg,
                     nk_o, cnqT_o, cnk_o, cnvT_o, cqT_o, ck_o, cvT_o, *, qscale):
    for h in range(NA_HEADS):
        sl = slice(h * HEAD_DIM, (h + 1) * HEAD_DIM)
        kn = _head(nk_ref, h, nkg).astype(BF16)
        nk_o[:, sl] = kn
        cnk_o[h] = kn
        cnqT_o[h] = _head(nq_ref, h, nqg, scale=qscale).T.astype(BF16)
        cnvT_o[h] = _head(nv_ref, h).T.astype(BF16)
    for h in range(GQA_Q_HEADS):
        src = zqa_ref if h < GQA_Q_HEADS // 2 else zqb_ref
        cqT_o[h] = _head(src, h % (GQA_Q_HEADS // 2), gqg, scale=qscale).T.astype(BF16)
    for h in range(GQA_KV_HEADS):
        ck_o[h] = _head(zk_ref, h, gkg).astype(BF16)
        cvT_o[h] = _head(zv_ref, h).T.astype(BF16)


def prep_ctx(czg, gains, qscale):
    m = czg.shape[0]
    col = lambda c0, w: pl.BlockSpec((m, w), lambda i: (0, c0 // w))
    gain = pl.BlockSpec((1, HEAD_DIM), lambda i: (0, 0))
    full = lambda shape: pl.BlockSpec(shape, lambda i: (0,) * len(shape))
    return pl.pallas_call(
        functools.partial(_prep_ctx_kernel, qscale=qscale),
        out_shape=(jax.ShapeDtypeStruct((m, NA_W), BF16),
                   jax.ShapeDtypeStruct((NA_HEADS, HEAD_DIM, m), BF16),
                   jax.ShapeDtypeStruct((NA_HEADS, m, HEAD_DIM), BF16),
                   jax.ShapeDtypeStruct((NA_HEADS, HEAD_DIM, m), BF16),
                   jax.ShapeDtypeStruct((GQA_Q_HEADS, HEAD_DIM, m), BF16),
                   jax.ShapeDtypeStruct((GQA_KV_HEADS, m, HEAD_DIM), BF16),
                   jax.ShapeDtypeStruct((GQA_KV_HEADS, HEAD_DIM, m), BF16)),
        grid=(1,),
        in_specs=[col(_C_NQ, NA_W), col(_C_NK, NA_W), col(_C_NV, NA_W), col(_C_ZQ, GQA_W // 2),
                  col(_C_ZQ + GQA_W // 2, GQA_W // 2), col(_C_ZK, GQA_KV_W), col(_C_ZV, GQA_KV_W),
                  gain, gain, gain, gain],
        out_specs=(full((m, NA_W)),
                   full((NA_HEADS, HEAD_DIM, m)), full((NA_HEADS, m, HEAD_DIM)), full((NA_HEADS, HEAD_DIM, m)),
                   full((GQA_Q_HEADS, HEAD_DIM, m)), full((GQA_KV_HEADS, m, HEAD_DIM)),
                   full((GQA_KV_HEADS, HEAD_DIM, m))),
        compiler_params=_cp("arbitrary"),
        name="prep_ctx",
    )(czg, czg, czg, czg, czg, czg, czg, *gains)


SCORE_BOUND_SAFE = 60.0
FLASH_TQ = 512
FLASH_UNITS = 32


def _flash_kernel(*refs, group, tk, has_extra):
    if has_extra:
        qT_ref, k_ref, vT_ref, k2_ref, vT2_ref, o_ref, kmax_ref, acc_ref = refs
    else:
        qT_ref, k_ref, vT_ref, o_ref, kmax_ref, acc_ref = refs
        k2_ref = vT2_ref = None
    tq = min(qT_ref.shape[2], FLASH_TQ)
    nsub = qT_ref.shape[2] // tq
    cols = [(h, sub) for h in range(group) for sub in range(nsub)]
    n_keys = k_ref.shape[1]
    n_full = n_keys // tk
    rem = n_keys - n_full * tk
    tails = ([(k_ref, vT_ref, n_full * tk, rem)] if rem > 0 else []) + (
        [(k2_ref, vT2_ref, 0, k2_ref.shape[1])] if has_extra else [])

    @pl.when(pl.program_id(1) == 0)
    def _():
        def ksq(kr, start, size):
            kb = kr[0, pl.ds(start, size), :].astype(F32)
            return jnp.max(jnp.sum(kb * kb, axis=-1, keepdims=True), axis=0, keepdims=True)

        mx = jnp.zeros((1, 1), F32)
        if n_full > 0:
            mx = lax.fori_loop(
                0, n_full, lambda i, c: jnp.maximum(c, ksq(k_ref, pl.multiple_of(i * tk, tk), tk)), mx)
        for kr, _, start, size in tails:
            mx = jnp.maximum(mx, ksq(kr, start, size))
        kmax_ref[...] = mx

    def run_blocks(step, carry):
        if n_full > 0:
            carry = lax.fori_loop(
                0, n_full, lambda kb, c: step(k_ref, vT_ref, pl.multiple_of(kb * tk, tk), tk, c), carry)
        for kr, vr, start, size in tails:
            carry = step(kr, vr, start, size, carry)
        return carry

    qTs = [qT_ref[h, :, sub * tq:(sub + 1) * tq] for h, sub in cols]
    shifts = []
    for qT in qTs:
        qf = qT.astype(F32)
        qsq = jnp.sum(qf * qf, axis=0, keepdims=True)
        shifts.append(jnp.sqrt(qsq * kmax_ref[...]) * 1.01)
    safe = jnp.max(functools.reduce(jnp.maximum, shifts)) <= SCORE_BOUND_SAFE

    @pl.when(safe)
    def _():
        def step(kr, vr, start, size, l8s, nb=1):
            units = [(j, c) for j in range(nb) for c in range(len(cols))]
            kblks = [kr[0, pl.ds(start + j * size, size), :] for j in range(nb)]
            vblks = [vr[0, :, pl.ds(start + j * size, size)] for j in range(nb)]
            l8s = list(l8s)
            s_next = jnp.dot(kblks[0], qTs[0], preferred_element_type=F32)
            for u, (j, c) in enumerate(units):
                s = s_next
                if u + 1 < len(units):
                    jn, cn = units[u + 1]
                    s_next = jnp.dot(kblks[jn], qTs[cn], preferred_element_type=F32)
                p = jnp.exp2(s - shifts[c])
                l8s[c] = l8s[c] + jnp.sum(p.reshape(size // 8, 8, tq), axis=0)
                acc_ref[c] += jnp.dot(vblks[j], p.astype(BF16), preferred_element_type=F32)
            return tuple(l8s)

        acc_ref[...] = jnp.zeros_like(acc_ref)
        l8s = tuple(jnp.zeros((8, tq), F32) for _ in cols)
        unroll = max(1, FLASH_UNITS // len(cols))
        n_body = n_full // unroll
        if n_body > 0:
            l8s = lax.fori_loop(
                0, n_body,
                lambda kb, c: step(k_ref, vT_ref, pl.multiple_of(kb * (tk * unroll), tk * unroll), tk, c,
                                   nb=unroll), l8s)
        for kb in range(n_body * unroll, n_full):
            l8s = step(k_ref, vT_ref, kb * tk, tk, l8s)
        for kr, vr, start, size in tails:
            l8s = step(kr, vr, start, size, l8s)
        for c in range(len(cols)):
            l = jnp.sum(l8s[c], axis=0, keepdims=True)
            acc_ref[c] = acc_ref[c] * (1.0 / l)

    @pl.when(jnp.logical_not(safe))
    def _():
        for c in range(len(cols)):
            def step(kr, vr, start, size, carry, qT=qTs[c]):
                m, l, acc = carry
                kblk = kr[0, pl.ds(start, size), :]
                s = jnp.dot(kblk, qT, preferred_element_type=F32)
                m_new = jnp.maximum(m, jnp.max(s, axis=0, keepdims=True))
                alpha = jnp.exp2(m - m_new)
                p = jnp.exp2(s - m_new)
                l = alpha * l + jnp.sum(p, axis=0, keepdims=True)
                vblk = vr[0, :, pl.ds(start, size)]
                acc = alpha * acc + jnp.dot(vblk, p.astype(BF16), preferred_element_type=F32)
                return m_new, l, acc

            carry = (jnp.full((1, tq), NEG_BIG, F32), jnp.zeros((1, tq), F32),
                     jnp.zeros((HEAD_DIM, tq), F32))
            _, l, acc = run_blocks(step, carry)
            acc_ref[c] = acc * (1.0 / l)

    for c, (h, sub) in enumerate(cols):
        o_ref[sub * tq:(sub + 1) * tq, h * HEAD_DIM:(h + 1) * HEAD_DIM] = acc_ref[c].T.astype(o_ref.dtype)


def flash_attention(qT, k, vT, tq, extra=None):
    hq, _, mq = qT.shape
    hkv = k.shape[0]
    group = hq // hkv
    kv_specs = lambda kk, vv: [pl.BlockSpec((1, kk.shape[1], HEAD_DIM), lambda g, i: (g, 0, 0)),
                               pl.BlockSpec((1, HEAD_DIM, vv.shape[2]), lambda g, i: (g, 0, 0))]
    args, specs = [k, vT], kv_specs(k, vT)
    if extra is not None:
        args += list(extra)
        specs += kv_specs(*extra)
    return pl.pallas_call(
        functools.partial(_flash_kernel, group=group, tk=512, has_extra=extra is not None),
        out_shape=jax.ShapeDtypeStruct((mq, hq * HEAD_DIM), BF16),
        grid=(hkv, mq // tq),
        in_specs=[pl.BlockSpec((group, HEAD_DIM, tq), lambda g, i: (g, 0, i))] + specs,
        out_specs=pl.BlockSpec((tq, group * HEAD_DIM), lambda g, i: (i, g)),
        scratch_shapes=[pltpu.VMEM((1, 1), F32),
                        pltpu.VMEM((group * max(1, tq // FLASH_TQ), HEAD_DIM, min(tq, FLASH_TQ)), F32)],
        compiler_params=_cp("parallel", "arbitrary"),
        name="flash_attention",
    )(qT, *args)


NA_RB = 8
NA_TOK = NA_RB * GRID_W
NA_PAIR = 2 * GRID_W
NA_SLAB_ROWS = NA_ROWS + 2
NA_SLAB = NA_SLAB_ROWS * GRID_W
NA_PATTERNS = 5


def _na_kernel(q_ref, kp, kc, kn, vp, vc, vn, ck_ref, cv_ref, bias_ref, o_ref, kbuf, vbuf):
    b = pl.program_id(0)
    rows = pl.num_programs(0) * NA_RB
    kbuf[0:NA_TOK, :] = kp[...]
    kbuf[NA_TOK:2 * NA_TOK, :] = kc[...]
    kbuf[2 * NA_TOK:3 * NA_TOK, :] = kn[...]
    vbuf[0:NA_TOK, :] = vp[...]
    vbuf[NA_TOK:2 * NA_TOK, :] = vc[...]
    vbuf[2 * NA_TOK:3 * NA_TOK, :] = vn[...]
    nt = (((1,), (1,)), ((), ()))

    def slab(pair):
        r = b * NA_RB + 2 * pair
        u0 = jnp.clip(r - NA_ROWS // 2, 0, rows - NA_ROWS)
        off = pl.multiple_of((u0 - b * NA_RB + NA_RB) * GRID_W, NA_PAIR)
        pat = jnp.where(r < NA_ROWS // 2, r // 2,
                        jnp.where(r >= rows - NA_ROWS // 2, 3 + (r - (rows - NA_ROWS // 2)) // 2, 2))
        return off, pat

    slabs = [slab(pair) for pair in range(NA_RB // 2)]
    units = [(pair, h) for pair in range(NA_RB // 2) for h in range(NA_HEADS)]

    def scores(pair, h):
        off, pat = slabs[pair]
        sl = slice(h * HEAD_DIM, (h + 1) * HEAD_DIM)
        qh = q_ref[pair * NA_PAIR:(pair + 1) * NA_PAIR, sl]
        s = lax.dot_general(qh, kbuf[pl.ds(off, NA_SLAB), sl], nt, preferred_element_type=F32)
        sc = lax.dot_general(qh, ck_ref[:, sl], nt, preferred_element_type=F32)
        return s + bias_ref[pat, h], sc

    nxt = scores(*units[0])
    for u, (pair, h) in enumerate(units):
        s, sc = nxt
        if u + 1 < len(units):
            nxt = scores(*units[u + 1])
        off, _ = slabs[pair]
        sl = slice(h * HEAD_DIM, (h + 1) * HEAD_DIM)
        m = jnp.maximum(jnp.max(s, axis=-1, keepdims=True), jnp.max(sc, axis=-1, keepdims=True))
        p = jnp.exp2(s - m)
        pc = jnp.exp2(sc - m)
        l = jnp.sum(p, axis=-1, keepdims=True) + jnp.sum(pc, axis=-1, keepdims=True)
        o = (jnp.dot(p.astype(BF16), vbuf[pl.ds(off, NA_SLAB), sl], preferred_element_type=F32)
             + jnp.dot(pc.astype(BF16), cv_ref[:, sl], preferred_element_type=F32))
        o_ref[pair * NA_PAIR:(pair + 1) * NA_PAIR, sl] = (o * (1.0 / l)).astype(o_ref.dtype)


def neighborhood_attention(qn, kn, zg, ckn, czg, bias):
    m = qn.shape[0]
    nb = m // NA_TOK
    vcol = _C_NV // NA_W
    blk = (NA_TOK, NA_W)
    return pl.pallas_call(
        _na_kernel,
        out_shape=jax.ShapeDtypeStruct((m, NA_W), BF16),
        grid=(nb,),
        in_specs=[pl.BlockSpec(blk, lambda b: (b, 0)),
                  pl.BlockSpec(blk, lambda b: (jnp.maximum(b - 1, 0), 0)),
                  pl.BlockSpec(blk, lambda b: (b, 0)),
                  pl.BlockSpec(blk, lambda b: (jnp.minimum(b + 1, nb - 1), 0)),
                  pl.BlockSpec(blk, lambda b: (jnp.maximum(b - 1, 0), vcol)),
                  pl.BlockSpec(blk, lambda b: (b, vcol)),
                  pl.BlockSpec(blk, lambda b: (jnp.minimum(b + 1, nb - 1), vcol)),
                  pl.BlockSpec((CTX_LEN, NA_W), lambda b: (0, 0)),
                  pl.BlockSpec((CTX_LEN, NA_W), lambda b: (0, vcol)),
                  pl.BlockSpec((NA_PATTERNS, NA_HEADS, NA_PAIR, NA_SLAB), lambda b: (0, 0, 0, 0))],
        out_specs=pl.BlockSpec(blk, lambda b: (b, 0)),
        scratch_shapes=[pltpu.VMEM((3 * NA_TOK, NA_W), BF16), pltpu.VMEM((3 * NA_TOK, NA_W), BF16)],
        compiler_params=_cp("arbitrary"),
        name="neighborhood_attention",
    )(qn, kn, kn, kn, zg, zg, zg, ckn, czg, bias)


def na_bias_table(rpb):
    col = np.arange(GRID_W)
    c0 = np.clip(col - NA_COLS // 2, 0, GRID_W - NA_COLS)
    kc = np.arange(GRID_W)
    inside = (kc[None, :] >= c0[:, None]) & (kc[None, :] < c0[:, None] + NA_COLS)
    dc = kc[None, :] - col[:, None] + NA_COLS - 1
    onehot = (dc[None] == np.arange(2 * NA_COLS - 1)[:, None, None]) & inside[None]
    t = jnp.einsum('hrd,dck->hrck', rpb.astype(F32) * LOG2E, jnp.asarray(onehot, F32),
                   precision=lax.Precision.HIGHEST)
    t = jnp.where(inside[None, None], t, NEG_BIG)
    masked = jnp.full((NA_HEADS, GRID_W, GRID_W), NEG_BIG, F32)
    patterns = [(0, 0), (2, 0), (4, 1), (4, 0), (6, 0)]
    tabs = []
    for a, e in patterns:
        per_q = []
        for start, dr0 in ((0, NA_ROWS - 1 - a), (e, NA_ROWS - 2 - a)):
            blocks = [t[:, w + dr0] if start <= w < start + NA_ROWS else masked for w in range(NA_SLAB_ROWS)]
            per_q.append(jnp.stack(blocks, axis=2).reshape(NA_HEADS, GRID_W, NA_SLAB))
        tabs.append(jnp.concatenate(per_q, axis=1))
    return jnp.stack(tabs, axis=0)


def _dft_mats(n):
    a = 2.0 * np.pi * np.outer(np.arange(n), np.arange(n)) / n
    return np.cos(a), np.sin(a)


def _chan_dft_kernel(z_ref, w_ref, o_ref):
    for g in range(FOURIER_GROUPS):
        x = z_ref[:, g * HEAD_DIM:(g + 1) * HEAD_DIM]
        r = jnp.dot(x, w_ref[...], preferred_element_type=F32)
        o_ref[0, :, g * HEAD_DIM:(g + 1) * HEAD_DIM] = r[:, :HEAD_DIM].astype(o_ref.dtype)
        o_ref[1, :, g * HEAD_DIM:(g + 1) * HEAD_DIM] = r[:, HEAD_DIM:].astype(o_ref.dtype)


def chan_dft(zg, wch):
    m = zg.shape[0]
    tm = min(m, 1024)
    return pl.pallas_call(
        _chan_dft_kernel,
        out_shape=jax.ShapeDtypeStruct((2, m, FOURIER_W), BF16),
        grid=(m // tm,),
        in_specs=[pl.BlockSpec((tm, FOURIER_W), lambda i: (i, _C_ZF // FOURIER_W)),
                  pl.BlockSpec((HEAD_DIM, 2 * HEAD_DIM), lambda i: (0, 0))],
        out_specs=pl.BlockSpec((2, tm, FOURIER_W), lambda i: (0, i, 0)),
        compiler_params=_cp("parallel"),
        name="chan_dft",
    )(zg, wch)


def _left_mm_kernel(l_ref, x_ref, o_ref, *, n_out):
    x = jnp.concatenate([x_ref[0], x_ref[1]], axis=0)
    r = jnp.dot(l_ref[...], x, preferred_element_type=F32)
    if n_out == 1:
        o_ref[...] = r.astype(o_ref.dtype)
    else:
        half = r.shape[0] // 2
        o_ref[0] = r[:half].astype(o_ref.dtype)
        o_ref[1] = r[half:].astype(o_ref.dtype)


def dft_stage1(zc, lmat):
    _, r, n = zc.shape
    tn = 4096
    return pl.pallas_call(
        functools.partial(_left_mm_kernel, n_out=2),
        out_shape=jax.ShapeDtypeStruct((2, r, n), BF16),
        grid=(n // tn,),
        in_specs=[pl.BlockSpec((2 * r, 2 * r), lambda j: (0, 0)),
                  pl.BlockSpec((2, r, tn), lambda j: (0, 0, j))],
        out_specs=pl.BlockSpec((2, r, tn), lambda j: (0, 0, j)),
        compiler_params=_cp("parallel"),
        name="dft_stage1",
    )(lmat, zc)


def _dft_stage3_kernel(l_ref, x_ref, o_ref, *, kb):
    for i in range(kb):
        x = jnp.concatenate([x_ref[0, i], x_ref[1, i]], axis=0)
        o_ref[i] = jnp.dot(l_ref[i], x, preferred_element_type=F32).astype(o_ref.dtype)


def dft_stage3(a, tables):
    _, n1, n2, c = a.shape
    kb = 8
    return pl.pallas_call(
        functools.partial(_dft_stage3_kernel, kb=kb),
        out_shape=jax.ShapeDtypeStruct((n1, n2, c), BF16),
        grid=(n1 // kb,),
        in_specs=[pl.BlockSpec((kb, n2, 2 * n2), lambda j: (j, 0, 0)),
                  pl.BlockSpec((2, kb, n2, c), lambda j: (0, j, 0, 0))],
        out_specs=pl.BlockSpec((kb, n2, c), lambda j: (j, 0, 0)),
        compiler_params=_cp("parallel"),
        name="dft_stage3",
    )(tables, a)


def dft_direct(zc, lmat):
    _, t, c = zc.shape
    return pl.pallas_call(
        functools.partial(_left_mm_kernel, n_out=1),
        out_shape=jax.ShapeDtypeStruct((t, c), BF16),
        grid=(1,),
        in_specs=[pl.BlockSpec((t, 2 * t), lambda j: (0, 0)),
                  pl.BlockSpec((2, t, c), lambda j: (0, 0, 0))],
        out_specs=pl.BlockSpec((t, c), lambda j: (0, 0)),
        compiler_params=_cp("arbitrary"),
        name="dft_direct",
    )(lmat, zc)


def _fourier_tables():
    c128, s128 = _dft_mats(HEAD_DIM)
    wch = np.concatenate([c128, -s128], axis=1)
    n1 = SEQ // HEAD_DIM
    l1 = np.block([[c128, s128], [-s128, c128]])
    k1 = np.arange(n1)[:, None, None]
    k2 = np.arange(n1)[None, :, None]
    t2 = np.arange(n1)[None, None, :]
    ang = 2.0 * np.pi * ((n1 * k2 + k1) * t2 % SEQ) / SEQ
    norm = 1.0 / math.sqrt(SEQ * HEAD_DIM)
    l3 = np.concatenate([np.cos(ang), np.sin(ang)], axis=2) * norm
    cc, sc = _dft_mats(CTX_LEN)
    lc = np.concatenate([cc, sc], axis=1) / math.sqrt(CTX_LEN * HEAD_DIM)
    as_bf = lambda a: jnp.asarray(a, F32).astype(BF16)
    return as_bf(wch), as_bf(l1), as_bf(l3), as_bf(lc)


def fourier_latent(zg, tabs):
    wch, l1, l3, _ = tabs
    n1 = SEQ // HEAD_DIM
    zc = chan_dft(zg, wch)
    a = dft_stage1(zc.reshape(2, n1, n1 * FOURIER_W), l1)
    o3 = dft_stage3(a.reshape(2, n1, n1, FOURIER_W), l3)
    return o3.transpose(1, 0, 2).reshape(SEQ, FOURIER_W)


def fourier_ctx(czg, tabs):
    wch, _, _, lc = tabs
    return dft_direct(chan_dft(czg, wch), lc)


def _rope_tables(n_tok):
    t = jnp.arange(n_tok)
    row = (t // GRID_W).astype(F32)
    col = (t % GRID_W).astype(F32)
    n_freq = HEAD_DIM // 4
    inv_freq = ROPE_THETA ** (-jnp.arange(n_freq, dtype=F32) / n_freq)
    ang = jnp.concatenate([row[:, None] * inv_freq, col[:, None] * inv_freq], axis=-1)
    cos, sin = jnp.cos(ang), jnp.sin(ang)
    return jnp.concatenate([cos, cos], axis=-1), jnp.concatenate([-sin, sin], axis=-1)


def kernel(x, c, ctx, c_ctx, w_ada, b_ada, norm1, w_in, conv_w, na_q_gain, na_k_gain, na_rpb, gqa_q_gain, gqa_k_gain, w_conv_out, w_fourier_out, w_na_out, w_gqa_out, w_gate, b_gate, w_o, norm2, w_up, ffn_conv_w, w_down):
    d = D_MODEL
    h = x[0]
    hc = ctx[0]
    rope = _rope_tables(SEQ)
    ftabs = _fourier_tables()
    qscale2 = HEAD_DIM ** -0.5 * LOG2E
    cvec = jnp.zeros((8, d), F32).at[0].set(c[0]).at[1].set(c_ctx)
    row = lambda v: v.reshape(1, -1)

    w_in_b, w_gate_b = w_in.astype(BF16), w_gate.astype(BF16)
    merge_w = [w.astype(BF16) for w in (w_conv_out, w_fourier_out, w_na_out, w_gqa_out)]
    w_o_b, w_up_b, w_down_b = w_o.astype(BF16), w_up.astype(BF16), w_down.astype(BF16)
    b_ada3 = b_ada.reshape(DEPTH, 1, -1)
    b_gate3 = b_gate.reshape(DEPTH, 1, -1)

    for i in range(DEPTH):
        last = i == DEPTH - 1
        bias_tab = na_bias_table(na_rpb[i])
        gains = (row(na_q_gain[i]), row(na_k_gain[i]), row(gqa_q_gain[i]), row(gqa_k_gain[i]))

        mods = ada_mod(cvec, w_ada, b_ada3, i)
        lat = [mods[0:1, k * d:(k + 1) * d] for k in range(6)]
        cm = [mods[1:2, k * d:(k + 1) * d] for k in range(6)]

        zg = in_proj(h, row(norm1[i]), lat[0], lat[1], w_in_b, w_gate_b, b_gate3, i, "in_proj")
        czg = in_proj(hc, row(norm1[i]), cm[0], cm[1], w_in_b, w_gate_b, b_gate3, i, "in_proj_ctx")

        cnk_rows, cnqT, cnk, cnvT, cqT, ck, cvT = prep_ctx(czg, gains, qscale2)
        nq, nk, gqT, gk, gvT = prep_latent(zg, gains, rope, qscale2)

        y_conv = conv_mixer(zg, conv_w, i)
        y_four = fourier_latent(zg, ftabs)
        y_na = neighborhood_attention(nq, nk, zg, cnk_rows, czg, bias_tab)
        y_gqa = flash_attention(gqT, gk, gvT, tq=1024, extra=(ck, cvT))

        merged = merge_branches([y_conv, y_four, y_na, y_gqa], merge_w, i, zg)
        h = mm_residual(merged, w_o_b, i, h, lat[2], "out_proj", in_place=i > 0)
        act = ffn_up_act(h, row(norm2[i]), lat[3], lat[4], w_up_b, ffn_conv_w, i, "ffn_up_act")
        h = mm_residual(act, w_down_b, i, h, lat[5], "ffn_down", in_place=True)

        if not last:
            cy_conv = conv_mixer(czg, conv_w, i)
            cy_four = fourier_ctx(czg, ftabs)
            cy_na = flash_attention(cnqT, cnk, cnvT, tq=CTX_LEN)
            cy_gqa = flash_attention(cqT, ck, cvT, tq=CTX_LEN)
            cmerged = merge_branches([cy_conv, cy_four, cy_na, cy_gqa], merge_w, i, czg)
            hc = mm_residual(cmerged, w_o_b, i, hc, cm[2], "out_proj_ctx", in_place=i > 0)
            cact = ffn_up_act(hc, row(norm2[i]), cm[3], cm[4], w_up_b, ffn_conv_w, i, "ffn_up_act_ctx")
            hc = mm_residual(cact, w_down_b, i, hc, cm[5], "ffn_down_ctx", in_place=True)

    return h[None]
```

```python
import functools
import math

import jax
import jax.numpy as jnp
import numpy as np
from jax import lax
from jax.experimental import pallas as pl
from jax.experimental.pallas import tpu as pltpu

D_MODEL = 2048
SEQ = 16384
DEPTH = 4
GRID_W = 64
CTX_LEN = 256
HEAD_DIM = 128
CONV_W = 512
FOURIER_GROUPS = 4
FOURIER_W = 512
NA_HEADS = 4
NA_W = 512
NA_ROWS = 8
NA_COLS = 16
GQA_Q_HEADS = 8
GQA_KV_HEADS = 2
GQA_GROUP = 4
GQA_W = 1024
GQA_KV_W = 256
N_BRANCH = 4
N_IN = 5120
D_FF = 5632
ROPE_THETA = 10000.0
EPS = 1e-6

_C_XA, _C_BG, _C_CG = 0, 512, 1024
_C_ZF = 1536
_C_NQ, _C_NK, _C_NV = 2048, 2560, 3072
_C_ZQ, _C_ZK, _C_ZV = 3584, 4608, 4864
_C_GATE = N_IN

BF16 = jnp.bfloat16
F32 = jnp.float32
V7X_VMEM_LIMIT = 52 * 1024 * 1024
NEG_BIG = -1e30
HALO = 16
NORM_CHUNK = 256
LOG2E = 1.4426950408889634


def _cp(*sem):
    return pltpu.CompilerParams(dimension_semantics=sem, vmem_limit_bytes=V7X_VMEM_LIMIT)


def _sigmoid(t):
    return 0.5 * jnp.tanh(0.5 * t) + 0.5


def _norm_mod(x, g, sh, sc):
    ms = jnp.mean(x * x, axis=-1, keepdims=True)
    return (x * lax.rsqrt(ms + EPS) * g) * (1.0 + sc) + sh


def _ada_kernel(c_ref, w_ref, b_ref, o_ref):
    c = c_ref[...]
    s = c * _sigmoid(c)
    o_ref[...] = jnp.dot(s.astype(BF16), w_ref[...].astype(BF16),
                         preferred_element_type=F32) + b_ref[...]


def ada_mod(cvec, w, b, layer):
    n = w.shape[2]
    tn = 1024
    return pl.pallas_call(
        _ada_kernel,
        out_shape=jax.ShapeDtypeStruct((8, n), F32),
        grid=(n // tn,),
        in_specs=[pl.BlockSpec((8, D_MODEL), lambda j: (0, 0)),
                  pl.BlockSpec((None, D_MODEL, tn), lambda j: (layer, 0, j)),
                  pl.BlockSpec((None, 1, tn), lambda j: (layer, 0, j))],
        out_specs=pl.BlockSpec((8, tn), lambda j: (0, j)),
        compiler_params=_cp("parallel"),
        name="ada_mod",
    )(cvec, w, b)


def _in_proj_kernel(h_ref, g_ref, sh_ref, sc_ref, win_ref, wgate_ref, b_ref, o_ref, xn_ref, *, n_plain):
    j = pl.program_id(1)
    tm = h_ref.shape[0]
    cm = min(tm, NORM_CHUNK)

    @pl.when(j == 0)
    def _():
        for c in range(tm // cm):
            rows = slice(c * cm, (c + 1) * cm)
            xc = _norm_mod(h_ref[rows, :], g_ref[...], sh_ref[...], sc_ref[...]).astype(BF16)
            xn_ref[rows, :] = xc
            o_ref[rows, :] = jnp.dot(xc, win_ref[...], preferred_element_type=F32).astype(o_ref.dtype)

    @pl.when(jnp.logical_and(j > 0, j < n_plain))
    def _():
        o_ref[...] = jnp.dot(xn_ref[...], win_ref[...], preferred_element_type=F32).astype(o_ref.dtype)

    @pl.when(j >= n_plain)
    def _():
        t = jnp.dot(xn_ref[...], wgate_ref[...], preferred_element_type=F32) + b_ref[...]
        o_ref[...] = _sigmoid(t).astype(o_ref.dtype)


def in_proj(h, gain, shift, scale, w_in, w_gate, b_gate, layer, name):
    m, d = h.shape
    n_in, n_gate = w_in.shape[2], w_gate.shape[2]
    tm = min(m, 1024)
    tn = 1024
    n_plain = n_in // tn
    vec = pl.BlockSpec((1, d), lambda i, j: (0, 0))
    return pl.pallas_call(
        functools.partial(_in_proj_kernel, n_plain=n_plain),
        out_shape=jax.ShapeDtypeStruct((m, n_in + n_gate), BF16),
        grid=(m // tm, (n_in + n_gate) // tn),
        in_specs=[pl.BlockSpec((tm, d), lambda i, j: (i, 0)), vec, vec, vec,
                  pl.BlockSpec((None, d, tn), lambda i, j: (layer, 0, jnp.minimum(j, n_plain - 1))),
                  pl.BlockSpec((None, d, tn), lambda i, j: (layer, 0, jnp.maximum(j - n_plain, 0))),
                  pl.BlockSpec((None, 1, tn), lambda i, j: (layer, 0, jnp.maximum(j - n_plain, 0)))],
        out_specs=pl.BlockSpec((tm, tn), lambda i, j: (i, j)),
        scratch_shapes=[pltpu.VMEM((tm, d), BF16)],
        compiler_params=_cp("parallel", "arbitrary"),
        name=name,
    )(h, gain, shift, scale, w_in, w_gate, b_gate)


def _mm_res_kernel(x_ref, w_ref, h_ref, g_ref, o_ref):
    acc = jnp.dot(x_ref[...], w_ref[...], preferred_element_type=F32)
    o_ref[...] = h_ref[...] + g_ref[...] * acc


def mm_residual(x, w, layer, h, gate, name, in_place):
    m, k = x.shape
    n = w.shape[2]
    tm = min(m, 1024)
    tn = 1024 if k <= D_MODEL else 512
    return pl.pallas_call(
        _mm_res_kernel,
        out_shape=jax.ShapeDtypeStruct((m, n), F32),
        grid=(m // tm, n // tn),
        in_specs=[pl.BlockSpec((tm, k), lambda i, j: (i, 0)),
                  pl.BlockSpec((None, k, tn), lambda i, j: (layer, 0, j)),
                  pl.BlockSpec((tm, tn), lambda i, j: (i, j)),
                  pl.BlockSpec((1, tn), lambda i, j: (0, j))],
        out_specs=pl.BlockSpec((tm, tn), lambda i, j: (i, j)),
        input_output_aliases={2: 0} if in_place else {},
        compiler_params=_cp("parallel", "arbitrary"),
        name=name,
    )(x, w, h, gate)


def _merge_kernel(yc, yf, yn, yg, wc, wf, wn, wg, g0, g1, g2, g3, o_ref):
    def term(y, w, g):
        return g[...].astype(F32) * jnp.dot(y[...], w[...], preferred_element_type=F32)

    o_ref[...] = (term(yc, wc, g0) + term(yf, wf, g1) + term(yn, wn, g2)
                  + term(yg, wg, g3)).astype(o_ref.dtype)


def merge_branches(ys, ws, layer, zg):
    m = zg.shape[0]
    tm = min(m, 1024)
    tn = 1024
    y_specs = [pl.BlockSpec((tm, y.shape[1]), lambda i, j: (i, 0)) for y in ys]
    w_specs = [pl.BlockSpec((None, w.shape[1], tn), lambda i, j: (layer, 0, j)) for w in ws]
    g_specs = [pl.BlockSpec((tm, tn), functools.partial(
        lambda i, j, b: (i, (_C_GATE + b * D_MODEL) // tn + j), b=b)) for b in range(N_BRANCH)]
    return pl.pallas_call(
        _merge_kernel,
        out_shape=jax.ShapeDtypeStruct((m, D_MODEL), BF16),
        grid=(m // tm, D_MODEL // tn),
        in_specs=y_specs + w_specs + g_specs,
        out_specs=pl.BlockSpec((tm, tn), lambda i, j: (i, j)),
        compiler_params=_cp("parallel", "arbitrary"),
        name="merge",
    )(*ys, *ws, zg, zg, zg, zg)


def _conv3(p, prev_row, next_row, w_ref):
    tm = p.shape[0]
    row = lax.broadcasted_iota(jnp.int32, p.shape, 0)
    up = jnp.where(row == 0, prev_row, pltpu.roll(p, 1, 0))
    dn = jnp.where(row == tm - 1, next_row, pltpu.roll(p, tm - 1, 0))
    return up * w_ref[0:1, :] + p * w_ref[1:2, :] + dn * w_ref[2:3, :]


def _halo_specs(tm, tc, m, col):
    nb = m // HALO
    per = tm // HALO
    main = pl.BlockSpec((tm, tc), lambda i: (i, col))
    prev = pl.BlockSpec((HALO, tc), lambda i: (jnp.maximum(i * per - 1, 0), col))
    nxt = pl.BlockSpec((HALO, tc), lambda i: (jnp.minimum((i + 1) * per, nb - 1), col))
    return main, prev, nxt


def _convmix_kernel(xa, xap, xan, cg, cgp, cgn, bg, w_ref, o_ref):
    i = pl.program_id(0)
    last = pl.num_programs(0) - 1
    p = cg[...].astype(F32) * xa[...].astype(F32)
    pp = cgp[...].astype(F32)[HALO - 1:HALO, :] * xap[...].astype(F32)[HALO - 1:HALO, :]
    pn = cgn[...].astype(F32)[0:1, :] * xan[...].astype(F32)[0:1, :]
    pp = jnp.where(i > 0, pp, 0.0)
    pn = jnp.where(i < last, pn, 0.0)
    o_ref[...] = (bg[...].astype(F32) * _conv3(p, pp, pn, w_ref)).astype(o_ref.dtype)


def conv_mixer(zg, w, layer):
    m = zg.shape[0]
    tm = min(m, 1024)
    tc = CONV_W
    xa = _halo_specs(tm, tc, m, _C_XA // tc)
    cg = _halo_specs(tm, tc, m, _C_CG // tc)
    bg = pl.BlockSpec((tm, tc), lambda i: (i, _C_BG // tc))
    return pl.pallas_call(
        _convmix_kernel,
        out_shape=jax.ShapeDtypeStruct((m, CONV_W), BF16),
        grid=(m // tm,),
        in_specs=[*xa, *cg, bg, pl.BlockSpec((None, 3, tc), lambda i: (layer, 0, 0))],
        out_specs=pl.BlockSpec((tm, tc), lambda i: (i, 0)),
        compiler_params=_cp("parallel"),
        name="conv_mixer",
    )(zg, zg, zg, zg, zg, zg, zg, w)


def _ffn_up_act_kernel(h_ref, hp_ref, hn_ref, g_ref, sh_ref, sc_ref, wa_ref, wg_ref, cwa_ref, cwg_ref,
                       o_ref, xn_ref):
    i = pl.program_id(0)
    j = pl.program_id(1)
    last = pl.num_programs(0) - 1
    tm = h_ref.shape[0]
    rows = tm + 2 * HALO

    cm = min(tm, NORM_CHUNK)
    nch = tm // cm

    def finish(ua, ug):
        def conv(u, cw_ref):
            c = (pltpu.roll(u, 1, 0) * cw_ref[0:1, :] + u * cw_ref[1:2, :]
                 + pltpu.roll(u, rows - 1, 0) * cw_ref[2:3, :])
            return c[HALO:HALO + tm]

        ca = conv(ua, cwa_ref)
        cgt = conv(ug, cwg_ref)
        o_ref[...] = (ca * _sigmoid(ca) * cgt).astype(o_ref.dtype)

    up = lambda x, w_ref: jnp.dot(x, w_ref[...], preferred_element_type=F32)

    @pl.when(j == 0)
    def _():
        nm = lambda x: _norm_mod(x, g_ref[...], sh_ref[...], sc_ref[...])
        xn_ref[0:HALO, :] = jnp.where(i > 0, nm(hp_ref[...]), 0.0).astype(BF16)
        uas, ugs = [], []
        for c in range(nch):
            lo, hi = c * cm, (c + 1) * cm
            xn_ref[HALO + lo:HALO + hi, :] = nm(h_ref[lo:hi, :]).astype(BF16)
            if c == nch - 1:
                xn_ref[HALO + tm:rows, :] = jnp.where(i < last, nm(hn_ref[...]), 0.0).astype(BF16)
            r0 = 0 if c == 0 else HALO + lo
            r1 = rows if c == nch - 1 else HALO + hi
            xc = xn_ref[r0:r1, :]
            uas.append(up(xc, wa_ref))
            ugs.append(up(xc, wg_ref))
        cat = lambda parts: parts[0] if len(parts) == 1 else jnp.concatenate(parts, axis=0)
        finish(cat(uas), cat(ugs))

    @pl.when(j > 0)
    def _():
        xn = xn_ref[...]
        finish(up(xn, wa_ref), up(xn, wg_ref))


def ffn_up_act(h, gain, shift, scale, w_up, conv_w, layer, name):
    m, d = h.shape
    tm = min(m, 1024)
    tn = 512
    nj = D_FF // tn
    nb = m // HALO
    per = tm // HALO
    vec = pl.BlockSpec((1, d), lambda i, j: (0, 0))
    return pl.pallas_call(
        _ffn_up_act_kernel,
        out_shape=jax.ShapeDtypeStruct((m, D_FF), BF16),
        grid=(m // tm, nj),
        in_specs=[pl.BlockSpec((tm, d), lambda i, j: (i, 0)),
                  pl.BlockSpec((HALO, d), lambda i, j: (jnp.maximum(i * per - 1, 0), 0)),
                  pl.BlockSpec((HALO, d), lambda i, j: (jnp.minimum((i + 1) * per, nb - 1), 0)),
                  vec, vec, vec,
                  pl.BlockSpec((None, d, tn), lambda i, j: (layer, 0, j)),
                  pl.BlockSpec((None, d, tn), lambda i, j: (layer, 0, nj + j)),
                  pl.BlockSpec((None, 3, tn), lambda i, j: (layer, 0, j)),
                  pl.BlockSpec((None, 3, tn), lambda i, j: (layer, 0, nj + j))],
        out_specs=pl.BlockSpec((tm, tn), lambda i, j: (i, j)),
        scratch_shapes=[pltpu.VMEM((tm + 2 * HALO, d), BF16)],
        compiler_params=_cp("parallel", "arbitrary"),
        name=name,
    )(h, h, h, gain, shift, scale, w_up, w_up, conv_w, conv_w)


def _head(z_ref, h, gain_ref=None, rope=None, scale=1.0):
    x = z_ref[:, h * HEAD_DIM:(h + 1) * HEAD_DIM].astype(F32)
    if gain_ref is not None:
        ms = jnp.mean(x * x, axis=-1, keepdims=True)
        x = x * lax.rsqrt(ms + EPS) * gain_ref[...]
    if rope is not None:
        x = x * rope[0][...] + pltpu.roll(x, HEAD_DIM // 2, 1) * rope[1][...]
    if scale != 1.0:
        x = x * scale
    return x


def _prep_latent_kernel(nq_ref, nk_ref, zqa_ref, zqb_ref, zk_ref, zv_ref, nqg, nkg, gqg, gkg, cos_ref, sin_ref,
                        nq_o, nk_o, gqT_o, k_o, vT_o, *, qscale):
    rope = (cos_ref, sin_ref)
    for h in range(NA_HEADS):
        sl = slice(h * HEAD_DIM, (h + 1) * HEAD_DIM)
        nq_o[:, sl] = _head(nq_ref, h, nqg, scale=qscale).astype(BF16)
        nk_o[:, sl] = _head(nk_ref, h, nkg).astype(BF16)
    for h in range(GQA_Q_HEADS):
        src = zqa_ref if h < GQA_Q_HEADS // 2 else zqb_ref
        gqT_o[h] = _head(src, h % (GQA_Q_HEADS // 2), gqg, rope, qscale).T.astype(BF16)
    for h in range(GQA_KV_HEADS):
        k_o[h] = _head(zk_ref, h, gkg, rope).astype(BF16)
        vT_o[h] = _head(zv_ref, h).T.astype(BF16)


def prep_latent(zg, gains, rope, qscale):
    m = zg.shape[0]
    tm = 1024
    col = lambda c0, w: pl.BlockSpec((tm, w), lambda i: (i, c0 // w))
    gain = pl.BlockSpec((1, HEAD_DIM), lambda i: (0, 0))
    tab = pl.BlockSpec((tm, HEAD_DIM), lambda i: (i, 0))
    return pl.pallas_call(
        functools.partial(_prep_latent_kernel, qscale=qscale),
        out_shape=(jax.ShapeDtypeStruct((m, NA_W), BF16), jax.ShapeDtypeStruct((m, NA_W), BF16),
                   jax.ShapeDtypeStruct((GQA_Q_HEADS, HEAD_DIM, m), BF16),
                   jax.ShapeDtypeStruct((GQA_KV_HEADS, m, HEAD_DIM), BF16),
                   jax.ShapeDtypeStruct((GQA_KV_HEADS, HEAD_DIM, m), BF16)),
        grid=(m // tm,),
        in_specs=[col(_C_NQ, NA_W), col(_C_NK, NA_W), col(_C_ZQ, GQA_W // 2), col(_C_ZQ + GQA_W // 2, GQA_W // 2),
                  col(_C_ZK, GQA_KV_W), col(_C_ZV, GQA_KV_W), gain, gain, gain, gain, tab, tab],
        out_specs=(pl.BlockSpec((tm, NA_W), lambda i: (i, 0)), pl.BlockSpec((tm, NA_W), lambda i: (i, 0)),
                   pl.BlockSpec((GQA_Q_HEADS, HEAD_DIM, tm), lambda i: (0, 0, i)),
                   pl.BlockSpec((GQA_KV_HEADS, tm, HEAD_DIM), lambda i: (0, i, 0)),
                   pl.BlockSpec((GQA_KV_HEADS, HEAD_DIM, tm), lambda i: (0, 0, i))),
        compiler_params=_cp("parallel"),
        name="prep_latent",
    )(zg, zg, zg, zg, zg, zg, *gains, *rope)


def _prep_ctx_kernel(nq_ref, nk_ref, nv_ref, zqa_ref, zqb_ref, zk_ref, zv_ref, nqg, nkg, gqg, gkg,
                     nk_o, cnqT_o, cnk_o, cnvT_o, cqT_o, ck_o, cvT_o, *, qscale):
    for h in range(NA_HEADS):
        sl = slice(h * HEAD_DIM, (h + 1) * HEAD_DIM)
        kn = _head(nk_ref, h, nkg).astype(BF16)
        nk_o[:, sl] = kn
        cnk_o[h] = kn
        cnqT_o[h] = _head(nq_ref, h, nqg, scale=qscale).T.astype(BF16)
        cnvT_o[h] = _head(nv_ref, h).T.astype(BF16)
    for h in range(GQA_Q_HEADS):
        src = zqa_ref if h < GQA_Q_HEADS // 2 else zqb_ref
        cqT_o[h] = _head(src, h % (GQA_Q_HEADS // 2), gqg, scale=qscale).T.astype(BF16)
    for h in range(GQA_KV_HEADS):
        ck_o[h] = _head(zk_ref, h, gkg).astype(BF16)
        cvT_o[h] = _head(zv_ref, h).T.astype(BF16)


def prep_ctx(czg, gains, qscale):
    m = czg.shape[0]
    col = lambda c0, w: pl.BlockSpec((m, w), lambda i: (0, c0 // w))
    gain = pl.BlockSpec((1, HEAD_DIM), lambda i: (0, 0))
    full = lambda shape: pl.BlockSpec(shape, lambda i: (0,) * len(shape))
    return pl.pallas_call(
        functools.partial(_prep_ctx_kernel, qscale=qscale),
        out_shape=(jax.ShapeDtypeStruct((m, NA_W), BF16),
                   jax.ShapeDtypeStruct((NA_HEADS, HEAD_DIM, m), BF16),
                   jax.ShapeDtypeStruct((NA_HEADS, m, HEAD_DIM), BF16),
                   jax.ShapeDtypeStruct((NA_HEADS, HEAD_DIM, m), BF16),
                   jax.ShapeDtypeStruct((GQA_Q_HEADS, HEAD_DIM, m), BF16),
                   jax.ShapeDtypeStruct((GQA_KV_HEADS, m, HEAD_DIM), BF16),
                   jax.ShapeDtypeStruct((GQA_KV_HEADS, HEAD_DIM, m), BF16)),
        grid=(1,),
        in_specs=[col(_C_NQ, NA_W), col(_C_NK, NA_W), col(_C_NV, NA_W), col(_C_ZQ, GQA_W // 2),
                  col(_C_ZQ + GQA_W // 2, GQA_W // 2), col(_C_ZK, GQA_KV_W), col(_C_ZV, GQA_KV_W),
                  gain, gain, gain, gain],
        out_specs=(full((m, NA_W)),
                   full((NA_HEADS, HEAD_DIM, m)), full((NA_HEADS, m, HEAD_DIM)), full((NA_HEADS, HEAD_DIM, m)),
                   full((GQA_Q_HEADS, HEAD_DIM, m)), full((GQA_KV_HEADS, m, HEAD_DIM)),
                   full((GQA_KV_HEADS, HEAD_DIM, m))),
        compiler_params=_cp("arbitrary"),
        name="prep_ctx",
    )(czg, czg, czg, czg, czg, czg, czg, *gains)


SCORE_BOUND_SAFE = 60.0
FLASH_TQ = 512
FLASH_UNITS = 64


def _flash_kernel(*refs, group, tk, has_extra):
    if has_extra:
        qT_ref, k_ref, vT_ref, k2_ref, vT2_ref, o_ref, kmax_ref, acc_ref = refs
    else:
        qT_ref, k_ref, vT_ref, o_ref, kmax_ref, acc_ref = refs
        k2_ref = vT2_ref = None
    tq = min(qT_ref.shape[2], FLASH_TQ)
    nsub = qT_ref.shape[2] // tq
    cols = [(h, sub) for h in range(group) for sub in range(nsub)]
    n_keys = k_ref.shape[1]
    n_full = n_keys // tk
    rem = n_keys - n_full * tk
    tails = ([(k_ref, vT_ref, n_full * tk, rem)] if rem > 0 else []) + (
        [(k2_ref, vT2_ref, 0, k2_ref.shape[1])] if has_extra else [])

    @pl.when(pl.program_id(1) == 0)
    def _():
        def ksq(kr, start, size):
            kb = kr[0, pl.ds(start, size), :].astype(F32)
            return jnp.max(jnp.sum(kb * kb, axis=-1, keepdims=True), axis=0, keepdims=True)

        mx = jnp.zeros((1, 1), F32)
        if n_full > 0:
            mx = lax.fori_loop(
                0, n_full, lambda i, c: jnp.maximum(c, ksq(k_ref, pl.multiple_of(i * tk, tk), tk)), mx)
        for kr, _, start, size in tails:
            mx = jnp.maximum(mx, ksq(kr, start, size))
        kmax_ref[...] = mx

    def run_blocks(step, carry):
        if n_full > 0:
            carry = lax.fori_loop(
                0, n_full, lambda kb, c: step(k_ref, vT_ref, pl.multiple_of(kb * tk, tk), tk, c), carry)
        for kr, vr, start, size in tails:
            carry = step(kr, vr, start, size, carry)
        return carry

    qTs = [qT_ref[h, :, sub * tq:(sub + 1) * tq] for h, sub in cols]
    shifts = []
    for qT in qTs:
        qf = qT.astype(F32)
        qsq = jnp.sum(qf * qf, axis=0, keepdims=True)
        shifts.append(jnp.sqrt(qsq * kmax_ref[...]) * 1.01)
    safe = jnp.max(functools.reduce(jnp.maximum, shifts)) <= SCORE_BOUND_SAFE

    @pl.when(safe)
    def _():
        def step(kr, vr, start, size, l8s, nb=1):
            units = [(j, c) for j in range(nb) for c in range(len(cols))]
            kblks = [kr[0, pl.ds(start + j * size, size), :] for j in range(nb)]
            vblks = [vr[0, :, pl.ds(start + j * size, size)] for j in range(nb)]
            l8s = list(l8s)
            s_next = jnp.dot(kblks[0], qTs[0], preferred_element_type=F32)
            for u, (j, c) in enumerate(units):
                s = s_next
                if u + 1 < len(units):
                    jn, cn = units[u + 1]
                    s_next = jnp.dot(kblks[jn], qTs[cn], preferred_element_type=F32)
                p = jnp.exp2(s - shifts[c])
                l8s[c] = l8s[c] + jnp.sum(p.reshape(size // 8, 8, tq), axis=0)
                acc_ref[c] += jnp.dot(vblks[j], p.astype(BF16), preferred_element_type=F32)
            return tuple(l8s)

        acc_ref[...] = jnp.zeros_like(acc_ref)
        l8s = tuple(jnp.zeros((8, tq), F32) for _ in cols)
        unroll = max(1, FLASH_UNITS // len(cols))
        n_body = n_full // unroll
        if n_body > 0:
            l8s = lax.fori_loop(
                0, n_body,
                lambda kb, c: step(k_ref, vT_ref, pl.multiple_of(kb * (tk * unroll), tk * unroll), tk, c,
                                   nb=unroll), l8s)
        for kb in range(n_body * unroll, n_full):
            l8s = step(k_ref, vT_ref, kb * tk, tk, l8s)
        for kr, vr, start, size in tails:
            l8s = step(kr, vr, start, size, l8s)
        for c in range(len(cols)):
            l = jnp.sum(l8s[c], axis=0, keepdims=True)
            acc_ref[c] = acc_ref[c] * (1.0 / l)

    @pl.when(jnp.logical_not(safe))
    def _():
        for c in range(len(cols)):
            def step(kr, vr, start, size, carry, qT=qTs[c]):
                m, l, acc = carry
                kblk = kr[0, pl.ds(start, size), :]
                s = jnp.dot(kblk, qT, preferred_element_type=F32)
                m_new = jnp.maximum(m, jnp.max(s, axis=0, keepdims=True))
                alpha = jnp.exp2(m - m_new)
                p = jnp.exp2(s - m_new)
                l = alpha * l + jnp.sum(p, axis=0, keepdims=True)
                vblk = vr[0, :, pl.ds(start, size)]
                acc = alpha * acc + jnp.dot(vblk, p.astype(BF16), preferred_element_type=F32)
                return m_new, l, acc

            carry = (jnp.full((1, tq), NEG_BIG, F32), jnp.zeros((1, tq), F32),
                     jnp.zeros((HEAD_DIM, tq), F32))
            _, l, acc = run_blocks(step, carry)
            acc_ref[c] = acc * (1.0 / l)

    for c, (h, sub) in enumerate(cols):
        o_ref[sub * tq:(sub + 1) * tq, h * HEAD_DIM:(h + 1) * HEAD_DIM] = acc_ref[c].T.astype(o_ref.dtype)


def flash_attention(qT, k, vT, tq, extra=None):
    hq, _, mq = qT.shape
    hkv = k.shape[0]
    group = hq // hkv
    kv_specs = lambda kk, vv: [pl.BlockSpec((1, kk.shape[1], HEAD_DIM), lambda g, i: (g, 0, 0)),
                               pl.BlockSpec((1, HEAD_DIM, vv.shape[2]), lambda g, i: (g, 0, 0))]
    args, specs = [k, vT], kv_specs(k, vT)
    if extra is not None:
        args += list(extra)
        specs += kv_specs(*extra)
    return pl.pallas_call(
        functools.partial(_flash_kernel, group=group, tk=512, has_extra=extra is not None),
        out_shape=jax.ShapeDtypeStruct((mq, hq * HEAD_DIM), BF16),
        grid=(hkv, mq // tq),
        in_specs=[pl.BlockSpec((group, HEAD_DIM, tq), lambda g, i: (g, 0, i))] + specs,
        out_specs=pl.BlockSpec((tq, group * HEAD_DIM), lambda g, i: (i, g)),
        scratch_shapes=[pltpu.VMEM((1, 1), F32),
                        pltpu.VMEM((group * max(1, tq // FLASH_TQ), HEAD_DIM, min(tq, FLASH_TQ)), F32)],
        compiler_params=_cp("parallel", "arbitrary"),
        name="flash_attention",
    )(qT, *args)


NA_RB = 8
NA_TOK = NA_RB * GRID_W
NA_PAIR = 2 * GRID_W
NA_SLAB_ROWS = NA_ROWS + 2
NA_SLAB = NA_SLAB_ROWS * GRID_W
NA_PATTERNS = 5


def _na_kernel(q_ref, kp, kc, kn, vp, vc, vn, ck_ref, cv_ref, bias_ref, o_ref, kbuf, vbuf):
    b = pl.program_id(0)
    rows = pl.num_programs(0) * NA_RB
    kbuf[0:NA_TOK, :] = kp[...]
    kbuf[NA_TOK:2 * NA_TOK, :] = kc[...]
    kbuf[2 * NA_TOK:3 * NA_TOK, :] = kn[...]
    vbuf[0:NA_TOK, :] = vp[...]
    vbuf[NA_TOK:2 * NA_TOK, :] = vc[...]
    vbuf[2 * NA_TOK:3 * NA_TOK, :] = vn[...]
    nt = (((1,), (1,)), ((), ()))

    def slab(pair):
        r = b * NA_RB + 2 * pair
        u0 = jnp.clip(r - NA_ROWS // 2, 0, rows - NA_ROWS)
        off = pl.multiple_of((u0 - b * NA_RB + NA_RB) * GRID_W, NA_PAIR)
        pat = jnp.where(r < NA_ROWS // 2, r // 2,
                        jnp.where(r >= rows - NA_ROWS // 2, 3 + (r - (rows - NA_ROWS // 2)) // 2, 2))
        return off, pat

    slabs = [slab(pair) for pair in range(NA_RB // 2)]
    units = [(pair, h) for pair in range(NA_RB // 2) for h in range(NA_HEADS)]

    def scores(pair, h):
        off, pat = slabs[pair]
        sl = slice(h * HEAD_DIM, (h + 1) * HEAD_DIM)
        qh = q_ref[pair * NA_PAIR:(pair + 1) * NA_PAIR, sl]
        s = lax.dot_general(qh, kbuf[pl.ds(off, NA_SLAB), sl], nt, preferred_element_type=F32)
        sc = lax.dot_general(qh, ck_ref[:, sl], nt, preferred_element_type=F32)
        return s + bias_ref[pat, h], sc

    nxt = scores(*units[0])
    for u, (pair, h) in enumerate(units):
        s, sc = nxt
        if u + 1 < len(units):
            nxt = scores(*units[u + 1])
        off, _ = slabs[pair]
        sl = slice(h * HEAD_DIM, (h + 1) * HEAD_DIM)
        m = jnp.maximum(jnp.max(s, axis=-1, keepdims=True), jnp.max(sc, axis=-1, keepdims=True))
        p = jnp.exp2(s - m)
        pc = jnp.exp2(sc - m)
        l = jnp.sum(p, axis=-1, keepdims=True) + jnp.sum(pc, axis=-1, keepdims=True)
        o = (jnp.dot(p.astype(BF16), vbuf[pl.ds(off, NA_SLAB), sl], preferred_element_type=F32)
             + jnp.dot(pc.astype(BF16), cv_ref[:, sl], preferred_element_type=F32))
        o_ref[pair * NA_PAIR:(pair + 1) * NA_PAIR, sl] = (o * (1.0 / l)).astype(o_ref.dtype)


def neighborhood_attention(qn, kn, zg, ckn, czg, bias):
    m = qn.shape[0]
    nb = m // NA_TOK
    vcol = _C_NV // NA_W
    blk = (NA_TOK, NA_W)
    return pl.pallas_call(
        _na_kernel,
        out_shape=jax.ShapeDtypeStruct((m, NA_W), BF16),
        grid=(nb,),
        in_specs=[pl.BlockSpec(blk, lambda b: (b, 0)),
                  pl.BlockSpec(blk, lambda b: (jnp.maximum(b - 1, 0), 0)),
                  pl.BlockSpec(blk, lambda b: (b, 0)),
                  pl.BlockSpec(blk, lambda b: (jnp.minimum(b + 1, nb - 1), 0)),
                  pl.BlockSpec(blk, lambda b: (jnp.maximum(b - 1, 0), vcol)),
                  pl.BlockSpec(blk, lambda b: (b, vcol)),
                  pl.BlockSpec(blk, lambda b: (jnp.minimum(b + 1, nb - 1), vcol)),
                  pl.BlockSpec((CTX_LEN, NA_W), lambda b: (0, 0)),
                  pl.BlockSpec((CTX_LEN, NA_W), lambda b: (0, vcol)),
                  pl.BlockSpec((NA_PATTERNS, NA_HEADS, NA_PAIR, NA_SLAB), lambda b: (0, 0, 0, 0))],
        out_specs=pl.BlockSpec(blk, lambda b: (b, 0)),
        scratch_shapes=[pltpu.VMEM((3 * NA_TOK, NA_W), BF16), pltpu.VMEM((3 * NA_TOK, NA_W), BF16)],
        compiler_params=_cp("arbitrary"),
        name="neighborhood_attention",
    )(qn, kn, kn, kn, zg, zg, zg, ckn, czg, bias)


def na_bias_table(rpb):
    col = np.arange(GRID_W)
    c0 = np.clip(col - NA_COLS // 2, 0, GRID_W - NA_COLS)
    kc = np.arange(GRID_W)
    inside = (kc[None, :] >= c0[:, None]) & (kc[None, :] < c0[:, None] + NA_COLS)
    dc = kc[None, :] - col[:, None] + NA_COLS - 1
    onehot = (dc[None] == np.arange(2 * NA_COLS - 1)[:, None, None]) & inside[None]
    t = jnp.einsum('hrd,dck->hrck', rpb.astype(F32) * LOG2E, jnp.asarray(onehot, F32),
                   precision=lax.Precision.HIGHEST)
    t = jnp.where(inside[None, None], t, NEG_BIG)
    masked = jnp.full((NA_HEADS, GRID_W, GRID_W), NEG_BIG, F32)
    patterns = [(0, 0), (2, 0), (4, 1), (4, 0), (6, 0)]
    tabs = []
    for a, e in patterns:
        per_q = []
        for start, dr0 in ((0, NA_ROWS - 1 - a), (e, NA_ROWS - 2 - a)):
            blocks = [t[:, w + dr0] if start <= w < start + NA_ROWS else masked for w in range(NA_SLAB_ROWS)]
            per_q.append(jnp.stack(blocks, axis=2).reshape(NA_HEADS, GRID_W, NA_SLAB))
        tabs.append(jnp.concatenate(per_q, axis=1))
    return jnp.stack(tabs, axis=0)


def _dft_mats(n):
    a = 2.0 * np.pi * np.outer(np.arange(n), np.arange(n)) / n
    return np.cos(a), np.sin(a)


def _chan_dft_kernel(z_ref, w_ref, o_ref):
    for g in range(FOURIER_GROUPS):
        x = z_ref[:, g * HEAD_DIM:(g + 1) * HEAD_DIM]
        r = jnp.dot(x, w_ref[...], preferred_element_type=F32)
        o_ref[0, :, g * HEAD_DIM:(g + 1) * HEAD_DIM] = r[:, :HEAD_DIM].astype(o_ref.dtype)
        o_ref[1, :, g * HEAD_DIM:(g + 1) * HEAD_DIM] = r[:, HEAD_DIM:].astype(o_ref.dtype)


def chan_dft(zg, wch):
    m = zg.shape[0]
    tm = min(m, 1024)
    return pl.pallas_call(
        _chan_dft_kernel,
        out_shape=jax.ShapeDtypeStruct((2, m, FOURIER_W), BF16),
        grid=(m // tm,),
        in_specs=[pl.BlockSpec((tm, FOURIER_W), lambda i: (i, _C_ZF // FOURIER_W)),
                  pl.BlockSpec((HEAD_DIM, 2 * HEAD_DIM), lambda i: (0, 0))],
        out_specs=pl.BlockSpec((2, tm, FOURIER_W), lambda i: (0, i, 0)),
        compiler_params=_cp("parallel"),
        name="chan_dft",
    )(zg, wch)


def _left_mm_kernel(l_ref, x_ref, o_ref, *, n_out):
    x = jnp.concatenate([x_ref[0], x_ref[1]], axis=0)
    r = jnp.dot(l_ref[...], x, preferred_element_type=F32)
    if n_out == 1:
        o_ref[...] = r.astype(o_ref.dtype)
    else:
        half = r.shape[0] // 2
        o_ref[0] = r[:half].astype(o_ref.dtype)
        o_ref[1] = r[half:].astype(o_ref.dtype)


def dft_stage1(zc, lmat):
    _, r, n = zc.shape
    tn = 4096
    return pl.pallas_call(
        functools.partial(_left_mm_kernel, n_out=2),
        out_shape=jax.ShapeDtypeStruct((2, r, n), BF16),
        grid=(n // tn,),
        in_specs=[pl.BlockSpec((2 * r, 2 * r), lambda j: (0, 0)),
                  pl.BlockSpec((2, r, tn), lambda j: (0, 0, j))],
        out_specs=pl.BlockSpec((2, r, tn), lambda j: (0, 0, j)),
        compiler_params=_cp("parallel"),
        name="dft_stage1",
    )(lmat, zc)


def _dft_stage3_kernel(l_ref, x_ref, o_ref, *, kb):
    for i in range(kb):
        x = jnp.concatenate([x_ref[0, i], x_ref[1, i]], axis=0)
        o_ref[i] = jnp.dot(l_ref[i], x, preferred_element_type=F32).astype(o_ref.dtype)


def dft_stage3(a, tables):
    _, n1, n2, c = a.shape
    kb = 8
    return pl.pallas_call(
        functools.partial(_dft_stage3_kernel, kb=kb),
        out_shape=jax.ShapeDtypeStruct((n1, n2, c), BF16),
        grid=(n1 // kb,),
        in_specs=[pl.BlockSpec((kb, n2, 2 * n2), lambda j: (j, 0, 0)),
                  pl.BlockSpec((2, kb, n2, c), lambda j: (0, j, 0, 0))],
        out_specs=pl.BlockSpec((kb, n2, c), lambda j: (j, 0, 0)),
        compiler_params=_cp("parallel"),
        name="dft_stage3",
    )(tables, a)


def dft_direct(zc, lmat):
    _, t, c = zc.shape
    return pl.pallas_call(
        functools.partial(_left_mm_kernel, n_out=1),
        out_shape=jax.ShapeDtypeStruct((t, c), BF16),
        grid=(1,),
        in_specs=[pl.BlockSpec((t, 2 * t), lambda j: (0, 0)),
                  pl.BlockSpec((2, t, c), lambda j: (0, 0, 0))],
        out_specs=pl.BlockSpec((t, c), lambda j: (0, 0)),
        compiler_params=_cp("arbitrary"),
        name="dft_direct",
    )(lmat, zc)


def _fourier_tables():
    c128, s128 = _dft_mats(HEAD_DIM)
    wch = np.concatenate([c128, -s128], axis=1)
    n1 = SEQ // HEAD_DIM
    l1 = np.block([[c128, s128], [-s128, c128]])
    k1 = np.arange(n1)[:, None, None]
    k2 = np.arange(n1)[None, :, None]
    t2 = np.arange(n1)[None, None, :]
    ang = 2.0 * np.pi * ((n1 * k2 + k1) * t2 % SEQ) / SEQ
    norm = 1.0 / math.sqrt(SEQ * HEAD_DIM)
    l3 = np.concatenate([np.cos(ang), np.sin(ang)], axis=2) * norm
    cc, sc = _dft_mats(CTX_LEN)
    lc = np.concatenate([cc, sc], axis=1) / math.sqrt(CTX_LEN * HEAD_DIM)
    as_bf = lambda a: jnp.asarray(a, F32).astype(BF16)
    return as_bf(wch), as_bf(l1), as_bf(l3), as_bf(lc)


def fourier_latent(zg, tabs):
    wch, l1, l3, _ = tabs
    n1 = SEQ // HEAD_DIM
    zc = chan_dft(zg, wch)
    a = dft_stage1(zc.reshape(2, n1, n1 * FOURIER_W), l1)
    o3 = dft_stage3(a.reshape(2, n1, n1, FOURIER_W), l3)
    return o3.transpose(1, 0, 2).reshape(SEQ, FOURIER_W)


def fourier_ctx(czg, tabs):
    wch, _, _, lc = tabs
    return dft_direct(chan_dft(czg, wch), lc)


def _rope_tables(n_tok):
    t = jnp.arange(n_tok)
    row = (t // GRID_W).astype(F32)
    col = (t % GRID_W).astype(F32)
    n_freq = HEAD_DIM // 4
    inv_freq = ROPE_THETA ** (-jnp.arange(n_freq, dtype=F32) / n_freq)
    ang = jnp.concatenate([row[:, None] * inv_freq, col[:, None] * inv_freq], axis=-1)
    cos, sin = jnp.cos(ang), jnp.sin(ang)
    return jnp.concatenate([cos, cos], axis=-1), jnp.concatenate([-sin, sin], axis=-1)


def kernel(x, c, ctx, c_ctx, w_ada, b_ada, norm1, w_in, conv_w, na_q_gain, na_k_gain, na_rpb, gqa_q_gain, gqa_k_gain, w_conv_out, w_fourier_out, w_na_out, w_gqa_out, w_gate, b_gate, w_o, norm2, w_up, ffn_conv_w, w_down):
    d = D_MODEL
    h = x[0]
    hc = ctx[0]
    rope = _rope_tables(SEQ)
    ftabs = _fourier_tables()
    qscale2 = HEAD_DIM ** -0.5 * LOG2E
    cvec = jnp.zeros((8, d), F32).at[0].set(c[0]).at[1].set(c_ctx)
    row = lambda v: v.reshape(1, -1)

    w_in_b, w_gate_b = w_in.astype(BF16), w_gate.astype(BF16)
    merge_w = [w.astype(BF16) for w in (w_conv_out, w_fourier_out, w_na_out, w_gqa_out)]
    w_o_b, w_up_b, w_down_b = w_o.astype(BF16), w_up.astype(BF16), w_down.astype(BF16)
    b_ada3 = b_ada.reshape(DEPTH, 1, -1)
    b_gate3 = b_gate.reshape(DEPTH, 1, -1)

    for i in range(DEPTH):
        last = i == DEPTH - 1
        bias_tab = na_bias_table(na_rpb[i])
        gains = (row(na_q_gain[i]), row(na_k_gain[i]), row(gqa_q_gain[i]), row(gqa_k_gain[i]))

        mods = ada_mod(cvec, w_ada, b_ada3, i)
        lat = [mods[0:1, k * d:(k + 1) * d] for k in range(6)]
        cm = [mods[1:2, k * d:(k + 1) * d] for k in range(6)]

        zg = in_proj(h, row(norm1[i]), lat[0], lat[1], w_in_b, w_gate_b, b_gate3, i, "in_proj")
        czg = in_proj(hc, row(norm1[i]), cm[0], cm[1], w_in_b, w_gate_b, b_gate3, i, "in_proj_ctx")

        cnk_rows, cnqT, cnk, cnvT, cqT, ck, cvT = prep_ctx(czg, gains, qscale2)
        nq, nk, gqT, gk, gvT = prep_latent(zg, gains, rope, qscale2)

        y_conv = conv_mixer(zg, conv_w, i)
        y_four = fourier_latent(zg, ftabs)
        y_na = neighborhood_attention(nq, nk, zg, cnk_rows, czg, bias_tab)
        y_gqa = flash_attention(gqT, gk, gvT, tq=1024, extra=(ck, cvT))

        merged = merge_branches([y_conv, y_four, y_na, y_gqa], merge_w, i, zg)
        h = mm_residual(merged, w_o_b, i, h, lat[2], "out_proj", in_place=i > 0)
        act = ffn_up_act(h, row(norm2[i]), lat[3], lat[4], w_up_b, ffn_conv_w, i, "ffn_up_act")
        h = mm_residual(act, w_down_b, i, h, lat[5], "ffn_down", in_place=True)

        if not last:
            cy_conv = conv_mixer(czg, conv_w, i)
            cy_four = fourier_ctx(czg, ftabs)
            cy_na = flash_attention(cnqT, cnk, cnvT, tq=CTX_LEN)
            cy_gqa = flash_attention(cqT, ck, cvT, tq=CTX_LEN)
            cmerged = merge_branches([cy_conv, cy_four, cy_na, cy_gqa], merge_w, i, czg)
            hc = mm_residual(cmerged, w_o_b, i, hc, cm[2], "out_proj_ctx", in_place=i > 0)
            cact = ffn_up_act(hc, row(norm2[i]), cm[3], cm[4], w_up_b, ffn_conv_w, i, "ffn_up_act_ctx")
            hc = mm_residual(cact, w_down_b, i, hc, cm[5], "ffn_down_ctx", in_place=True)

    return h[None]
```

```python
import functools
import math

import jax
import jax.numpy as jnp
import numpy as np
from jax import lax
from jax.experimental import pallas as pl
from jax.experimental.pallas import tpu as pltpu

D_MODEL = 2048
SEQ = 16384
DEPTH = 4
GRID_W = 64
CTX_LEN = 256
HEAD_DIM = 128
CONV_W = 512
FOURIER_GROUPS = 4
FOURIER_W = 512
NA_HEADS = 4
NA_W = 512
NA_ROWS = 8
NA_COLS = 16
GQA_Q_HEADS = 8
GQA_KV_HEADS = 2
GQA_GROUP = 4
GQA_W = 1024
GQA_KV_W = 256
N_BRANCH = 4
N_IN = 5120
D_FF = 5632
ROPE_THETA = 10000.0
EPS = 1e-6

_C_XA, _C_BG, _C_CG = 0, 512, 1024
_C_ZF = 1536
_C_NQ, _C_NK, _C_NV = 2048, 2560, 3072
_C_ZQ, _C_ZK, _C_ZV = 3584, 4608, 4864
_C_GATE = N_IN

BF16 = jnp.bfloat16
F32 = jnp.float32
V7X_VMEM_LIMIT = 52 * 1024 * 1024
NEG_BIG = -1e30
HALO = 16
NORM_CHUNK = 256
LOG2E = 1.4426950408889634


def _cp(*sem):
    return pltpu.CompilerParams(dimension_semantics=sem, vmem_limit_bytes=V7X_VMEM_LIMIT)


def _sigmoid(t):
    return 0.5 * jnp.tanh(0.5 * t) + 0.5


def _norm_mod(x, g, sh, sc):
    ms = jnp.mean(x * x, axis=-1, keepdims=True)
    return (x * lax.rsqrt(ms + EPS) * g) * (1.0 + sc) + sh


def _ada_kernel(c_ref, w_ref, b_ref, o_ref):
    c = c_ref[...]
    s = c * _sigmoid(c)
    o_ref[...] = jnp.dot(s.astype(BF16), w_ref[...].astype(BF16),
                         preferred_element_type=F32) + b_ref[...]


def ada_mod(cvec, w, b, layer):
    n = w.shape[2]
    tn = 1024
    return pl.pallas_call(
        _ada_kernel,
        out_shape=jax.ShapeDtypeStruct((8, n), F32),
        grid=(n // tn,),
        in_specs=[pl.BlockSpec((8, D_MODEL), lambda j: (0, 0)),
                  pl.BlockSpec((None, D_MODEL, tn), lambda j: (layer, 0, j)),
                  pl.BlockSpec((None, 1, tn), lambda j: (layer, 0, j))],
        out_specs=pl.BlockSpec((8, tn), lambda j: (0, j)),
        compiler_params=_cp("parallel"),
        name="ada_mod",
    )(cvec, w, b)


def _in_proj_kernel(h_ref, g_ref, sh_ref, sc_ref, win_ref, wgate_ref, b_ref, o_ref, xn_ref, *, n_plain):
    j = pl.program_id(1)
    tm = h_ref.shape[0]
    cm = min(tm, NORM_CHUNK)

    @pl.when(j == 0)
    def _():
        for c in range(tm // cm):
            rows = slice(c * cm, (c + 1) * cm)
            xc = _norm_mod(h_ref[rows, :], g_ref[...], sh_ref[...], sc_ref[...]).astype(BF16)
            xn_ref[rows, :] = xc
            o_ref[rows, :] = jnp.dot(xc, win_ref[...], preferred_element_type=F32).astype(o_ref.dtype)

    @pl.when(jnp.logical_and(j > 0, j < n_plain))
    def _():
        o_ref[...] = jnp.dot(xn_ref[...], win_ref[...], preferred_element_type=F32).astype(o_ref.dtype)

    @pl.when(j >= n_plain)
    def _():
        t = jnp.dot(xn_ref[...], wgate_ref[...], preferred_element_type=F32) + b_ref[...]
        o_ref[...] = _sigmoid(t).astype(o_ref.dtype)


def in_proj(h, gain, shift, scale, w_in, w_gate, b_gate, layer, name):
    m, d = h.shape
    n_in, n_gate = w_in.shape[2], w_gate.shape[2]
    tm = min(m, 1024)
    tn = 1024
    n_plain = n_in // tn
    vec = pl.BlockSpec((1, d), lambda i, j: (0, 0))
    return pl.pallas_call(
        functools.partial(_in_proj_kernel, n_plain=n_plain),
        out_shape=jax.ShapeDtypeStruct((m, n_in + n_gate), BF16),
        grid=(m // tm, (n_in + n_gate) // tn),
        in_specs=[pl.BlockSpec((tm, d), lambda i, j: (i, 0)), vec, vec, vec,
                  pl.BlockSpec((None, d, tn), lambda i, j: (0, 0, jnp.minimum(j, n_plain - 1))),
                  pl.BlockSpec((None, d, tn), lambda i, j: (0, 0, jnp.maximum(j - n_plain, 0))),
                  pl.BlockSpec((None, 1, tn), lambda i, j: (layer, 0, jnp.maximum(j - n_plain, 0)))],
        out_specs=pl.BlockSpec((tm, tn), lambda i, j: (i, j)),
        scratch_shapes=[pltpu.VMEM((tm, d), BF16)],
        compiler_params=_cp("parallel", "arbitrary"),
        name=name,
    )(h, gain, shift, scale, w_in, w_gate, b_gate)


def _mm_res_kernel(x_ref, w_ref, h_ref, g_ref, o_ref):
    acc = jnp.dot(x_ref[...], w_ref[...], preferred_element_type=F32)
    o_ref[...] = h_ref[...] + g_ref[...] * acc


def mm_residual(x, w, h, gate, name, in_place):
    m, k = x.shape
    n = w.shape[2]
    tm = min(m, 1024)
    tn = 1024 if k <= D_MODEL else 512
    return pl.pallas_call(
        _mm_res_kernel,
        out_shape=jax.ShapeDtypeStruct((m, n), F32),
        grid=(m // tm, n // tn),
        in_specs=[pl.BlockSpec((tm, k), lambda i, j: (i, 0)),
                  pl.BlockSpec((None, k, tn), lambda i, j: (0, 0, j)),
                  pl.BlockSpec((tm, tn), lambda i, j: (i, j)),
                  pl.BlockSpec((1, tn), lambda i, j: (0, j))],
        out_specs=pl.BlockSpec((tm, tn), lambda i, j: (i, j)),
        input_output_aliases={2: 0} if in_place else {},
        compiler_params=_cp("parallel", "arbitrary"),
        name=name,
    )(x, w, h, gate)


def _merge_kernel(yc, yf, yn, yg, wc, wf, wn, wg, g0, g1, g2, g3, o_ref):
    def term(y, w, g):
        return g[...].astype(F32) * jnp.dot(y[...], w[...], preferred_element_type=F32)

    o_ref[...] = (term(yc, wc, g0) + term(yf, wf, g1) + term(yn, wn, g2)
                  + term(yg, wg, g3)).astype(o_ref.dtype)


def merge_branches(ys, ws, zg):
    m = zg.shape[0]
    tm = min(m, 1024)
    tn = 1024
    y_specs = [pl.BlockSpec((tm, y.shape[1]), lambda i, j: (i, 0)) for y in ys]
    w_specs = [pl.BlockSpec((None, w.shape[1], tn), lambda i, j: (0, 0, j)) for w in ws]
    g_specs = [pl.BlockSpec((tm, tn), functools.partial(
        lambda i, j, b: (i, (_C_GATE + b * D_MODEL) // tn + j), b=b)) for b in range(N_BRANCH)]
    return pl.pallas_call(
        _merge_kernel,
        out_shape=jax.ShapeDtypeStruct((m, D_MODEL), BF16),
        grid=(m // tm, D_MODEL // tn),
        in_specs=y_specs + w_specs + g_specs,
        out_specs=pl.BlockSpec((tm, tn), lambda i, j: (i, j)),
        compiler_params=_cp("parallel", "arbitrary"),
        name="merge",
    )(*ys, *ws, zg, zg, zg, zg)


def _conv3(p, prev_row, next_row, w_ref):
    tm = p.shape[0]
    row = lax.broadcasted_iota(jnp.int32, p.shape, 0)
    up = jnp.where(row == 0, prev_row, pltpu.roll(p, 1, 0))
    dn = jnp.where(row == tm - 1, next_row, pltpu.roll(p, tm - 1, 0))
    return up * w_ref[0:1, :] + p * w_ref[1:2, :] + dn * w_ref[2:3, :]


def _halo_specs(tm, tc, m, col):
    nb = m // HALO
    per = tm // HALO
    main = pl.BlockSpec((tm, tc), lambda i: (i, col))
    prev = pl.BlockSpec((HALO, tc), lambda i: (jnp.maximum(i * per - 1, 0), col))
    nxt = pl.BlockSpec((HALO, tc), lambda i: (jnp.minimum((i + 1) * per, nb - 1), col))
    return main, prev, nxt


def _convmix_kernel(xa, xap, xan, cg, cgp, cgn, bg, w_ref, o_ref):
    i = pl.program_id(0)
    last = pl.num_programs(0) - 1
    p = cg[...].astype(F32) * xa[...].astype(F32)
    pp = cgp[...].astype(F32)[HALO - 1:HALO, :] * xap[...].astype(F32)[HALO - 1:HALO, :]
    pn = cgn[...].astype(F32)[0:1, :] * xan[...].astype(F32)[0:1, :]
    pp = jnp.where(i > 0, pp, 0.0)
    pn = jnp.where(i < last, pn, 0.0)
    o_ref[...] = (bg[...].astype(F32) * _conv3(p, pp, pn, w_ref)).astype(o_ref.dtype)


def conv_mixer(zg, w, layer):
    m = zg.shape[0]
    tm = min(m, 1024)
    tc = CONV_W
    xa = _halo_specs(tm, tc, m, _C_XA // tc)
    cg = _halo_specs(tm, tc, m, _C_CG // tc)
    bg = pl.BlockSpec((tm, tc), lambda i: (i, _C_BG // tc))
    return pl.pallas_call(
        _convmix_kernel,
        out_shape=jax.ShapeDtypeStruct((m, CONV_W), BF16),
        grid=(m // tm,),
        in_specs=[*xa, *cg, bg, pl.BlockSpec((None, 3, tc), lambda i: (layer, 0, 0))],
        out_specs=pl.BlockSpec((tm, tc), lambda i: (i, 0)),
        compiler_params=_cp("parallel"),
        name="conv_mixer",
    )(zg, zg, zg, zg, zg, zg, zg, w)


def _ffn_up_act_kernel(h_ref, hp_ref, hn_ref, g_ref, sh_ref, sc_ref, wa_ref, wg_ref, cwa_ref, cwg_ref,
                       o_ref, xn_ref):
    i = pl.program_id(0)
    j = pl.program_id(1)
    last = pl.num_programs(0) - 1
    tm = h_ref.shape[0]
    rows = tm + 2 * HALO

    cm = min(tm, NORM_CHUNK)
    nch = tm // cm

    def finish(ua, ug):
        def conv(u, cw_ref):
            c = (pltpu.roll(u, 1, 0) * cw_ref[0:1, :] + u * cw_ref[1:2, :]
                 + pltpu.roll(u, rows - 1, 0) * cw_ref[2:3, :])
            return c[HALO:HALO + tm]

        ca = conv(ua, cwa_ref)
        cgt = conv(ug, cwg_ref)
        o_ref[...] = (ca * _sigmoid(ca) * cgt).astype(o_ref.dtype)

    up = lambda x, w_ref: jnp.dot(x, w_ref[...], preferred_element_type=F32)

    @pl.when(j == 0)
    def _():
        nm = lambda x: _norm_mod(x, g_ref[...], sh_ref[...], sc_ref[...])
        xn_ref[0:HALO, :] = jnp.where(i > 0, nm(hp_ref[...]), 0.0).astype(BF16)
        uas, ugs = [], []
        for c in range(nch):
            lo, hi = c * cm, (c + 1) * cm
            xn_ref[HALO + lo:HALO + hi, :] = nm(h_ref[lo:hi, :]).astype(BF16)
            if c == nch - 1:
                xn_ref[HALO + tm:rows, :] = jnp.where(i < last, nm(hn_ref[...]), 0.0).astype(BF16)
            r0 = 0 if c == 0 else HALO + lo
            r1 = rows if c == nch - 1 else HALO + hi
            xc = xn_ref[r0:r1, :]
            uas.append(up(xc, wa_ref))
            ugs.append(up(xc, wg_ref))
        cat = lambda parts: parts[0] if len(parts) == 1 else jnp.concatenate(parts, axis=0)
        finish(cat(uas), cat(ugs))

    @pl.when(j > 0)
    def _():
        xn = xn_ref[...]
        finish(up(xn, wa_ref), up(xn, wg_ref))


def ffn_up_act(h, gain, shift, scale, w_up, conv_w, layer, name):
    m, d = h.shape
    tm = min(m, 1024)
    tn = 512
    nj = D_FF // tn
    nb = m // HALO
    per = tm // HALO
    vec = pl.BlockSpec((1, d), lambda i, j: (0, 0))
    return pl.pallas_call(
        _ffn_up_act_kernel,
        out_shape=jax.ShapeDtypeStruct((m, D_FF), BF16),
        grid=(m // tm, nj),
        in_specs=[pl.BlockSpec((tm, d), lambda i, j: (i, 0)),
                  pl.BlockSpec((HALO, d), lambda i, j: (jnp.maximum(i * per - 1, 0), 0)),
                  pl.BlockSpec((HALO, d), lambda i, j: (jnp.minimum((i + 1) * per, nb - 1), 0)),
                  vec, vec, vec,
                  pl.BlockSpec((None, d, tn), lambda i, j: (0, 0, j)),
                  pl.BlockSpec((None, d, tn), lambda i, j: (0, 0, nj + j)),
                  pl.BlockSpec((None, 3, tn), lambda i, j: (layer, 0, j)),
                  pl.BlockSpec((None, 3, tn), lambda i, j: (layer, 0, nj + j))],
        out_specs=pl.BlockSpec((tm, tn), lambda i, j: (i, j)),
        scratch_shapes=[pltpu.VMEM((tm + 2 * HALO, d), BF16)],
        compiler_params=_cp("parallel", "arbitrary"),
        name=name,
    )(h, h, h, gain, shift, scale, w_up, w_up, conv_w, conv_w)


def _head(z_ref, h, gain_ref=None, rope=None, scale=1.0):
    x = z_ref[:, h * HEAD_DIM:(h + 1) * HEAD_DIM].astype(F32)
    if gain_ref is not None:
        ms = jnp.mean(x * x, axis=-1, keepdims=True)
        x = x * lax.rsqrt(ms + EPS) * gain_ref[...]
    if rope is not None:
        x = x * rope[0][...] + pltpu.roll(x, HEAD_DIM // 2, 1) * rope[1][...]
    if scale != 1.0:
        x = x * scale
    return x


def _prep_latent_kernel(nq_ref, nk_ref, zqa_ref, zqb_ref, zk_ref, zv_ref, nqg, nkg, gqg, gkg, cos_ref, sin_ref,
                        nq_o, nk_o, gqT_o, k_o, vT_o, *, qscale):
    rope = (cos_ref, sin_ref)
    for h in range(NA_HEADS):
        sl = slice(h * HEAD_DIM, (h + 1) * HEAD_DIM)
        nq_o[:, sl] = _head(nq_ref, h, nqg, scale=qscale).astype(BF16)
        nk_o[:, sl] = _head(nk_ref, h, nkg).astype(BF16)
    for h in range(GQA_Q_HEADS):
        src = zqa_ref if h < GQA_Q_HEADS // 2 else zqb_ref
        gqT_o[h] = _head(src, h % (GQA_Q_HEADS // 2), gqg, rope, qscale).T.astype(BF16)
    for h in range(GQA_KV_HEADS):
        k_o[h] = _head(zk_ref, h, gkg, rope).astype(BF16)
        vT_o[h] = _head(zv_ref, h).T.astype(BF16)


def prep_latent(zg, gains, rope, qscale):
    m = zg.shape[0]
    tm = 1024
    col = lambda c0, w: pl.BlockSpec((tm, w), lambda i: (i, c0 // w))
    gain = pl.BlockSpec((1, HEAD_DIM), lambda i: (0, 0))
    tab = pl.BlockSpec((tm, HEAD_DIM), lambda i: (i, 0))
    return pl.pallas_call(
        functools.partial(_prep_latent_kernel, qscale=qscale),
        out_shape=(jax.ShapeDtypeStruct((m, NA_W), BF16), jax.ShapeDtypeStruct((m, NA_W), BF16),
                   jax.ShapeDtypeStruct((GQA_Q_HEADS, HEAD_DIM, m), BF16),
                   jax.ShapeDtypeStruct((GQA_KV_HEADS, m, HEAD_DIM), BF16),
                   jax.ShapeDtypeStruct((GQA_KV_HEADS, HEAD_DIM, m), BF16)),
        grid=(m // tm,),
        in_specs=[col(_C_NQ, NA_W), col(_C_NK, NA_W), col(_C_ZQ, GQA_W // 2), col(_C_ZQ + GQA_W // 2, GQA_W // 2),
                  col(_C_ZK, GQA_KV_W), col(_C_ZV, GQA_KV_W), gain, gain, gain, gain, tab, tab],
        out_specs=(pl.BlockSpec((tm, NA_W), lambda i: (i, 0)), pl.BlockSpec((tm, NA_W), lambda i: (i, 0)),
                   pl.BlockSpec((GQA_Q_HEADS, HEAD_DIM, tm), lambda i: (0, 0, i)),
                   pl.BlockSpec((GQA_KV_HEADS, tm, HEAD_DIM), lambda i: (0, i, 0)),
                   pl.BlockSpec((GQA_KV_HEADS, HEAD_DIM, tm), lambda i: (0, 0, i))),
        compiler_params=_cp("parallel"),
        name="prep_latent",
    )(zg, zg, zg, zg, zg, zg, *gains, *rope)


def _prep_ctx_kernel(nq_ref, nk_ref, nv_ref, zqa_ref, zqb_ref, zk_ref, zv_ref, nqg, nkg, gqg, gkg,
                     nk_o, cnqT_o, cnk_o, cnvT_o, cqT_o, ck_o, cvT_o, *, qscale):
    for h in range(NA_HEADS):
        sl = slice(h * HEAD_DIM, (h + 1) * HEAD_DIM)
        kn = _head(nk_ref, h, nkg).astype(BF16)
        nk_o[:, sl] = kn
        cnk_o[h] = kn
        cnqT_o[h] = _head(nq_ref, h, nqg, scale=qscale).T.astype(BF16)
        cnvT_o[h] = _head(nv_ref, h).T.astype(BF16)
    for h in range(GQA_Q_HEADS):
        src = zqa_ref if h < GQA_Q_HEADS // 2 else zqb_ref
        cqT_o[h] = _head(src, h % (GQA_Q_HEADS // 2), gqg, scale=qscale).T.astype(BF16)
    for h in range(GQA_KV_HEADS):
        ck_o[h] = _head(zk_ref, h, gkg).astype(BF16)
        cvT_o[h] = _head(zv_ref, h).T.astype(BF16)


def prep_ctx(czg, gains, qscale):
    m = czg.shape[0]
    col = lambda c0, w: pl.BlockSpec((m, w), lambda i: (0, c0 // w))
    gain = pl.BlockSpec((1, HEAD_DIM), lambda i: (0, 0))
    full = lambda shape: pl.BlockSpec(shape, lambda i: (0,) * len(shape))
    return pl.pallas_call(
        functools.partial(_prep_ctx_kernel, qscale=qscale),
        out_shape=(jax.ShapeDtypeStruct((m, NA_W), BF16),
                   jax.ShapeDtypeStruct((NA_HEADS, HEAD_DIM, m), BF16),
                   jax.ShapeDtypeStruct((NA_HEADS, m, HEAD_DIM), BF16),
                   jax.ShapeDtypeStruct((NA_HEADS, HEAD_DIM, m), BF16),
                   jax.ShapeDtypeStruct((GQA_Q_HEADS, HEAD_DIM, m), BF16),
                   jax.ShapeDtypeStruct((GQA_KV_HEADS, m, HEAD_DIM), BF16),
                   jax.ShapeDtypeStruct((GQA_KV_HEADS, HEAD_DIM, m), BF16)),
        grid=(1,),
        in_specs=[col(_C_NQ, NA_W), col(_C_NK, NA_W), col(_C_NV, NA_W), col(_C_ZQ, GQA_W // 2),
                  col(_C_ZQ + GQA_W // 2, GQA_W // 2), col(_C_ZK, GQA_KV_W), col(_C_ZV, GQA_KV_W),
                  gain, gain, gain, gain],
        out_specs=(full((m, NA_W)),
                   full((NA_HEADS, HEAD_DIM, m)), full((NA_HEADS, m, HEAD_DIM)), full((NA_HEADS, HEAD_DIM, m)),
                   full((GQA_Q_HEADS, HEAD_DIM, m)), full((GQA_KV_HEADS, m, HEAD_DIM)),
                   full((GQA_KV_HEADS, HEAD_DIM, m))),
        compiler_params=_cp("arbitrary"),
        name="prep_ctx",
    )(czg, czg, czg, czg, czg, czg, czg, *gains)


SCORE_BOUND_SAFE = 60.0
FLASH_TQ = 512
FLASH_UNITS = 64


def _flash_kernel(*refs, group, tk, has_extra, n_cast):
    n_in = 5 if has_extra else 3
    qT_ref, k_ref, vT_ref = refs[:3]
    k2_ref, vT2_ref = refs[3:5] if has_extra else (None, None)
    o_ref = refs[n_in + n_cast]
    kmax_ref, acc_ref = refs[-2:]
    tq = min(qT_ref.shape[2], FLASH_TQ)
    nsub = qT_ref.shape[2] // tq
    cols = [(h, sub) for h in range(group) for sub in range(nsub)]
    n_keys = k_ref.shape[1]
    n_full = n_keys // tk
    rem = n_keys - n_full * tk
    unroll = max(1, FLASH_UNITS // len(cols))
    n_body = n_full // unroll

    cast_pairs = list(zip(refs[n_in:n_in + n_cast], refs[n_in + n_cast + 1:n_in + 2 * n_cast + 1]))
    in_loop = [p for p in cast_pairs if n_body > 0 and p[0].shape[0] % (n_body * HALO) == 0]
    for w_ref, wb_ref in cast_pairs:
        if not any(w_ref is q[0] for q in in_loop):
            wb_ref[...] = w_ref[...].astype(BF16)

    def cast_rows(trip):
        for w_ref, wb_ref in in_loop:
            n = w_ref.shape[0] // n_body
            rows = pl.ds(trip * n if isinstance(trip, int) else pl.multiple_of(trip * n, n), n)
            wb_ref[rows, :] = w_ref[rows, :].astype(BF16)
    tails = ([(k_ref, vT_ref, n_full * tk, rem)] if rem > 0 else []) + (
        [(k2_ref, vT2_ref, 0, k2_ref.shape[1])] if has_extra else [])

    @pl.when(pl.program_id(1) == 0)
    def _():
        def ksq(kr, start, size):
            kb = kr[0, pl.ds(start, size), :].astype(F32)
            return jnp.max(jnp.sum(kb * kb, axis=-1, keepdims=True), axis=0, keepdims=True)

        mx = jnp.zeros((1, 1), F32)
        if n_full > 0:
            mx = lax.fori_loop(
                0, n_full, lambda i, c: jnp.maximum(c, ksq(k_ref, pl.multiple_of(i * tk, tk), tk)), mx)
        for kr, _, start, size in tails:
            mx = jnp.maximum(mx, ksq(kr, start, size))
        kmax_ref[...] = mx

    def run_blocks(step, carry):
        if n_full > 0:
            carry = lax.fori_loop(
                0, n_full, lambda kb, c: step(k_ref, vT_ref, pl.multiple_of(kb * tk, tk), tk, c), carry)
        for kr, vr, start, size in tails:
            carry = step(kr, vr, start, size, carry)
        return carry

    qTs = [qT_ref[h, :, sub * tq:(sub + 1) * tq] for h, sub in cols]
    shifts = []
    for qT in qTs:
        qf = qT.astype(F32)
        qsq = jnp.sum(qf * qf, axis=0, keepdims=True)
        shifts.append(jnp.sqrt(qsq * kmax_ref[...]) * 1.01)
    safe = jnp.max(functools.reduce(jnp.maximum, shifts)) <= SCORE_BOUND_SAFE

    @pl.when(safe)
    def _():
        def step(kr, vr, start, size, l8s, nb=1):
            units = [(j, c) for j in range(nb) for c in range(len(cols))]
            kblks = [kr[0, pl.ds(start + j * size, size), :] for j in range(nb)]
            vblks = [vr[0, :, pl.ds(start + j * size, size)] for j in range(nb)]
            l8s = list(l8s)
            s_next = jnp.dot(kblks[0], qTs[0], preferred_element_type=F32)
            for u, (j, c) in enumerate(units):
                s = s_next
                if u + 1 < len(units):
                    jn, cn = units[u + 1]
                    s_next = jnp.dot(kblks[jn], qTs[cn], preferred_element_type=F32)
                p = jnp.exp2(s - shifts[c])
                l8s[c] = l8s[c] + jnp.sum(p.reshape(size // 8, 8, tq), axis=0)
                acc_ref[c] += jnp.dot(vblks[j], p.astype(BF16), preferred_element_type=F32)
            return tuple(l8s)

        acc_ref[...] = jnp.zeros_like(acc_ref)
        l8s = tuple(jnp.zeros((8, tq), F32) for _ in cols)

        def body(kb, c):
            cast_rows(kb)
            return step(k_ref, vT_ref, pl.multiple_of(kb * (tk * unroll), tk * unroll), tk, c, nb=unroll)

        if n_body > 0:
            l8s = lax.fori_loop(0, n_body, body, l8s)
        for kb in range(n_body * unroll, n_full):
            l8s = step(k_ref, vT_ref, kb * tk, tk, l8s)
        for kr, vr, start, size in tails:
            l8s = step(kr, vr, start, size, l8s)
        for c in range(len(cols)):
            l = jnp.sum(l8s[c], axis=0, keepdims=True)
            acc_ref[c] = acc_ref[c] * (1.0 / l)

    @pl.when(jnp.logical_not(safe))
    def _():
        for trip in range(n_body if in_loop else 0):
            cast_rows(trip)
        for c in range(len(cols)):
            def step(kr, vr, start, size, carry, qT=qTs[c]):
                m, l, acc = carry
                kblk = kr[0, pl.ds(start, size), :]
                s = jnp.dot(kblk, qT, preferred_element_type=F32)
                m_new = jnp.maximum(m, jnp.max(s, axis=0, keepdims=True))
                alpha = jnp.exp2(m - m_new)
                p = jnp.exp2(s - m_new)
                l = alpha * l + jnp.sum(p, axis=0, keepdims=True)
                vblk = vr[0, :, pl.ds(start, size)]
                acc = alpha * acc + jnp.dot(vblk, p.astype(BF16), preferred_element_type=F32)
                return m_new, l, acc

            carry = (jnp.full((1, tq), NEG_BIG, F32), jnp.zeros((1, tq), F32),
                     jnp.zeros((HEAD_DIM, tq), F32))
            _, l, acc = run_blocks(step, carry)
            acc_ref[c] = acc * (1.0 / l)

    for c, (h, sub) in enumerate(cols):
        o_ref[sub * tq:(sub + 1) * tq, h * HEAD_DIM:(h + 1) * HEAD_DIM] = acc_ref[c].T.astype(o_ref.dtype)


def flash_attention(qT, k, vT, tq, extra=None, casts=()):
    hq, _, mq = qT.shape
    hkv = k.shape[0]
    group = hq // hkv
    nq = mq // tq
    kv_specs = lambda kk, vv: [
        pl.BlockSpec((1, kk.shape[1], HEAD_DIM), lambda g, i: (g, 0, 0), pipeline_mode=pl.Buffered(1)),
        pl.BlockSpec((1, HEAD_DIM, vv.shape[2]), lambda g, i: (g, 0, 0), pipeline_mode=pl.Buffered(1))]
    args, specs = [k, vT], kv_specs(k, vT)
    if extra is not None:
        args += list(extra)
        specs += kv_specs(*extra)
    out_shapes = [jax.ShapeDtypeStruct((mq, hq * HEAD_DIM), BF16)]
    out_specs = [pl.BlockSpec((tq, group * HEAD_DIM), lambda g, i: (i, g))]
    for w, layer in casts:
        _, kk, nn = w.shape
        rb = kk // (hkv * nq)
        args.append(w)
        specs.append(pl.BlockSpec((None, rb, nn), functools.partial(
            lambda g, i, layer: (layer, g * nq + i, 0), layer=layer)))
        out_shapes.append(jax.ShapeDtypeStruct((1, kk, nn), BF16))
        out_specs.append(pl.BlockSpec((None, rb, nn), lambda g, i: (0, g * nq + i, 0)))
    outs = pl.pallas_call(
        functools.partial(_flash_kernel, group=group, tk=512, has_extra=extra is not None, n_cast=len(casts)),
        out_shape=out_shapes,
        grid=(hkv, nq),
        in_specs=[pl.BlockSpec((group, HEAD_DIM, tq), lambda g, i: (g, 0, i))] + specs,
        out_specs=out_specs,
        scratch_shapes=[pltpu.VMEM((1, 1), F32),
                        pltpu.VMEM((group * max(1, tq // FLASH_TQ), HEAD_DIM, min(tq, FLASH_TQ)), F32)],
        compiler_params=_cp("parallel", "arbitrary"),
        name="flash_attention",
    )(qT, *args)
    return outs[0], list(outs[1:])


NA_RB = 8
NA_TOK = NA_RB * GRID_W
NA_PAIR = 2 * GRID_W
NA_SLAB_ROWS = NA_ROWS + 2
NA_SLAB = NA_SLAB_ROWS * GRID_W
NA_PATTERNS = 5


def _na_kernel(q_ref, kp, kc, kn, vp, vc, vn, ck_ref, cv_ref, bias_ref, o_ref, kbuf, vbuf):
    b = pl.program_id(0)
    rows = pl.num_programs(0) * NA_RB
    kbuf[0:NA_TOK, :] = kp[...]
    kbuf[NA_TOK:2 * NA_TOK, :] = kc[...]
    kbuf[2 * NA_TOK:3 * NA_TOK, :] = kn[...]
    vbuf[0:NA_TOK, :] = vp[...]
    vbuf[NA_TOK:2 * NA_TOK, :] = vc[...]
    vbuf[2 * NA_TOK:3 * NA_TOK, :] = vn[...]
    nt = (((1,), (1,)), ((), ()))

    def slab(pair):
        r = b * NA_RB + 2 * pair
        u0 = jnp.clip(r - NA_ROWS // 2, 0, rows - NA_ROWS)
        off = pl.multiple_of((u0 - b * NA_RB + NA_RB) * GRID_W, NA_PAIR)
        pat = jnp.where(r < NA_ROWS // 2, r // 2,
                        jnp.where(r >= rows - NA_ROWS // 2, 3 + (r - (rows - NA_ROWS // 2)) // 2, 2))
        return off, pat

    slabs = [slab(pair) for pair in range(NA_RB // 2)]
    units = [(pair, h) for pair in range(NA_RB // 2) for h in range(NA_HEADS)]

    def scores(pair, h):
        off, pat = slabs[pair]
        sl = slice(h * HEAD_DIM, (h + 1) * HEAD_DIM)
        qh = q_ref[pair * NA_PAIR:(pair + 1) * NA_PAIR, sl]
        s = lax.dot_general(qh, kbuf[pl.ds(off, NA_SLAB), sl], nt, preferred_element_type=F32)
        sc = lax.dot_general(qh, ck_ref[:, sl], nt, preferred_element_type=F32)
        return s + bias_ref[pat, h], sc

    nxt = scores(*units[0])
    for u, (pair, h) in enumerate(units):
        s, sc = nxt
        if u + 1 < len(units):
            nxt = scores(*units[u + 1])
        off, _ = slabs[pair]
        sl = slice(h * HEAD_DIM, (h + 1) * HEAD_DIM)
        m = jnp.maximum(jnp.max(s, axis=-1, keepdims=True), jnp.max(sc, axis=-1, keepdims=True))
        p = jnp.exp2(s - m)
        pc = jnp.exp2(sc - m)
        l = jnp.sum(p, axis=-1, keepdims=True) + jnp.sum(pc, axis=-1, keepdims=True)
        o = (jnp.dot(p.astype(BF16), vbuf[pl.ds(off, NA_SLAB), sl], preferred_element_type=F32)
             + jnp.dot(pc.astype(BF16), cv_ref[:, sl], preferred_element_type=F32))
        o_ref[pair * NA_PAIR:(pair + 1) * NA_PAIR, sl] = (o * (1.0 / l)).astype(o_ref.dtype)


def neighborhood_attention(qn, kn, zg, ckn, czg, bias):
    m = qn.shape[0]
    nb = m // NA_TOK
    vcol = _C_NV // NA_W
    blk = (NA_TOK, NA_W)
    return pl.pallas_call(
        _na_kernel,
        out_shape=jax.ShapeDtypeStruct((m, NA_W), BF16),
        grid=(nb,),
        in_specs=[pl.BlockSpec(blk, lambda b: (b, 0)),
                  pl.BlockSpec(blk, lambda b: (jnp.maximum(b - 1, 0), 0)),
                  pl.BlockSpec(blk, lambda b: (b, 0)),
                  pl.BlockSpec(blk, lambda b: (jnp.minimum(b + 1, nb - 1), 0)),
                  pl.BlockSpec(blk, lambda b: (jnp.maximum(b - 1, 0), vcol)),
                  pl.BlockSpec(blk, lambda b: (b, vcol)),
                  pl.BlockSpec(blk, lambda b: (jnp.minimum(b + 1, nb - 1), vcol)),
                  pl.BlockSpec((CTX_LEN, NA_W), lambda b: (0, 0)),
                  pl.BlockSpec((CTX_LEN, NA_W), lambda b: (0, vcol)),
                  pl.BlockSpec((NA_PATTERNS, NA_HEADS, NA_PAIR, NA_SLAB), lambda b: (0, 0, 0, 0))],
        out_specs=pl.BlockSpec(blk, lambda b: (b, 0)),
        scratch_shapes=[pltpu.VMEM((3 * NA_TOK, NA_W), BF16), pltpu.VMEM((3 * NA_TOK, NA_W), BF16)],
        compiler_params=_cp("arbitrary"),
        name="neighborhood_attention",
    )(qn, kn, kn, kn, zg, zg, zg, ckn, czg, bias)


def na_bias_table(rpb):
    col = np.arange(GRID_W)
    c0 = np.clip(col - NA_COLS // 2, 0, GRID_W - NA_COLS)
    kc = np.arange(GRID_W)
    inside = (kc[None, :] >= c0[:, None]) & (kc[None, :] < c0[:, None] + NA_COLS)
    dc = kc[None, :] - col[:, None] + NA_COLS - 1
    onehot = (dc[None] == np.arange(2 * NA_COLS - 1)[:, None, None]) & inside[None]
    t = jnp.einsum('hrd,dck->hrck', rpb.astype(F32) * LOG2E, jnp.asarray(onehot, F32),
                   precision=lax.Precision.HIGHEST)
    t = jnp.where(inside[None, None], t, NEG_BIG)
    masked = jnp.full((NA_HEADS, GRID_W, GRID_W), NEG_BIG, F32)
    patterns = [(0, 0), (2, 0), (4, 1), (4, 0), (6, 0)]
    tabs = []
    for a, e in patterns:
        per_q = []
        for start, dr0 in ((0, NA_ROWS - 1 - a), (e, NA_ROWS - 2 - a)):
            blocks = [t[:, w + dr0] if start <= w < start + NA_ROWS else masked for w in range(NA_SLAB_ROWS)]
            per_q.append(jnp.stack(blocks, axis=2).reshape(NA_HEADS, GRID_W, NA_SLAB))
        tabs.append(jnp.concatenate(per_q, axis=1))
    return jnp.stack(tabs, axis=0)


def _dft_mats(n):
    a = 2.0 * np.pi * np.outer(np.arange(n), np.arange(n)) / n
    return np.cos(a), np.sin(a)


def _chan_dft_kernel(z_ref, w_ref, o_ref):
    for g in range(FOURIER_GROUPS):
        x = z_ref[:, g * HEAD_DIM:(g + 1) * HEAD_DIM]
        r = jnp.dot(x, w_ref[...], preferred_element_type=F32)
        o_ref[0, :, g * HEAD_DIM:(g + 1) * HEAD_DIM] = r[:, :HEAD_DIM].astype(o_ref.dtype)
        o_ref[1, :, g * HEAD_DIM:(g + 1) * HEAD_DIM] = r[:, HEAD_DIM:].astype(o_ref.dtype)


def chan_dft(zg, wch):
    m = zg.shape[0]
    tm = min(m, 1024)
    return pl.pallas_call(
        _chan_dft_kernel,
        out_shape=jax.ShapeDtypeStruct((2, m, FOURIER_W), BF16),
        grid=(m // tm,),
        in_specs=[pl.BlockSpec((tm, FOURIER_W), lambda i: (i, _C_ZF // FOURIER_W)),
                  pl.BlockSpec((HEAD_DIM, 2 * HEAD_DIM), lambda i: (0, 0))],
        out_specs=pl.BlockSpec((2, tm, FOURIER_W), lambda i: (0, i, 0)),
        compiler_params=_cp("parallel"),
        name="chan_dft",
    )(zg, wch)


def _left_mm_kernel(l_ref, x_ref, o_ref, *, n_out):
    x = jnp.concatenate([x_ref[0], x_ref[1]], axis=0)
    r = jnp.dot(l_ref[...], x, preferred_element_type=F32)
    if n_out == 1:
        o_ref[...] = r.astype(o_ref.dtype)
    else:
        half = r.shape[0] // 2
        o_ref[0] = r[:half].astype(o_ref.dtype)
        o_ref[1] = r[half:].astype(o_ref.dtype)


def dft_stage1(zc, lmat):
    _, r, n = zc.shape
    tn = 4096
    return pl.pallas_call(
        functools.partial(_left_mm_kernel, n_out=2),
        out_shape=jax.ShapeDtypeStruct((2, r, n), BF16),
        grid=(n // tn,),
        in_specs=[pl.BlockSpec((2 * r, 2 * r), lambda j: (0, 0)),
                  pl.BlockSpec((2, r, tn), lambda j: (0, 0, j))],
        out_specs=pl.BlockSpec((2, r, tn), lambda j: (0, 0, j)),
        compiler_params=_cp("parallel"),
        name="dft_stage1",
    )(lmat, zc)


def _dft_stage3_kernel(l_ref, x_ref, o_ref, *, kb):
    for i in range(kb):
        x = jnp.concatenate([x_ref[0, i], x_ref[1, i]], axis=0)
        o_ref[i] = jnp.dot(l_ref[i], x, preferred_element_type=F32).astype(o_ref.dtype)


def dft_stage3(a, tables):
    _, n1, n2, c = a.shape
    kb = 8
    return pl.pallas_call(
        functools.partial(_dft_stage3_kernel, kb=kb),
        out_shape=jax.ShapeDtypeStruct((n1, n2, c), BF16),
        grid=(n1 // kb,),
        in_specs=[pl.BlockSpec((kb, n2, 2 * n2), lambda j: (j, 0, 0)),
                  pl.BlockSpec((2, kb, n2, c), lambda j: (0, j, 0, 0))],
        out_specs=pl.BlockSpec((kb, n2, c), lambda j: (j, 0, 0)),
        compiler_params=_cp("parallel"),
        name="dft_stage3",
    )(tables, a)


def dft_direct(zc, lmat):
    _, t, c = zc.shape
    return pl.pallas_call(
        functools.partial(_left_mm_kernel, n_out=1),
        out_shape=jax.ShapeDtypeStruct((t, c), BF16),
        grid=(1,),
        in_specs=[pl.BlockSpec((t, 2 * t), lambda j: (0, 0)),
                  pl.BlockSpec((2, t, c), lambda j: (0, 0, 0))],
        out_specs=pl.BlockSpec((t, c), lambda j: (0, 0)),
        compiler_params=_cp("arbitrary"),
        name="dft_direct",
    )(lmat, zc)


def _fourier_tables():
    c128, s128 = _dft_mats(HEAD_DIM)
    wch = np.concatenate([c128, -s128], axis=1)
    n1 = SEQ // HEAD_DIM
    l1 = np.block([[c128, s128], [-s128, c128]])
    k1 = np.arange(n1)[:, None, None]
    k2 = np.arange(n1)[None, :, None]
    t2 = np.arange(n1)[None, None, :]
    ang = 2.0 * np.pi * ((n1 * k2 + k1) * t2 % SEQ) / SEQ
    norm = 1.0 / math.sqrt(SEQ * HEAD_DIM)
    l3 = np.concatenate([np.cos(ang), np.sin(ang)], axis=2) * norm
    cc, sc = _dft_mats(CTX_LEN)
    lc = np.concatenate([cc, sc], axis=1) / math.sqrt(CTX_LEN * HEAD_DIM)
    as_bf = lambda a: jnp.asarray(a, F32).astype(BF16)
    return as_bf(wch), as_bf(l1), as_bf(l3), as_bf(lc)


def fourier_latent(zg, tabs):
    wch, l1, l3, _ = tabs
    n1 = SEQ // HEAD_DIM
    zc = chan_dft(zg, wch)
    a = dft_stage1(zc.reshape(2, n1, n1 * FOURIER_W), l1)
    o3 = dft_stage3(a.reshape(2, n1, n1, FOURIER_W), l3)
    return o3.transpose(1, 0, 2).reshape(SEQ, FOURIER_W)


def fourier_ctx(czg, tabs):
    wch, _, _, lc = tabs
    return dft_direct(chan_dft(czg, wch), lc)


def _rope_tables(n_tok):
    t = jnp.arange(n_tok)
    row = (t // GRID_W).astype(F32)
    col = (t % GRID_W).astype(F32)
    n_freq = HEAD_DIM // 4
    inv_freq = ROPE_THETA ** (-jnp.arange(n_freq, dtype=F32) / n_freq)
    ang = jnp.concatenate([row[:, None] * inv_freq, col[:, None] * inv_freq], axis=-1)
    cos, sin = jnp.cos(ang), jnp.sin(ang)
    return jnp.concatenate([cos, cos], axis=-1), jnp.concatenate([-sin, sin], axis=-1)


def kernel(x, c, ctx, c_ctx, w_ada, b_ada, norm1, w_in, conv_w, na_q_gain, na_k_gain, na_rpb, gqa_q_gain, gqa_k_gain, w_conv_out, w_fourier_out, w_na_out, w_gqa_out, w_gate, b_gate, w_o, norm2, w_up, ffn_conv_w, w_down):
    d = D_MODEL
    h = x[0]
    hc = ctx[0]
    rope = _rope_tables(SEQ)
    ftabs = _fourier_tables()
    qscale2 = HEAD_DIM ** -0.5 * LOG2E
    cvec = jnp.zeros((8, d), F32).at[0].set(c[0]).at[1].set(c_ctx)
    row = lambda v: v.reshape(1, -1)

    w_in_b, w_gate_b = w_in[0:1].astype(BF16), w_gate[0:1].astype(BF16)
    late_w = (w_conv_out, w_fourier_out, w_na_out, w_gqa_out, w_o, w_up, w_down)
    b_ada3 = b_ada.reshape(DEPTH, 1, -1)
    b_gate3 = b_gate.reshape(DEPTH, 1, -1)

    for i in range(DEPTH):
        last = i == DEPTH - 1
        bias_tab = na_bias_table(na_rpb[i])
        gains = (row(na_q_gain[i]), row(na_k_gain[i]), row(gqa_q_gain[i]), row(gqa_k_gain[i]))

        mods = ada_mod(cvec, w_ada, b_ada3, i)
        lat = [mods[0:1, k * d:(k + 1) * d] for k in range(6)]
        cm = [mods[1:2, k * d:(k + 1) * d] for k in range(6)]

        zg = in_proj(h, row(norm1[i]), lat[0], lat[1], w_in_b, w_gate_b, b_gate3, i, "in_proj")
        czg = in_proj(hc, row(norm1[i]), cm[0], cm[1], w_in_b, w_gate_b, b_gate3, i, "in_proj_ctx")

        cnk_rows, cnqT, cnk, cnvT, cqT, ck, cvT = prep_ctx(czg, gains, qscale2)
        nq, nk, gqT, gk, gvT = prep_latent(zg, gains, rope, qscale2)

        y_conv = conv_mixer(zg, conv_w, i)
        y_four = fourier_latent(zg, ftabs)
        y_na = neighborhood_attention(nq, nk, zg, cnk_rows, czg, bias_tab)
        casts = [(w, i) for w in late_w] + ([] if last else [(w_in, i + 1), (w_gate, i + 1)])
        y_gqa, wb = flash_attention(gqT, gk, gvT, tq=1024, extra=(ck, cvT), casts=casts)
        merge_w, w_o_b, w_up_b, w_down_b = wb[0:4], wb[4], wb[5], wb[6]

        merged = merge_branches([y_conv, y_four, y_na, y_gqa], merge_w, zg)
        h = mm_residual(merged, w_o_b, h, lat[2], "out_proj", in_place=i > 0)
        act = ffn_up_act(h, row(norm2[i]), lat[3], lat[4], w_up_b, ffn_conv_w, i, "ffn_up_act")
        h = mm_residual(act, w_down_b, h, lat[5], "ffn_down", in_place=True)

        if not last:
            cy_conv = conv_mixer(czg, conv_w, i)
            cy_four = fourier_ctx(czg, ftabs)
            cy_na, _ = flash_attention(cnqT, cnk, cnvT, tq=CTX_LEN)
            cy_gqa, _ = flash_attention(cqT, ck, cvT, tq=CTX_LEN)
            cmerged = merge_branches([cy_conv, cy_four, cy_na, cy_gqa], merge_w, czg)
            hc = mm_residual(cmerged, w_o_b, hc, cm[2], "out_proj_ctx", in_place=i > 0)
            cact = ffn_up_act(hc, row(norm2[i]), cm[3], cm[4], w_up_b, ffn_conv_w, i, "ffn_up_act_ctx")
            hc = mm_residual(cact, w_down_b, hc, cm[5], "ffn_down_ctx", in_place=True)
            w_in_b, w_gate_b = wb[7], wb[8]

    return h[None]
```

```python
import functools
import math

import jax
import jax.numpy as jnp
import numpy as np
from jax import lax
from jax.experimental import pallas as pl
from jax.experimental.pallas import tpu as pltpu

D_MODEL = 2048
SEQ = 16384
DEPTH = 4
GRID_W = 64
CTX_LEN = 256
HEAD_DIM = 128
CONV_W = 512
FOURIER_GROUPS = 4
FOURIER_W = 512
NA_HEADS = 4
NA_W = 512
NA_ROWS = 8
NA_COLS = 16
GQA_Q_HEADS = 8
GQA_KV_HEADS = 2
GQA_GROUP = 4
GQA_W = 1024
GQA_KV_W = 256
N_BRANCH = 4
N_IN = 5120
D_FF = 5632
ROPE_THETA = 10000.0
EPS = 1e-6

_C_XA, _C_BG, _C_CG = 0, 512, 1024
_C_ZF = 1536
_C_NQ, _C_NK, _C_NV = 2048, 2560, 3072
_C_ZQ, _C_ZK, _C_ZV = 3584, 4608, 4864
_C_GATE = N_IN

BF16 = jnp.bfloat16
F32 = jnp.float32
V7X_VMEM_LIMIT = 52 * 1024 * 1024
NEG_BIG = -1e30
HALO = 16
NORM_CHUNK = 256
FFN_CHUNK = 128
LOG2E = 1.4426950408889634


def _cp(*sem):
    return pltpu.CompilerParams(dimension_semantics=sem, vmem_limit_bytes=V7X_VMEM_LIMIT)


def _sigmoid(t):
    return 0.5 * jnp.tanh(0.5 * t) + 0.5


def _norm_mod(x, g, sh, sc):
    ms = jnp.mean(x * x, axis=-1, keepdims=True)
    return (x * lax.rsqrt(ms + EPS) * g) * (1.0 + sc) + sh


def _ada_kernel(c_ref, w_ref, b_ref, o_ref):
    c = c_ref[...]
    s = c * _sigmoid(c)
    o_ref[...] = jnp.dot(s.astype(BF16), w_ref[...].astype(BF16),
                         preferred_element_type=F32) + b_ref[...]


def ada_mod(cvec, w, b, layer):
    n = w.shape[2]
    tn = 1024
    return pl.pallas_call(
        _ada_kernel,
        out_shape=jax.ShapeDtypeStruct((8, n), F32),
        grid=(n // tn,),
        in_specs=[pl.BlockSpec((8, D_MODEL), lambda j: (0, 0)),
                  pl.BlockSpec((None, D_MODEL, tn), lambda j: (layer, 0, j)),
                  pl.BlockSpec((None, 1, tn), lambda j: (layer, 0, j))],
        out_specs=pl.BlockSpec((8, tn), lambda j: (0, j)),
        compiler_params=_cp("parallel"),
        name="ada_mod",
    )(cvec, w, b)


def _in_proj_kernel(h_ref, g_ref, sh_ref, sc_ref, win_ref, wgate_ref, b_ref, o_ref, xn_ref, *, n_plain):
    j = pl.program_id(1)
    tm = h_ref.shape[0]
    cm = min(tm, NORM_CHUNK)

    @pl.when(j == 0)
    def _():
        for c in range(tm // cm):
            rows = slice(c * cm, (c + 1) * cm)
            xc = _norm_mod(h_ref[rows, :], g_ref[...], sh_ref[...], sc_ref[...]).astype(BF16)
            xn_ref[rows, :] = xc
            o_ref[rows, :] = jnp.dot(xc, win_ref[...], preferred_element_type=F32).astype(o_ref.dtype)

    @pl.when(jnp.logical_and(j > 0, j < n_plain))
    def _():
        o_ref[...] = jnp.dot(xn_ref[...], win_ref[...], preferred_element_type=F32).astype(o_ref.dtype)

    @pl.when(j >= n_plain)
    def _():
        t = jnp.dot(xn_ref[...], wgate_ref[...], preferred_element_type=F32) + b_ref[...]
        o_ref[...] = _sigmoid(t).astype(o_ref.dtype)


def in_proj(h, gain, shift, scale, w_in, w_gate, b_gate, layer, name):
    m, d = h.shape
    n_in, n_gate = w_in.shape[2], w_gate.shape[2]
    tm = min(m, 1024)
    tn = 1024
    n_plain = n_in // tn
    vec = pl.BlockSpec((1, d), lambda i, j: (0, 0))
    return pl.pallas_call(
        functools.partial(_in_proj_kernel, n_plain=n_plain),
        out_shape=jax.ShapeDtypeStruct((m, n_in + n_gate), BF16),
        grid=(m // tm, (n_in + n_gate) // tn),
        in_specs=[pl.BlockSpec((tm, d), lambda i, j: (i, 0)), vec, vec, vec,
                  pl.BlockSpec((None, d, tn), lambda i, j: (0, 0, jnp.minimum(j, n_plain - 1))),
                  pl.BlockSpec((None, d, tn), lambda i, j: (0, 0, jnp.maximum(j - n_plain, 0))),
                  pl.BlockSpec((None, 1, tn), lambda i, j: (layer, 0, jnp.maximum(j - n_plain, 0)))],
        out_specs=pl.BlockSpec((tm, tn), lambda i, j: (i, j)),
        scratch_shapes=[pltpu.VMEM((tm, d), BF16)],
        compiler_params=_cp("parallel", "arbitrary"),
        name=name,
    )(h, gain, shift, scale, w_in, w_gate, b_gate)


def _mm_res_kernel(x_ref, w_ref, h_ref, g_ref, o_ref):
    acc = jnp.dot(x_ref[...], w_ref[...], preferred_element_type=F32)
    o_ref[...] = h_ref[...] + g_ref[...] * acc


def mm_residual(x, w, h, gate, name, in_place):
    m, k = x.shape
    n = w.shape[2]
    tm = min(m, 1024)
    tn = 1024 if k <= D_MODEL else 512
    return pl.pallas_call(
        _mm_res_kernel,
        out_shape=jax.ShapeDtypeStruct((m, n), F32),
        grid=(m // tm, n // tn),
        in_specs=[pl.BlockSpec((tm, k), lambda i, j: (i, 0)),
                  pl.BlockSpec((None, k, tn), lambda i, j: (0, 0, j)),
                  pl.BlockSpec((tm, tn), lambda i, j: (i, j)),
                  pl.BlockSpec((1, tn), lambda i, j: (0, j))],
        out_specs=pl.BlockSpec((tm, tn), lambda i, j: (i, j)),
        input_output_aliases={2: 0} if in_place else {},
        compiler_params=_cp("parallel", "arbitrary"),
        name=name,
    )(x, w, h, gate)


def _merge_kernel(yc, yf, yn, yg, wc, wf, wn, wg, g0, g1, g2, g3, o_ref):
    def term(y, w, g):
        return g[...].astype(F32) * jnp.dot(y[...], w[...], preferred_element_type=F32)

    o_ref[...] = (term(yc, wc, g0) + term(yf, wf, g1) + term(yn, wn, g2)
                  + term(yg, wg, g3)).astype(o_ref.dtype)


def merge_branches(ys, ws, zg):
    m = zg.shape[0]
    tm = min(m, 1024)
    tn = 1024
    y_specs = [pl.BlockSpec((tm, y.shape[1]), lambda i, j: (i, 0)) for y in ys]
    w_specs = [pl.BlockSpec((None, w.shape[1], tn), lambda i, j: (0, 0, j)) for w in ws]
    g_specs = [pl.BlockSpec((tm, tn), functools.partial(
        lambda i, j, b: (i, (_C_GATE + b * D_MODEL) // tn + j), b=b)) for b in range(N_BRANCH)]
    return pl.pallas_call(
        _merge_kernel,
        out_shape=jax.ShapeDtypeStruct((m, D_MODEL), BF16),
        grid=(m // tm, D_MODEL // tn),
        in_specs=y_specs + w_specs + g_specs,
        out_specs=pl.BlockSpec((tm, tn), lambda i, j: (i, j)),
        compiler_params=_cp("parallel", "arbitrary"),
        name="merge",
    )(*ys, *ws, zg, zg, zg, zg)


def _conv3(p, prev_row, next_row, w_ref):
    tm = p.shape[0]
    row = lax.broadcasted_iota(jnp.int32, p.shape, 0)
    up = jnp.where(row == 0, prev_row, pltpu.roll(p, 1, 0))
    dn = jnp.where(row == tm - 1, next_row, pltpu.roll(p, tm - 1, 0))
    return up * w_ref[0:1, :] + p * w_ref[1:2, :] + dn * w_ref[2:3, :]


def _halo_specs(tm, tc, m, col):
    nb = m // HALO
    per = tm // HALO
    main = pl.BlockSpec((tm, tc), lambda i: (i, col))
    prev = pl.BlockSpec((HALO, tc), lambda i: (jnp.maximum(i * per - 1, 0), col))
    nxt = pl.BlockSpec((HALO, tc), lambda i: (jnp.minimum((i + 1) * per, nb - 1), col))
    return main, prev, nxt


def _convmix_kernel(xa, xap, xan, cg, cgp, cgn, bg, w_ref, o_ref):
    i = pl.program_id(0)
    last = pl.num_programs(0) - 1
    p = cg[...].astype(F32) * xa[...].astype(F32)
    pp = cgp[...].astype(F32)[HALO - 1:HALO, :] * xap[...].astype(F32)[HALO - 1:HALO, :]
    pn = cgn[...].astype(F32)[0:1, :] * xan[...].astype(F32)[0:1, :]
    pp = jnp.where(i > 0, pp, 0.0)
    pn = jnp.where(i < last, pn, 0.0)
    o_ref[...] = (bg[...].astype(F32) * _conv3(p, pp, pn, w_ref)).astype(o_ref.dtype)


def conv_mixer(zg, w, layer):
    m = zg.shape[0]
    tm = min(m, 1024)
    tc = CONV_W
    xa = _halo_specs(tm, tc, m, _C_XA // tc)
    cg = _halo_specs(tm, tc, m, _C_CG // tc)
    bg = pl.BlockSpec((tm, tc), lambda i: (i, _C_BG // tc))
    return pl.pallas_call(
        _convmix_kernel,
        out_shape=jax.ShapeDtypeStruct((m, CONV_W), BF16),
        grid=(m // tm,),
        in_specs=[*xa, *cg, bg, pl.BlockSpec((None, 3, tc), lambda i: (layer, 0, 0))],
        out_specs=pl.BlockSpec((tm, tc), lambda i: (i, 0)),
        compiler_params=_cp("parallel"),
        name="conv_mixer",
    )(zg, zg, zg, zg, zg, zg, zg, w)


def _ffn_up_act_kernel(h_ref, hp_ref, hn_ref, g_ref, sh_ref, sc_ref, wa_ref, wg_ref, cwa_ref, cwg_ref,
                       o_ref, xn_ref):
    i = pl.program_id(0)
    j = pl.program_id(1)
    last = pl.num_programs(0) - 1
    tm = h_ref.shape[0]
    rows = tm + 2 * HALO

    cm = min(tm, NORM_CHUNK)
    nch = tm // cm

    def finish(ua, ug):
        def conv(u, cw_ref):
            c = (pltpu.roll(u, 1, 0) * cw_ref[0:1, :] + u * cw_ref[1:2, :]
                 + pltpu.roll(u, rows - 1, 0) * cw_ref[2:3, :])
            return c[HALO:HALO + tm]

        ca = conv(ua, cwa_ref)
        cgt = conv(ug, cwg_ref)
        o_ref[...] = (ca * _sigmoid(ca) * cgt).astype(o_ref.dtype)

    up = lambda x, w_ref: jnp.dot(x, w_ref[...], preferred_element_type=F32)

    @pl.when(j == 0)
    def _():
        nm = lambda x: _norm_mod(x, g_ref[...], sh_ref[...], sc_ref[...])
        xn_ref[0:HALO, :] = jnp.where(i > 0, nm(hp_ref[...]), 0.0).astype(BF16)
        uas, ugs = [], []
        for c in range(nch):
            lo, hi = c * cm, (c + 1) * cm
            xn_ref[HALO + lo:HALO + hi, :] = nm(h_ref[lo:hi, :]).astype(BF16)
            if c == nch - 1:
                xn_ref[HALO + tm:rows, :] = jnp.where(i < last, nm(hn_ref[...]), 0.0).astype(BF16)
            r0 = 0 if c == 0 else HALO + lo
            r1 = rows if c == nch - 1 else HALO + hi
            xc = xn_ref[r0:r1, :]
            uas.append(up(xc, wa_ref))
            ugs.append(up(xc, wg_ref))
        cat = lambda parts: parts[0] if len(parts) == 1 else jnp.concatenate(parts, axis=0)
        finish(cat(uas), cat(ugs))

    @pl.when(j > 0)
    def _():
        uas, ugs = [], []
        cm2 = min(tm, FFN_CHUNK)
        for c in range(tm // cm2):
            r0 = 0 if c == 0 else HALO + c * cm2
            r1 = rows if c == tm // cm2 - 1 else HALO + (c + 1) * cm2
            xc = xn_ref[r0:r1, :]
            uas.append(up(xc, wa_ref))
            ugs.append(up(xc, wg_ref))
        cat = lambda parts: parts[0] if len(parts) == 1 else jnp.concatenate(parts, axis=0)
        finish(cat(uas), cat(ugs))


def ffn_up_act(h, gain, shift, scale, w_up, conv_w, layer, name):
    m, d = h.shape
    tm = min(m, 1024)
    tn = 512
    nj = D_FF // tn
    nb = m // HALO
    per = tm // HALO
    vec = pl.BlockSpec((1, d), lambda i, j: (0, 0))
    return pl.pallas_call(
        _ffn_up_act_kernel,
        out_shape=jax.ShapeDtypeStruct((m, D_FF), BF16),
        grid=(m // tm, nj),
        in_specs=[pl.BlockSpec((tm, d), lambda i, j: (i, 0)),
                  pl.BlockSpec((HALO, d), lambda i, j: (jnp.maximum(i * per - 1, 0), 0)),
                  pl.BlockSpec((HALO, d), lambda i, j: (jnp.minimum((i + 1) * per, nb - 1), 0)),
                  vec, vec, vec,
                  pl.BlockSpec((None, d, tn), lambda i, j: (0, 0, j)),
                  pl.BlockSpec((None, d, tn), lambda i, j: (0, 0, nj + j)),
                  pl.BlockSpec((None, 3, tn), lambda i, j: (layer, 0, j)),
                  pl.BlockSpec((None, 3, tn), lambda i, j: (layer, 0, nj + j))],
        out_specs=pl.BlockSpec((tm, tn), lambda i, j: (i, j)),
        scratch_shapes=[pltpu.VMEM((tm + 2 * HALO, d), BF16)],
        compiler_params=_cp("parallel", "arbitrary"),
        name=name,
    )(h, h, h, gain, shift, scale, w_up, w_up, conv_w, conv_w)


def _head(z_ref, h, gain_ref=None, rope=None, scale=1.0):
    x = z_ref[:, h * HEAD_DIM:(h + 1) * HEAD_DIM].astype(F32)
    if gain_ref is not None:
        ms = jnp.mean(x * x, axis=-1, keepdims=True)
        x = x * lax.rsqrt(ms + EPS) * gain_ref[...]
    if rope is not None:
        x = x * rope[0][...] + pltpu.roll(x, HEAD_DIM // 2, 1) * rope[1][...]
    if scale != 1.0:
        x = x * scale
    return x


def _prep_latent_kernel(nq_ref, nk_ref, zqa_ref, zqb_ref, zk_ref, zv_ref, nqg, nkg, gqg, gkg, cos_ref, sin_ref,
                        nq_o, nk_o, gqT_o, k_o, vT_o, *, qscale):
    rope = (cos_ref, sin_ref)
    for h in range(NA_HEADS):
        sl = slice(h * HEAD_DIM, (h + 1) * HEAD_DIM)
        nq_o[:, sl] = _head(nq_ref, h, nqg, scale=qscale).astype(BF16)
        nk_o[:, sl] = _head(nk_ref, h, nkg).astype(BF16)
    for h in range(GQA_Q_HEADS):
        src = zqa_ref if h < GQA_Q_HEADS // 2 else zqb_ref
        gqT_o[h] = _head(src, h % (GQA_Q_HEADS // 2), gqg, rope, qscale).T.astype(BF16)
    for h in range(GQA_KV_HEADS):
        k_o[h] = _head(zk_ref, h, gkg, rope).astype(BF16)
        vT_o[h] = _head(zv_ref, h).T.astype(BF16)


def prep_latent(zg, gains, rope, qscale):
    m = zg.shape[0]
    tm = 1024
    col = lambda c0, w: pl.BlockSpec((tm, w), lambda i: (i, c0 // w))
    gain = pl.BlockSpec((1, HEAD_DIM), lambda i: (0, 0))
    tab = pl.BlockSpec((tm, HEAD_DIM), lambda i: (i, 0))
    return pl.pallas_call(
        functools.partial(_prep_latent_kernel, qscale=qscale),
        out_shape=(jax.ShapeDtypeStruct((m, NA_W), BF16), jax.ShapeDtypeStruct((m, NA_W), BF16),
                   jax.ShapeDtypeStruct((GQA_Q_HEADS, HEAD_DIM, m), BF16),
                   jax.ShapeDtypeStruct((GQA_KV_HEADS, m, HEAD_DIM), BF16),
                   jax.ShapeDtypeStruct((GQA_KV_HEADS, HEAD_DIM, m), BF16)),
        grid=(m // tm,),
        in_specs=[col(_C_NQ, NA_W), col(_C_NK, NA_W), col(_C_ZQ, GQA_W // 2), col(_C_ZQ + GQA_W // 2, GQA_W // 2),
                  col(_C_ZK, GQA_KV_W), col(_C_ZV, GQA_KV_W), gain, gain, gain, gain, tab, tab],
        out_specs=(pl.BlockSpec((tm, NA_W), lambda i: (i, 0)), pl.BlockSpec((tm, NA_W), lambda i: (i, 0)),
                   pl.BlockSpec((GQA_Q_HEADS, HEAD_DIM, tm), lambda i: (0, 0, i)),
                   pl.BlockSpec((GQA_KV_HEADS, tm, HEAD_DIM), lambda i: (0, i, 0)),
                   pl.BlockSpec((GQA_KV_HEADS, HEAD_DIM, tm), lambda i: (0, 0, i))),
        compiler_params=_cp("parallel"),
        name="prep_latent",
    )(zg, zg, zg, zg, zg, zg, *gains, *rope)


def _prep_ctx_kernel(nq_ref, nk_ref, nv_ref, zqa_ref, zqb_ref, zk_ref, zv_ref, nqg, nkg, gqg, gkg,
                     nk_o, cnqT_o, cnk_o, cnvT_o, cqT_o, ck_o, cvT_o, *, qscale):
    for h in range(NA_HEADS):
        sl = slice(h * HEAD_DIM, (h + 1) * HEAD_DIM)
        kn = _head(nk_ref, h, nkg).astype(BF16)
        nk_o[:, sl] = kn
        cnk_o[h] = kn
        cnqT_o[h] = _head(nq_ref, h, nqg, scale=qscale).T.astype(BF16)
        cnvT_o[h] = _head(nv_ref, h).T.astype(BF16)
    for h in range(GQA_Q_HEADS):
        src = zqa_ref if h < GQA_Q_HEADS // 2 else zqb_ref
        cqT_o[h] = _head(src, h % (GQA_Q_HEADS // 2), gqg, scale=qscale).T.astype(BF16)
    for h in range(GQA_KV_HEADS):
        ck_o[h] = _head(zk_ref, h, gkg).astype(BF16)
        cvT_o[h] = _head(zv_ref, h).T.astype(BF16)


def prep_ctx(czg, gains, qscale):
    m = czg.shape[0]
    col = lambda c0, w: pl.BlockSpec((m, w), lambda i: (0, c0 // w))
    gain = pl.BlockSpec((1, HEAD_DIM), lambda i: (0, 0))
    full = lambda shape: pl.BlockSpec(shape, lambda i: (0,) * len(shape))
    return pl.pallas_call(
        functools.partial(_prep_ctx_kernel, qscale=qscale),
        out_shape=(jax.ShapeDtypeStruct((m, NA_W), BF16),
                   jax.ShapeDtypeStruct((NA_HEADS, HEAD_DIM, m), BF16),
                   jax.ShapeDtypeStruct((NA_HEADS, m, HEAD_DIM), BF16),
                   jax.ShapeDtypeStruct((NA_HEADS, HEAD_DIM, m), BF16),
                   jax.ShapeDtypeStruct((GQA_Q_HEADS, HEAD_DIM, m), BF16),
                   jax.ShapeDtypeStruct((GQA_KV_HEADS, m, HEAD_DIM), BF16),
                   jax.ShapeDtypeStruct((GQA_KV_HEADS, HEAD_DIM, m), BF16)),
        grid=(1,),
        in_specs=[col(_C_NQ, NA_W), col(_C_NK, NA_W), col(_C_NV, NA_W), col(_C_ZQ, GQA_W // 2),
                  col(_C_ZQ + GQA_W // 2, GQA_W // 2), col(_C_ZK, GQA_KV_W), col(_C_ZV, GQA_KV_W),
                  gain, gain, gain, gain],
        out_specs=(full((m, NA_W)),
                   full((NA_HEADS, HEAD_DIM, m)), full((NA_HEADS, m, HEAD_DIM)), full((NA_HEADS, HEAD_DIM, m)),
                   full((GQA_Q_HEADS, HEAD_DIM, m)), full((GQA_KV_HEADS, m, HEAD_DIM)),
                   full((GQA_KV_HEADS, HEAD_DIM, m))),
        compiler_params=_cp("arbitrary"),
        name="prep_ctx",
    )(czg, czg, czg, czg, czg, czg, czg, *gains)


SCORE_BOUND_SAFE = 60.0
FLASH_TQ = 512
FLASH_UNITS = 64


def _flash_kernel(*refs, group, tk, has_extra, n_cast):
    n_in = 5 if has_extra else 3
    qT_ref, k_ref, vT_ref = refs[:3]
    k2_ref, vT2_ref = refs[3:5] if has_extra else (None, None)
    o_ref = refs[n_in + n_cast]
    kmax_ref, acc_ref = refs[-2:]
    tq = min(qT_ref.shape[2], FLASH_TQ)
    nsub = qT_ref.shape[2] // tq
    cols = [(h, sub) for h in range(group) for sub in range(nsub)]
    n_keys = k_ref.shape[1]
    n_full = n_keys // tk
    rem = n_keys - n_full * tk
    unroll = max(1, FLASH_UNITS // len(cols))
    n_body = n_full // unroll

    cast_pairs = list(zip(refs[n_in:n_in + n_cast], refs[n_in + n_cast + 1:n_in + 2 * n_cast + 1]))
    in_loop = [p for p in cast_pairs if n_body > 0 and p[0].shape[0] % (n_body * HALO) == 0]
    for w_ref, wb_ref in cast_pairs:
        if not any(w_ref is q[0] for q in in_loop):
            wb_ref[...] = w_ref[...].astype(BF16)

    def cast_rows(trip):
        for w_ref, wb_ref in in_loop:
            n = w_ref.shape[0] // n_body
            rows = pl.ds(trip * n if isinstance(trip, int) else pl.multiple_of(trip * n, n), n)
            wb_ref[rows, :] = w_ref[rows, :].astype(BF16)
    tails = ([(k_ref, vT_ref, n_full * tk, rem)] if rem > 0 else []) + (
        [(k2_ref, vT2_ref, 0, k2_ref.shape[1])] if has_extra else [])

    @pl.when(pl.program_id(1) == 0)
    def _():
        def ksq(kr, start, size):
            kb = kr[0, pl.ds(start, size), :].astype(F32)
            return jnp.max(jnp.sum(kb * kb, axis=-1, keepdims=True), axis=0, keepdims=True)

        mx = jnp.zeros((1, 1), F32)
        if n_full > 0:
            mx = lax.fori_loop(
                0, n_full, lambda i, c: jnp.maximum(c, ksq(k_ref, pl.multiple_of(i * tk, tk), tk)), mx)
        for kr, _, start, size in tails:
            mx = jnp.maximum(mx, ksq(kr, start, size))
        kmax_ref[...] = mx

    def run_blocks(step, carry):
        if n_full > 0:
            carry = lax.fori_loop(
                0, n_full, lambda kb, c: step(k_ref, vT_ref, pl.multiple_of(kb * tk, tk), tk, c), carry)
        for kr, vr, start, size in tails:
            carry = step(kr, vr, start, size, carry)
        return carry

    qTs = [qT_ref[h, :, sub * tq:(sub + 1) * tq] for h, sub in cols]
    shifts = []
    for qT in qTs:
        qf = qT.astype(F32)
        qsq = jnp.sum(qf * qf, axis=0, keepdims=True)
        shifts.append(jnp.sqrt(qsq * kmax_ref[...]) * 1.01)
    safe = jnp.max(functools.reduce(jnp.maximum, shifts)) <= SCORE_BOUND_SAFE

    @pl.when(safe)
    def _():
        def step(kr, vr, start, size, l8s, nb=1):
            units = [(j, c) for j in range(nb) for c in range(len(cols))]
            kblks = [kr[0, pl.ds(start + j * size, size), :] for j in range(nb)]
            vblks = [vr[0, :, pl.ds(start + j * size, size)] for j in range(nb)]
            l8s = list(l8s)
            s_next = jnp.dot(kblks[0], qTs[0], preferred_element_type=F32)
            for u, (j, c) in enumerate(units):
                s = s_next
                if u + 1 < len(units):
                    jn, cn = units[u + 1]
                    s_next = jnp.dot(kblks[jn], qTs[cn], preferred_element_type=F32)
                p = jnp.exp2(s - shifts[c])
                l8s[c] = l8s[c] + jnp.sum(p.reshape(size // 8, 8, tq), axis=0)
                acc_ref[c] += jnp.dot(vblks[j], p.astype(BF16), preferred_element_type=F32)
            return tuple(l8s)

        acc_ref[...] = jnp.zeros_like(acc_ref)
        l8s = tuple(jnp.zeros((8, tq), F32) for _ in cols)

        def body(kb, c):
            cast_rows(kb)
            return step(k_ref, vT_ref, pl.multiple_of(kb * (tk * unroll), tk * unroll), tk, c, nb=unroll)

        if n_body > 0:
            l8s = lax.fori_loop(0, n_body, body, l8s)
        for kb in range(n_body * unroll, n_full):
            l8s = step(k_ref, vT_ref, kb * tk, tk, l8s)
        for kr, vr, start, size in tails:
            l8s = step(kr, vr, start, size, l8s)
        for c in range(len(cols)):
            l = jnp.sum(l8s[c], axis=0, keepdims=True)
            acc_ref[c] = acc_ref[c] * (1.0 / l)

    @pl.when(jnp.logical_not(safe))
    def _():
        for trip in range(n_body if in_loop else 0):
            cast_rows(trip)
        for c in range(len(cols)):
            def step(kr, vr, start, size, carry, qT=qTs[c]):
                m, l, acc = carry
                kblk = kr[0, pl.ds(start, size), :]
                s = jnp.dot(kblk, qT, preferred_element_type=F32)
                m_new = jnp.maximum(m, jnp.max(s, axis=0, keepdims=True))
                alpha = jnp.exp2(m - m_new)
                p = jnp.exp2(s - m_new)
                l = alpha * l + jnp.sum(p, axis=0, keepdims=True)
                vblk = vr[0, :, pl.ds(start, size)]
                acc = alpha * acc + jnp.dot(vblk, p.astype(BF16), preferred_element_type=F32)
                return m_new, l, acc

            carry = (jnp.full((1, tq), NEG_BIG, F32), jnp.zeros((1, tq), F32),
                     jnp.zeros((HEAD_DIM, tq), F32))
            _, l, acc = run_blocks(step, carry)
            acc_ref[c] = acc * (1.0 / l)

    for c, (h, sub) in enumerate(cols):
        o_ref[sub * tq:(sub + 1) * tq, h * HEAD_DIM:(h + 1) * HEAD_DIM] = acc_ref[c].T.astype(o_ref.dtype)


def flash_attention(qT, k, vT, tq, extra=None, casts=()):
    hq, _, mq = qT.shape
    hkv = k.shape[0]
    group = hq // hkv
    nq = mq // tq
    kv_specs = lambda kk, vv: [
        pl.BlockSpec((1, kk.shape[1], HEAD_DIM), lambda g, i: (g, 0, 0), pipeline_mode=pl.Buffered(1)),
        pl.BlockSpec((1, HEAD_DIM, vv.shape[2]), lambda g, i: (g, 0, 0), pipeline_mode=pl.Buffered(1))]
    args, specs = [k, vT], kv_specs(k, vT)
    if extra is not None:
        args += list(extra)
        specs += kv_specs(*extra)
    out_shapes = [jax.ShapeDtypeStruct((mq, hq * HEAD_DIM), BF16)]
    out_specs = [pl.BlockSpec((tq, group * HEAD_DIM), lambda g, i: (i, g))]
    for w, layer in casts:
        _, kk, nn = w.shape
        rb = kk // (hkv * nq)
        args.append(w)
        specs.append(pl.BlockSpec((None, rb, nn), functools.partial(
            lambda g, i, layer: (layer, g * nq + i, 0), layer=layer)))
        out_shapes.append(jax.ShapeDtypeStruct((1, kk, nn), BF16))
        out_specs.append(pl.BlockSpec((None, rb, nn), lambda g, i: (0, g * nq + i, 0)))
    outs = pl.pallas_call(
        functools.partial(_flash_kernel, group=group, tk=512, has_extra=extra is not None, n_cast=len(casts)),
        out_shape=out_shapes,
        grid=(hkv, nq),
        in_specs=[pl.BlockSpec((group, HEAD_DIM, tq), lambda g, i: (g, 0, i))] + specs,
        out_specs=out_specs,
        scratch_shapes=[pltpu.VMEM((1, 1), F32),
                        pltpu.VMEM((group * max(1, tq // FLASH_TQ), HEAD_DIM, min(tq, FLASH_TQ)), F32)],
        compiler_params=_cp("parallel", "arbitrary"),
        name="flash_attention",
    )(qT, *args)
    return outs[0], list(outs[1:])


NA_RB = 8
NA_TOK = NA_RB * GRID_W
NA_PAIR = 2 * GRID_W
NA_SLAB_ROWS = NA_ROWS + 2
NA_SLAB = NA_SLAB_ROWS * GRID_W
NA_PATTERNS = 5


def _na_kernel(q_ref, kp, kc, kn, vp, vc, vn, ck_ref, cv_ref, bias_ref, o_ref, kbuf, vbuf):
    b = pl.program_id(0)
    rows = pl.num_programs(0) * NA_RB
    kbuf[0:NA_TOK, :] = kp[...]
    kbuf[NA_TOK:2 * NA_TOK, :] = kc[...]
    kbuf[2 * NA_TOK:3 * NA_TOK, :] = kn[...]
    vbuf[0:NA_TOK, :] = vp[...]
    vbuf[NA_TOK:2 * NA_TOK, :] = vc[...]
    vbuf[2 * NA_TOK:3 * NA_TOK, :] = vn[...]
    nt = (((1,), (1,)), ((), ()))

    def slab(pair):
        r = b * NA_RB + 2 * pair
        u0 = jnp.clip(r - NA_ROWS // 2, 0, rows - NA_ROWS)
        off = pl.multiple_of((u0 - b * NA_RB + NA_RB) * GRID_W, NA_PAIR)
        pat = jnp.where(r < NA_ROWS // 2, r // 2,
                        jnp.where(r >= rows - NA_ROWS // 2, 3 + (r - (rows - NA_ROWS // 2)) // 2, 2))
        return off, pat

    slabs = [slab(pair) for pair in range(NA_RB // 2)]
    units = [(pair, h) for pair in range(NA_RB // 2) for h in range(NA_HEADS)]

    def scores(pair, h):
        off, pat = slabs[pair]
        sl = slice(h * HEAD_DIM, (h + 1) * HEAD_DIM)
        qh = q_ref[pair * NA_PAIR:(pair + 1) * NA_PAIR, sl]
        s = lax.dot_general(qh, kbuf[pl.ds(off, NA_SLAB), sl], nt, preferred_element_type=F32)
        sc = lax.dot_general(qh, ck_ref[:, sl], nt, preferred_element_type=F32)
        return s + bias_ref[pat, h], sc

    nxt = scores(*units[0])
    for u, (pair, h) in enumerate(units):
        s, sc = nxt
        if u + 1 < len(units):
            nxt = scores(*units[u + 1])
        off, _ = slabs[pair]
        sl = slice(h * HEAD_DIM, (h + 1) * HEAD_DIM)
        m = jnp.maximum(jnp.max(s, axis=-1, keepdims=True), jnp.max(sc, axis=-1, keepdims=True))
        p = jnp.exp2(s - m)
        pc = jnp.exp2(sc - m)
        l = jnp.sum(p, axis=-1, keepdims=True) + jnp.sum(pc, axis=-1, keepdims=True)
        o = (jnp.dot(p.astype(BF16), vbuf[pl.ds(off, NA_SLAB), sl], preferred_element_type=F32)
             + jnp.dot(pc.astype(BF16), cv_ref[:, sl], preferred_element_type=F32))
        o_ref[pair * NA_PAIR:(pair + 1) * NA_PAIR, sl] = (o * (1.0 / l)).astype(o_ref.dtype)


def neighborhood_attention(qn, kn, zg, ckn, czg, bias):
    m = qn.shape[0]
    nb = m // NA_TOK
    vcol = _C_NV // NA_W
    blk = (NA_TOK, NA_W)
    return pl.pallas_call(
        _na_kernel,
        out_shape=jax.ShapeDtypeStruct((m, NA_W), BF16),
        grid=(nb,),
        in_specs=[pl.BlockSpec(blk, lambda b: (b, 0)),
                  pl.BlockSpec(blk, lambda b: (jnp.maximum(b - 1, 0), 0)),
                  pl.BlockSpec(blk, lambda b: (b, 0)),
                  pl.BlockSpec(blk, lambda b: (jnp.minimum(b + 1, nb - 1), 0)),
                  pl.BlockSpec(blk, lambda b: (jnp.maximum(b - 1, 0), vcol)),
                  pl.BlockSpec(blk, lambda b: (b, vcol)),
                  pl.BlockSpec(blk, lambda b: (jnp.minimum(b + 1, nb - 1), vcol)),
                  pl.BlockSpec((CTX_LEN, NA_W), lambda b: (0, 0)),
                  pl.BlockSpec((CTX_LEN, NA_W), lambda b: (0, vcol)),
                  pl.BlockSpec((NA_PATTERNS, NA_HEADS, NA_PAIR, NA_SLAB), lambda b: (0, 0, 0, 0))],
        out_specs=pl.BlockSpec(blk, lambda b: (b, 0)),
        scratch_shapes=[pltpu.VMEM((3 * NA_TOK, NA_W), BF16), pltpu.VMEM((3 * NA_TOK, NA_W), BF16)],
        compiler_params=_cp("arbitrary"),
        name="neighborhood_attention",
    )(qn, kn, kn, kn, zg, zg, zg, ckn, czg, bias)


def na_bias_table(rpb):
    col = np.arange(GRID_W)
    c0 = np.clip(col - NA_COLS // 2, 0, GRID_W - NA_COLS)
    kc = np.arange(GRID_W)
    inside = (kc[None, :] >= c0[:, None]) & (kc[None, :] < c0[:, None] + NA_COLS)
    dc = kc[None, :] - col[:, None] + NA_COLS - 1
    onehot = (dc[None] == np.arange(2 * NA_COLS - 1)[:, None, None]) & inside[None]
    t = jnp.einsum('hrd,dck->hrck', rpb.astype(F32) * LOG2E, jnp.asarray(onehot, F32),
                   precision=lax.Precision.HIGHEST)
    t = jnp.where(inside[None, None], t, NEG_BIG)
    masked = jnp.full((NA_HEADS, GRID_W, GRID_W), NEG_BIG, F32)
    patterns = [(0, 0), (2, 0), (4, 1), (4, 0), (6, 0)]
    tabs = []
    for a, e in patterns:
        per_q = []
        for start, dr0 in ((0, NA_ROWS - 1 - a), (e, NA_ROWS - 2 - a)):
            blocks = [t[:, w + dr0] if start <= w < start + NA_ROWS else masked for w in range(NA_SLAB_ROWS)]
            per_q.append(jnp.stack(blocks, axis=2).reshape(NA_HEADS, GRID_W, NA_SLAB))
        tabs.append(jnp.concatenate(per_q, axis=1))
    return jnp.stack(tabs, axis=0)


def _dft_mats(n):
    a = 2.0 * np.pi * np.outer(np.arange(n), np.arange(n)) / n
    return np.cos(a), np.sin(a)


def _chan_dft_kernel(z_ref, w_ref, o_ref):
    for g in range(FOURIER_GROUPS):
        x = z_ref[:, g * HEAD_DIM:(g + 1) * HEAD_DIM]
        r = jnp.dot(x, w_ref[...], preferred_element_type=F32)
        o_ref[0, :, g * HEAD_DIM:(g + 1) * HEAD_DIM] = r[:, :HEAD_DIM].astype(o_ref.dtype)
        o_ref[1, :, g * HEAD_DIM:(g + 1) * HEAD_DIM] = r[:, HEAD_DIM:].astype(o_ref.dtype)


def chan_dft(zg, wch):
    m = zg.shape[0]
    tm = min(m, 1024)
    return pl.pallas_call(
        _chan_dft_kernel,
        out_shape=jax.ShapeDtypeStruct((2, m, FOURIER_W), BF16),
        grid=(m // tm,),
        in_specs=[pl.BlockSpec((tm, FOURIER_W), lambda i: (i, _C_ZF // FOURIER_W)),
                  pl.BlockSpec((HEAD_DIM, 2 * HEAD_DIM), lambda i: (0, 0))],
        out_specs=pl.BlockSpec((2, tm, FOURIER_W), lambda i: (0, i, 0)),
        compiler_params=_cp("parallel"),
        name="chan_dft",
    )(zg, wch)


def _left_mm_kernel(l_ref, x_ref, o_ref, *, n_out):
    x = jnp.concatenate([x_ref[0], x_ref[1]], axis=0)
    r = jnp.dot(l_ref[...], x, preferred_element_type=F32)
    if n_out == 1:
        o_ref[...] = r.astype(o_ref.dtype)
    else:
        half = r.shape[0] // 2
        o_ref[0] = r[:half].astype(o_ref.dtype)
        o_ref[1] = r[half:].astype(o_ref.dtype)


def dft_stage1(zc, lmat):
    _, r, n = zc.shape
    tn = 4096
    return pl.pallas_call(
        functools.partial(_left_mm_kernel, n_out=2),
        out_shape=jax.ShapeDtypeStruct((2, r, n), BF16),
        grid=(n // tn,),
        in_specs=[pl.BlockSpec((2 * r, 2 * r), lambda j: (0, 0)),
                  pl.BlockSpec((2, r, tn), lambda j: (0, 0, j))],
        out_specs=pl.BlockSpec((2, r, tn), lambda j: (0, 0, j)),
        compiler_params=_cp("parallel"),
        name="dft_stage1",
    )(lmat, zc)


def _dft_stage3_kernel(l_ref, x_ref, o_ref, *, kb):
    for i in range(kb):
        x = jnp.concatenate([x_ref[0, i], x_ref[1, i]], axis=0)
        o_ref[i] = jnp.dot(l_ref[i], x, preferred_element_type=F32).astype(o_ref.dtype)


def dft_stage3(a, tables):
    _, n1, n2, c = a.shape
    kb = 8
    return pl.pallas_call(
        functools.partial(_dft_stage3_kernel, kb=kb),
        out_shape=jax.ShapeDtypeStruct((n1, n2, c), BF16),
        grid=(n1 // kb,),
        in_specs=[pl.BlockSpec((kb, n2, 2 * n2), lambda j: (j, 0, 0)),
                  pl.BlockSpec((2, kb, n2, c), lambda j: (0, j, 0, 0))],
        out_specs=pl.BlockSpec((kb, n2, c), lambda j: (j, 0, 0)),
        compiler_params=_cp("parallel"),
        name="dft_stage3",
    )(tables, a)


def dft_direct(zc, lmat):
    _, t, c = zc.shape
    return pl.pallas_call(
        functools.partial(_left_mm_kernel, n_out=1),
        out_shape=jax.ShapeDtypeStruct((t, c), BF16),
        grid=(1,),
        in_specs=[pl.BlockSpec((t, 2 * t), lambda j: (0, 0)),
                  pl.BlockSpec((2, t, c), lambda j: (0, 0, 0))],
        out_specs=pl.BlockSpec((t, c), lambda j: (0, 0)),
        compiler_params=_cp("arbitrary"),
        name="dft_direct",
    )(lmat, zc)


def _fourier_tables():
    c128, s128 = _dft_mats(HEAD_DIM)
    wch = np.concatenate([c128, -s128], axis=1)
    n1 = SEQ // HEAD_DIM
    l1 = np.block([[c128, s128], [-s128, c128]])
    k1 = np.arange(n1)[:, None, None]
    k2 = np.arange(n1)[None, :, None]
    t2 = np.arange(n1)[None, None, :]
    ang = 2.0 * np.pi * ((n1 * k2 + k1) * t2 % SEQ) / SEQ
    norm = 1.0 / math.sqrt(SEQ * HEAD_DIM)
    l3 = np.concatenate([np.cos(ang), np.sin(ang)], axis=2) * norm
    cc, sc = _dft_mats(CTX_LEN)
    lc = np.concatenate([cc, sc], axis=1) / math.sqrt(CTX_LEN * HEAD_DIM)
    as_bf = lambda a: jnp.asarray(a, F32).astype(BF16)
    return as_bf(wch), as_bf(l1), as_bf(l3), as_bf(lc)


def fourier_latent(zg, tabs):
    wch, l1, l3, _ = tabs
    n1 = SEQ // HEAD_DIM
    zc = chan_dft(zg, wch)
    a = dft_stage1(zc.reshape(2, n1, n1 * FOURIER_W), l1)
    o3 = dft_stage3(a.reshape(2, n1, n1, FOURIER_W), l3)
    return o3.transpose(1, 0, 2).reshape(SEQ, FOURIER_W)


def fourier_ctx(czg, tabs):
    wch, _, _, lc = tabs
    return dft_direct(chan_dft(czg, wch), lc)


def _rope_tables(n_tok):
    t = jnp.arange(n_tok)
    row = (t // GRID_W).astype(F32)
    col = (t % GRID_W).astype(F32)
    n_freq = HEAD_DIM // 4
    inv_freq = ROPE_THETA ** (-jnp.arange(n_freq, dtype=F32) / n_freq)
    ang = jnp.concatenate([row[:, None] * inv_freq, col[:, None] * inv_freq], axis=-1)
    cos, sin = jnp.cos(ang), jnp.sin(ang)
    return jnp.concatenate([cos, cos], axis=-1), jnp.concatenate([-sin, sin], axis=-1)


def kernel(x, c, ctx, c_ctx, w_ada, b_ada, norm1, w_in, conv_w, na_q_gain, na_k_gain, na_rpb, gqa_q_gain, gqa_k_gain, w_conv_out, w_fourier_out, w_na_out, w_gqa_out, w_gate, b_gate, w_o, norm2, w_up, ffn_conv_w, w_down):
    d = D_MODEL
    h = x[0]
    hc = ctx[0]
    rope = _rope_tables(SEQ)
    ftabs = _fourier_tables()
    qscale2 = HEAD_DIM ** -0.5 * LOG2E
    cvec = jnp.zeros((8, d), F32).at[0].set(c[0]).at[1].set(c_ctx)
    row = lambda v: v.reshape(1, -1)

    w_in_b, w_gate_b = w_in[0:1].astype(BF16), w_gate[0:1].astype(BF16)
    late_w = (w_conv_out, w_fourier_out, w_na_out, w_gqa_out, w_o, w_up, w_down)
    b_ada3 = b_ada.reshape(DEPTH, 1, -1)
    b_gate3 = b_gate.reshape(DEPTH, 1, -1)

    for i in range(DEPTH):
        last = i == DEPTH - 1
        bias_tab = na_bias_table(na_rpb[i])
        gains = (row(na_q_gain[i]), row(na_k_gain[i]), row(gqa_q_gain[i]), row(gqa_k_gain[i]))

        mods = ada_mod(cvec, w_ada, b_ada3, i)
        lat = [mods[0:1, k * d:(k + 1) * d] for k in range(6)]
        cm = [mods[1:2, k * d:(k + 1) * d] for k in range(6)]

        zg = in_proj(h, row(norm1[i]), lat[0], lat[1], w_in_b, w_gate_b, b_gate3, i, "in_proj")
        czg = in_proj(hc, row(norm1[i]), cm[0], cm[1], w_in_b, w_gate_b, b_gate3, i, "in_proj_ctx")

        cnk_rows, cnqT, cnk, cnvT, cqT, ck, cvT = prep_ctx(czg, gains, qscale2)
        nq, nk, gqT, gk, gvT = prep_latent(zg, gains, rope, qscale2)

        y_conv = conv_mixer(zg, conv_w, i)
        y_four = fourier_latent(zg, ftabs)
        y_na = neighborhood_attention(nq, nk, zg, cnk_rows, czg, bias_tab)
        casts = [(w, i) for w in late_w] + ([] if last else [(w_in, i + 1), (w_gate, i + 1)])
        y_gqa, wb = flash_attention(gqT, gk, gvT, tq=1024, extra=(ck, cvT), casts=casts)
        merge_w, w_o_b, w_up_b, w_down_b = wb[0:4], wb[4], wb[5], wb[6]

        merged = merge_branches([y_conv, y_four, y_na, y_gqa], merge_w, zg)
        h = mm_residual(merged, w_o_b, h, lat[2], "out_proj", in_place=i > 0)
        act = ffn_up_act(h, row(norm2[i]), lat[3], lat[4], w_up_b, ffn_conv_w, i, "ffn_up_act")
        h = mm_residual(act, w_down_b, h, lat[5], "ffn_down", in_place=True)

        if not last:
            cy_conv = conv_mixer(czg, conv_w, i)
            cy_four = fourier_ctx(czg, ftabs)
            cy_na, _ = flash_attention(cnqT, cnk, cnvT, tq=CTX_LEN)
            cy_gqa, _ = flash_attention(cqT, ck, cvT, tq=CTX_LEN)
            cmerged = merge_branches([cy_conv, cy_four, cy_na, cy_gqa], merge_w, czg)
            hc = mm_residual(cmerged, w_o_b, hc, cm[2], "out_proj_ctx", in_place=i > 0)
            cact = ffn_up_act(hc, row(norm2[i]), cm[3], cm[4], w_up_b, ffn_conv_w, i, "ffn_up_act_ctx")
            hc = mm_residual(cact, w_down_b, hc, cm[5], "ffn_down_ctx", in_place=True)
            w_in_b, w_gate_b = wb[7], wb[8]

    return h[None]
```

```python
import functools
import math

import jax
import jax.numpy as jnp
import numpy as np
from jax import lax
from jax.experimental import pallas as pl
from jax.experimental.pallas import tpu as pltpu

D_MODEL = 2048
SEQ = 16384
DEPTH = 4
GRID_W = 64
CTX_LEN = 256
HEAD_DIM = 128
CONV_W = 512
FOURIER_GROUPS = 4
FOURIER_W = 512
NA_HEADS = 4
NA_W = 512
NA_ROWS = 8
NA_COLS = 16
GQA_Q_HEADS = 8
GQA_KV_HEADS = 2
GQA_GROUP = 4
GQA_W = 1024
GQA_KV_W = 256
N_BRANCH = 4
N_IN = 5120
D_FF = 5632
ROPE_THETA = 10000.0
EPS = 1e-6

_C_XA, _C_BG, _C_CG = 0, 512, 1024
_C_ZF = 1536
_C_NQ, _C_NK, _C_NV = 2048, 2560, 3072
_C_ZQ, _C_ZK, _C_ZV = 3584, 4608, 4864
_C_GATE = N_IN

BF16 = jnp.bfloat16
F32 = jnp.float32
V7X_VMEM_LIMIT = 52 * 1024 * 1024
NEG_BIG = -1e30
HALO = 16
NORM_CHUNK = 256
LOG2E = 1.4426950408889634


def _cp(*sem):
    return pltpu.CompilerParams(dimension_semantics=sem, vmem_limit_bytes=V7X_VMEM_LIMIT)


def _sigmoid(t):
    return 0.5 * jnp.tanh(0.5 * t) + 0.5


def _norm_mod(x, g, sh, sc):
    ms = jnp.mean(x * x, axis=-1, keepdims=True)
    return (x * lax.rsqrt(ms + EPS) * g) * (1.0 + sc) + sh


def _ada_kernel(c_ref, w_ref, b_ref, o_ref):
    c = c_ref[...]
    s = c * _sigmoid(c)
    o_ref[...] = jnp.dot(s.astype(BF16), w_ref[...].astype(BF16),
                         preferred_element_type=F32) + b_ref[...]


def ada_mod(cvec, w, b, layer):
    n = w.shape[2]
    tn = 1024
    return pl.pallas_call(
        _ada_kernel,
        out_shape=jax.ShapeDtypeStruct((8, n), F32),
        grid=(n // tn,),
        in_specs=[pl.BlockSpec((8, D_MODEL), lambda j: (0, 0)),
                  pl.BlockSpec((None, D_MODEL, tn), lambda j: (layer, 0, j)),
                  pl.BlockSpec((None, 1, tn), lambda j: (layer, 0, j))],
        out_specs=pl.BlockSpec((8, tn), lambda j: (0, j)),
        compiler_params=_cp("parallel"),
        name="ada_mod",
    )(cvec, w, b)


def _in_proj_kernel(h_ref, g_ref, sh_ref, sc_ref, win_ref, wgate_ref, b_ref, o_ref, xn_ref, *, n_plain):
    j = pl.program_id(1)
    tm = h_ref.shape[0]
    cm = min(tm, NORM_CHUNK)

    @pl.when(j == 0)
    def _():
        for c in range(tm // cm):
            rows = slice(c * cm, (c + 1) * cm)
            xc = _norm_mod(h_ref[rows, :], g_ref[...], sh_ref[...], sc_ref[...]).astype(BF16)
            xn_ref[rows, :] = xc
            o_ref[rows, :] = jnp.dot(xc, win_ref[...], preferred_element_type=F32).astype(o_ref.dtype)

    @pl.when(jnp.logical_and(j > 0, j < n_plain))
    def _():
        o_ref[...] = jnp.dot(xn_ref[...], win_ref[...], preferred_element_type=F32).astype(o_ref.dtype)

    @pl.when(j >= n_plain)
    def _():
        t = jnp.dot(xn_ref[...], wgate_ref[...], preferred_element_type=F32) + b_ref[...]
        o_ref[...] = _sigmoid(t).astype(o_ref.dtype)


def in_proj(h, gain, shift, scale, w_in, w_gate, b_gate, layer, name):
    m, d = h.shape
    n_in, n_gate = w_in.shape[2], w_gate.shape[2]
    tm = min(m, 1024)
    tn = 1024
    n_plain = n_in // tn
    vec = pl.BlockSpec((1, d), lambda i, j: (0, 0))
    return pl.pallas_call(
        functools.partial(_in_proj_kernel, n_plain=n_plain),
        out_shape=jax.ShapeDtypeStruct((m, n_in + n_gate), BF16),
        grid=(m // tm, (n_in + n_gate) // tn),
        in_specs=[pl.BlockSpec((tm, d), lambda i, j: (i, 0)), vec, vec, vec,
                  pl.BlockSpec((None, d, tn), lambda i, j: (0, 0, jnp.minimum(j, n_plain - 1))),
                  pl.BlockSpec((None, d, tn), lambda i, j: (0, 0, jnp.maximum(j - n_plain, 0))),
                  pl.BlockSpec((None, 1, tn), lambda i, j: (layer, 0, jnp.maximum(j - n_plain, 0)))],
        out_specs=pl.BlockSpec((tm, tn), lambda i, j: (i, j)),
        scratch_shapes=[pltpu.VMEM((tm, d), BF16)],
        compiler_params=_cp("parallel", "arbitrary"),
        name=name,
    )(h, gain, shift, scale, w_in, w_gate, b_gate)


def _mm_res_kernel(x_ref, w_ref, h_ref, g_ref, o_ref):
    acc = jnp.dot(x_ref[...], w_ref[...], preferred_element_type=F32)
    o_ref[...] = h_ref[...] + g_ref[...] * acc


def mm_residual(x, w, h, gate, name, in_place):
    m, k = x.shape
    n = w.shape[2]
    if k <= D_MODEL:
        tm, tn = min(m, 512), n
    else:
        tm, tn = min(m, 1024), 512
    return pl.pallas_call(
        _mm_res_kernel,
        out_shape=jax.ShapeDtypeStruct((m, n), F32),
        grid=(m // tm, n // tn),
        in_specs=[pl.BlockSpec((tm, k), lambda i, j: (i, 0)),
                  pl.BlockSpec((None, k, tn), lambda i, j: (0, 0, j)),
                  pl.BlockSpec((tm, tn), lambda i, j: (i, j)),
                  pl.BlockSpec((1, tn), lambda i, j: (0, j))],
        out_specs=pl.BlockSpec((tm, tn), lambda i, j: (i, j)),
        input_output_aliases={2: 0} if in_place else {},
        compiler_params=_cp("parallel", "arbitrary"),
        name=name,
    )(x, w, h, gate)


def _merge_kernel(yc, yf, yn, yg, wc, wf, wn, wg, g0, g1, g2, g3, o_ref):
    def term(y, w, g):
        return g[...].astype(F32) * jnp.dot(y[...], w[...], preferred_element_type=F32)

    o_ref[...] = (term(yc, wc, g0) + term(yf, wf, g1) + term(yn, wn, g2)
                  + term(yg, wg, g3)).astype(o_ref.dtype)


def merge_branches(ys, ws, zg):
    m = zg.shape[0]
    tm = min(m, 1024)
    tn = 1024
    y_specs = [pl.BlockSpec((tm, y.shape[1]), lambda i, j: (i, 0)) for y in ys]
    w_specs = [pl.BlockSpec((None, w.shape[1], tn), lambda i, j: (0, 0, j)) for w in ws]
    g_specs = [pl.BlockSpec((tm, tn), functools.partial(
        lambda i, j, b: (i, (_C_GATE + b * D_MODEL) // tn + j), b=b)) for b in range(N_BRANCH)]
    return pl.pallas_call(
        _merge_kernel,
        out_shape=jax.ShapeDtypeStruct((m, D_MODEL), BF16),
        grid=(m // tm, D_MODEL // tn),
        in_specs=y_specs + w_specs + g_specs,
        out_specs=pl.BlockSpec((tm, tn), lambda i, j: (i, j)),
        compiler_params=_cp("parallel", "arbitrary"),
        name="merge",
    )(*ys, *ws, zg, zg, zg, zg)


def _conv3(p, prev_row, next_row, w_ref):
    tm = p.shape[0]
    row = lax.broadcasted_iota(jnp.int32, p.shape, 0)
    up = jnp.where(row == 0, prev_row, pltpu.roll(p, 1, 0))
    dn = jnp.where(row == tm - 1, next_row, pltpu.roll(p, tm - 1, 0))
    return up * w_ref[0:1, :] + p * w_ref[1:2, :] + dn * w_ref[2:3, :]


def _halo_specs(tm, tc, m, col):
    nb = m // HALO
    per = tm // HALO
    main = pl.BlockSpec((tm, tc), lambda i: (i, col))
    prev = pl.BlockSpec((HALO, tc), lambda i: (jnp.maximum(i * per - 1, 0), col))
    nxt = pl.BlockSpec((HALO, tc), lambda i: (jnp.minimum((i + 1) * per, nb - 1), col))
    return main, prev, nxt


def _convmix_kernel(xa, xap, xan, cg, cgp, cgn, bg, w_ref, o_ref):
    i = pl.program_id(0)
    last = pl.num_programs(0) - 1
    p = cg[...].astype(F32) * xa[...].astype(F32)
    pp = cgp[...].astype(F32)[HALO - 1:HALO, :] * xap[...].astype(F32)[HALO - 1:HALO, :]
    pn = cgn[...].astype(F32)[0:1, :] * xan[...].astype(F32)[0:1, :]
    pp = jnp.where(i > 0, pp, 0.0)
    pn = jnp.where(i < last, pn, 0.0)
    o_ref[...] = (bg[...].astype(F32) * _conv3(p, pp, pn, w_ref)).astype(o_ref.dtype)


def conv_mixer(zg, w, layer):
    m = zg.shape[0]
    tm = min(m, 1024)
    tc = CONV_W
    xa = _halo_specs(tm, tc, m, _C_XA // tc)
    cg = _halo_specs(tm, tc, m, _C_CG // tc)
    bg = pl.BlockSpec((tm, tc), lambda i: (i, _C_BG // tc))
    return pl.pallas_call(
        _convmix_kernel,
        out_shape=jax.ShapeDtypeStruct((m, CONV_W), BF16),
        grid=(m // tm,),
        in_specs=[*xa, *cg, bg, pl.BlockSpec((None, 3, tc), lambda i: (layer, 0, 0))],
        out_specs=pl.BlockSpec((tm, tc), lambda i: (i, 0)),
        compiler_params=_cp("parallel"),
        name="conv_mixer",
    )(zg, zg, zg, zg, zg, zg, zg, w)


def _ffn_up_act_kernel(h_ref, hp_ref, hn_ref, g_ref, sh_ref, sc_ref, wa_ref, wg_ref, cwa_ref, cwg_ref,
                       o_ref, xn_ref):
    i = pl.program_id(0)
    j = pl.program_id(1)
    last = pl.num_programs(0) - 1
    tm = h_ref.shape[0]
    rows = tm + 2 * HALO

    cm = min(tm, NORM_CHUNK)
    nch = tm // cm

    def finish(ua, ug):
        def conv(u, cw_ref):
            c = (pltpu.roll(u, 1, 0) * cw_ref[0:1, :] + u * cw_ref[1:2, :]
                 + pltpu.roll(u, rows - 1, 0) * cw_ref[2:3, :])
            return c[HALO:HALO + tm]

        ca = conv(ua, cwa_ref)
        cgt = conv(ug, cwg_ref)
        o_ref[...] = (ca * _sigmoid(ca) * cgt).astype(o_ref.dtype)

    up = lambda x, w_ref: jnp.dot(x, w_ref[...], preferred_element_type=F32)

    @pl.when(j == 0)
    def _():
        nm = lambda x: _norm_mod(x, g_ref[...], sh_ref[...], sc_ref[...])
        xn_ref[0:HALO, :] = jnp.where(i > 0, nm(hp_ref[...]), 0.0).astype(BF16)
        uas, ugs = [], []
        for c in range(nch):
            lo, hi = c * cm, (c + 1) * cm
            xn_ref[HALO + lo:HALO + hi, :] = nm(h_ref[lo:hi, :]).astype(BF16)
            if c == nch - 1:
                xn_ref[HALO + tm:rows, :] = jnp.where(i < last, nm(hn_ref[...]), 0.0).astype(BF16)
            r0 = 0 if c == 0 else HALO + lo
            r1 = rows if c == nch - 1 else HALO + hi
            xc = xn_ref[r0:r1, :]
            uas.append(up(xc, wa_ref))
            ugs.append(up(xc, wg_ref))
        cat = lambda parts: parts[0] if len(parts) == 1 else jnp.concatenate(parts, axis=0)
        finish(cat(uas), cat(ugs))

    @pl.when(j > 0)
    def _():
        xn = xn_ref[...]
        finish(up(xn, wa_ref), up(xn, wg_ref))


def ffn_up_act(h, gain, shift, scale, w_up, conv_w, layer, name):
    m, d = h.shape
    tm = min(m, 1024)
    tn = 512
    nj = D_FF // tn
    nb = m // HALO
    per = tm // HALO
    vec = pl.BlockSpec((1, d), lambda i, j: (0, 0))
    return pl.pallas_call(
        _ffn_up_act_kernel,
        out_shape=jax.ShapeDtypeStruct((m, D_FF), BF16),
        grid=(m // tm, nj),
        in_specs=[pl.BlockSpec((tm, d), lambda i, j: (i, 0)),
                  pl.BlockSpec((HALO, d), lambda i, j: (jnp.maximum(i * per - 1, 0), 0)),
                  pl.BlockSpec((HALO, d), lambda i, j: (jnp.minimum((i + 1) * per, nb - 1), 0)),
                  vec, vec, vec,
                  pl.BlockSpec((None, d, tn), lambda i, j: (0, 0, j)),
                  pl.BlockSpec((None, d, tn), lambda i, j: (0, 0, nj + j)),
                  pl.BlockSpec((None, 3, tn), lambda i, j: (layer, 0, j)),
                  pl.BlockSpec((None, 3, tn), lambda i, j: (layer, 0, nj + j))],
        out_specs=pl.BlockSpec((tm, tn), lambda i, j: (i, j)),
        scratch_shapes=[pltpu.VMEM((tm + 2 * HALO, d), BF16)],
        compiler_params=_cp("parallel", "arbitrary"),
        name=name,
    )(h, h, h, gain, shift, scale, w_up, w_up, conv_w, conv_w)


def _head(z_ref, h, gain_ref=None, rope=None, scale=1.0):
    x = z_ref[:, h * HEAD_DIM:(h + 1) * HEAD_DIM].astype(F32)
    if gain_ref is not None:
        ms = jnp.mean(x * x, axis=-1, keepdims=True)
        x = x * lax.rsqrt(ms + EPS) * gain_ref[...]
    if rope is not None:
        x = x * rope[0][...] + pltpu.roll(x, HEAD_DIM // 2, 1) * rope[1][...]
    if scale != 1.0:
        x = x * scale
    return x


def _prep_latent_kernel(nq_ref, nk_ref, zqa_ref, zqb_ref, zk_ref, zv_ref, nqg, nkg, gqg, gkg, cos_ref, sin_ref,
                        nq_o, nk_o, gqT_o, k_o, vT_o, *, qscale):
    rope = (cos_ref, sin_ref)
    for h in range(NA_HEADS):
        sl = slice(h * HEAD_DIM, (h + 1) * HEAD_DIM)
        nq_o[:, sl] = _head(nq_ref, h, nqg, scale=qscale).astype(BF16)
        nk_o[:, sl] = _head(nk_ref, h, nkg).astype(BF16)
    for h in range(GQA_Q_HEADS):
        src = zqa_ref if h < GQA_Q_HEADS // 2 else zqb_ref
        gqT_o[h] = _head(src, h % (GQA_Q_HEADS // 2), gqg, rope, qscale).T.astype(BF16)
    for h in range(GQA_KV_HEADS):
        k_o[h] = _head(zk_ref, h, gkg, rope).astype(BF16)
        vT_o[h] = _head(zv_ref, h).T.astype(BF16)


def prep_latent(zg, gains, rope, qscale):
    m = zg.shape[0]
    tm = 1024
    col = lambda c0, w: pl.BlockSpec((tm, w), lambda i: (i, c0 // w))
    gain = pl.BlockSpec((1, HEAD_DIM), lambda i: (0, 0))
    tab = pl.BlockSpec((tm, HEAD_DIM), lambda i: (i, 0))
    return pl.pallas_call(
        functools.partial(_prep_latent_kernel, qscale=qscale),
        out_shape=(jax.ShapeDtypeStruct((m, NA_W), BF16), jax.ShapeDtypeStruct((m, NA_W), BF16),
                   jax.ShapeDtypeStruct((GQA_Q_HEADS, HEAD_DIM, m), BF16),
                   jax.ShapeDtypeStruct((GQA_KV_HEADS, m, HEAD_DIM), BF16),
                   jax.ShapeDtypeStruct((GQA_KV_HEADS, HEAD_DIM, m), BF16)),
        grid=(m // tm,),
        in_specs=[col(_C_NQ, NA_W), col(_C_NK, NA_W), col(_C_ZQ, GQA_W // 2), col(_C_ZQ + GQA_W // 2, GQA_W // 2),
                  col(_C_ZK, GQA_KV_W), col(_C_ZV, GQA_KV_W), gain, gain, gain, gain, tab, tab],
        out_specs=(pl.BlockSpec((tm, NA_W), lambda i: (i, 0)), pl.BlockSpec((tm, NA_W), lambda i: (i, 0)),
                   pl.BlockSpec((GQA_Q_HEADS, HEAD_DIM, tm), lambda i: (0, 0, i)),
                   pl.BlockSpec((GQA_KV_HEADS, tm, HEAD_DIM), lambda i: (0, i, 0)),
                   pl.BlockSpec((GQA_KV_HEADS, HEAD_DIM, tm), lambda i: (0, 0, i))),
        compiler_params=_cp("parallel"),
        name="prep_latent",
    )(zg, zg, zg, zg, zg, zg, *gains, *rope)


def _prep_ctx_kernel(nq_ref, nk_ref, nv_ref, zqa_ref, zqb_ref, zk_ref, zv_ref, nqg, nkg, gqg, gkg,
                     nk_o, cnqT_o, cnk_o, cnvT_o, cqT_o, ck_o, cvT_o, *, qscale):
    for h in range(NA_HEADS):
        sl = slice(h * HEAD_DIM, (h + 1) * HEAD_DIM)
        kn = _head(nk_ref, h, nkg).astype(BF16)
        nk_o[:, sl] = kn
        cnk_o[h] = kn
        cnqT_o[h] = _head(nq_ref, h, nqg, scale=qscale).T.astype(BF16)
        cnvT_o[h] = _head(nv_ref, h).T.astype(BF16)
    for h in range(GQA_Q_HEADS):
        src = zqa_ref if h < GQA_Q_HEADS // 2 else zqb_ref
        cqT_o[h] = _head(src, h % (GQA_Q_HEADS // 2), gqg, scale=qscale).T.astype(BF16)
    for h in range(GQA_KV_HEADS):
        ck_o[h] = _head(zk_ref, h, gkg).astype(BF16)
        cvT_o[h] = _head(zv_ref, h).T.astype(BF16)


def prep_ctx(czg, gains, qscale):
    m = czg.shape[0]
    col = lambda c0, w: pl.BlockSpec((m, w), lambda i: (0, c0 // w))
    gain = pl.BlockSpec((1, HEAD_DIM), lambda i: (0, 0))
    full = lambda shape: pl.BlockSpec(shape, lambda i: (0,) * len(shape))
    return pl.pallas_call(
        functools.partial(_prep_ctx_kernel, qscale=qscale),
        out_shape=(jax.ShapeDtypeStruct((m, NA_W), BF16),
                   jax.ShapeDtypeStruct((NA_HEADS, HEAD_DIM, m), BF16),
                   jax.ShapeDtypeStruct((NA_HEADS, m, HEAD_DIM), BF16),
                   jax.ShapeDtypeStruct((NA_HEADS, HEAD_DIM, m), BF16),
                   jax.ShapeDtypeStruct((GQA_Q_HEADS, HEAD_DIM, m), BF16),
                   jax.ShapeDtypeStruct((GQA_KV_HEADS, m, HEAD_DIM), BF16),
                   jax.ShapeDtypeStruct((GQA_KV_HEADS, HEAD_DIM, m), BF16)),
        grid=(1,),
        in_specs=[col(_C_NQ, NA_W), col(_C_NK, NA_W), col(_C_NV, NA_W), col(_C_ZQ, GQA_W // 2),
                  col(_C_ZQ + GQA_W // 2, GQA_W // 2), col(_C_ZK, GQA_KV_W), col(_C_ZV, GQA_KV_W),
                  gain, gain, gain, gain],
        out_specs=(full((m, NA_W)),
                   full((NA_HEADS, HEAD_DIM, m)), full((NA_HEADS, m, HEAD_DIM)), full((NA_HEADS, HEAD_DIM, m)),
                   full((GQA_Q_HEADS, HEAD_DIM, m)), full((GQA_KV_HEADS, m, HEAD_DIM)),
                   full((GQA_KV_HEADS, HEAD_DIM, m))),
        compiler_params=_cp("arbitrary"),
        name="prep_ctx",
    )(czg, czg, czg, czg, czg, czg, czg, *gains)


SCORE_BOUND_SAFE = 60.0
FLASH_TQ = 512
FLASH_UNITS = 64


def _flash_kernel(*refs, group, tk, has_extra, n_cast):
    n_in = 5 if has_extra else 3
    qT_ref, k_ref, vT_ref = refs[:3]
    k2_ref, vT2_ref = refs[3:5] if has_extra else (None, None)
    o_ref = refs[n_in + n_cast]
    kmax_ref, acc_ref = refs[-2:]
    tq = min(qT_ref.shape[2], FLASH_TQ)
    nsub = qT_ref.shape[2] // tq
    cols = [(h, sub) for h in range(group) for sub in range(nsub)]
    n_keys = k_ref.shape[1]
    n_full = n_keys // tk
    rem = n_keys - n_full * tk
    unroll = max(1, FLASH_UNITS // len(cols))
    n_body = n_full // unroll

    cast_pairs = list(zip(refs[n_in:n_in + n_cast], refs[n_in + n_cast + 1:n_in + 2 * n_cast + 1]))
    in_loop = [p for p in cast_pairs if n_body > 0 and p[0].shape[0] % (n_body * HALO) == 0]
    for w_ref, wb_ref in cast_pairs:
        if not any(w_ref is q[0] for q in in_loop):
            wb_ref[...] = w_ref[...].astype(BF16)

    def cast_rows(trip):
        for w_ref, wb_ref in in_loop:
            n = w_ref.shape[0] // n_body
            rows = pl.ds(trip * n if isinstance(trip, int) else pl.multiple_of(trip * n, n), n)
            wb_ref[rows, :] = w_ref[rows, :].astype(BF16)
    tails = ([(k_ref, vT_ref, n_full * tk, rem)] if rem > 0 else []) + (
        [(k2_ref, vT2_ref, 0, k2_ref.shape[1])] if has_extra else [])

    @pl.when(pl.program_id(1) == 0)
    def _():
        def ksq(kr, start, size):
            kb = kr[0, pl.ds(start, size), :].astype(F32)
            return jnp.max(jnp.sum(kb * kb, axis=-1, keepdims=True), axis=0, keepdims=True)

        mx = jnp.zeros((1, 1), F32)
        if n_full > 0:
            mx = lax.fori_loop(
                0, n_full, lambda i, c: jnp.maximum(c, ksq(k_ref, pl.multiple_of(i * tk, tk), tk)), mx)
        for kr, _, start, size in tails:
            mx = jnp.maximum(mx, ksq(kr, start, size))
        kmax_ref[...] = mx

    def run_blocks(step, carry):
        if n_full > 0:
            carry = lax.fori_loop(
                0, n_full, lambda kb, c: step(k_ref, vT_ref, pl.multiple_of(kb * tk, tk), tk, c), carry)
        for kr, vr, start, size in tails:
            carry = step(kr, vr, start, size, carry)
        return carry

    qTs = [qT_ref[h, :, sub * tq:(sub + 1) * tq] for h, sub in cols]
    shifts = []
    for qT in qTs:
        qf = qT.astype(F32)
        qsq = jnp.sum(qf * qf, axis=0, keepdims=True)
        shifts.append(jnp.sqrt(qsq * kmax_ref[...]) * 1.01)
    safe = jnp.max(functools.reduce(jnp.maximum, shifts)) <= SCORE_BOUND_SAFE

    @pl.when(safe)
    def _():
        def step(kr, vr, start, size, l8s, nb=1):
            units = [(j, c) for j in range(nb) for c in range(len(cols))]
            kblks = [kr[0, pl.ds(start + j * size, size), :] for j in range(nb)]
            vblks = [vr[0, :, pl.ds(start + j * size, size)] for j in range(nb)]
            l8s = list(l8s)
            s_next = jnp.dot(kblks[0], qTs[0], preferred_element_type=F32)
            for u, (j, c) in enumerate(units):
                s = s_next
                if u + 1 < len(units):
                    jn, cn = units[u + 1]
                    s_next = jnp.dot(kblks[jn], qTs[cn], preferred_element_type=F32)
                p = jnp.exp2(s - shifts[c])
                l8s[c] = l8s[c] + jnp.sum(p.reshape(size // 8, 8, tq), axis=0)
                acc_ref[c] += jnp.dot(vblks[j], p.astype(BF16), preferred_element_type=F32)
            return tuple(l8s)

        acc_ref[...] = jnp.zeros_like(acc_ref)
        l8s = tuple(jnp.zeros((8, tq), F32) for _ in cols)

        def body(kb, c):
            cast_rows(kb)
            return step(k_ref, vT_ref, pl.multiple_of(kb * (tk * unroll), tk * unroll), tk, c, nb=unroll)

        if n_body > 0:
            l8s = lax.fori_loop(0, n_body, body, l8s)
        for kb in range(n_body * unroll, n_full):
            l8s = step(k_ref, vT_ref, kb * tk, tk, l8s)
        for kr, vr, start, size in tails:
            l8s = step(kr, vr, start, size, l8s)
        for c in range(len(cols)):
            l = jnp.sum(l8s[c], axis=0, keepdims=True)
            acc_ref[c] = acc_ref[c] * (1.0 / l)

    @pl.when(jnp.logical_not(safe))
    def _():
        for trip in range(n_body if in_loop else 0):
            cast_rows(trip)
        for c in range(len(cols)):
            def step(kr, vr, start, size, carry, qT=qTs[c]):
                m, l, acc = carry
                kblk = kr[0, pl.ds(start, size), :]
                s = jnp.dot(kblk, qT, preferred_element_type=F32)
                m_new = jnp.maximum(m, jnp.max(s, axis=0, keepdims=True))
                alpha = jnp.exp2(m - m_new)
                p = jnp.exp2(s - m_new)
                l = alpha * l + jnp.sum(p, axis=0, keepdims=True)
                vblk = vr[0, :, pl.ds(start, size)]
                acc = alpha * acc + jnp.dot(vblk, p.astype(BF16), preferred_element_type=F32)
                return m_new, l, acc

            carry = (jnp.full((1, tq), NEG_BIG, F32), jnp.zeros((1, tq), F32),
                     jnp.zeros((HEAD_DIM, tq), F32))
            _, l, acc = run_blocks(step, carry)
            acc_ref[c] = acc * (1.0 / l)

    for c, (h, sub) in enumerate(cols):
        o_ref[sub * tq:(sub + 1) * tq, h * HEAD_DIM:(h + 1) * HEAD_DIM] = acc_ref[c].T.astype(o_ref.dtype)


def flash_attention(qT, k, vT, tq, extra=None, casts=()):
    hq, _, mq = qT.shape
    hkv = k.shape[0]
    group = hq // hkv
    nq = mq // tq
    kv_specs = lambda kk, vv: [
        pl.BlockSpec((1, kk.shape[1], HEAD_DIM), lambda g, i: (g, 0, 0), pipeline_mode=pl.Buffered(1)),
        pl.BlockSpec((1, HEAD_DIM, vv.shape[2]), lambda g, i: (g, 0, 0), pipeline_mode=pl.Buffered(1))]
    args, specs = [k, vT], kv_specs(k, vT)
    if extra is not None:
        args += list(extra)
        specs += kv_specs(*extra)
    out_shapes = [jax.ShapeDtypeStruct((mq, hq * HEAD_DIM), BF16)]
    out_specs = [pl.BlockSpec((tq, group * HEAD_DIM), lambda g, i: (i, g))]
    for w, layer in casts:
        _, kk, nn = w.shape
        rb = kk // (hkv * nq)
        args.append(w)
        specs.append(pl.BlockSpec((None, rb, nn), functools.partial(
            lambda g, i, layer: (layer, g * nq + i, 0), layer=layer)))
        out_shapes.append(jax.ShapeDtypeStruct((1, kk, nn), BF16))
        out_specs.append(pl.BlockSpec((None, rb, nn), lambda g, i: (0, g * nq + i, 0)))
    outs = pl.pallas_call(
        functools.partial(_flash_kernel, group=group, tk=512, has_extra=extra is not None, n_cast=len(casts)),
        out_shape=out_shapes,
        grid=(hkv, nq),
        in_specs=[pl.BlockSpec((group, HEAD_DIM, tq), lambda g, i: (g, 0, i))] + specs,
        out_specs=out_specs,
        scratch_shapes=[pltpu.VMEM((1, 1), F32),
                        pltpu.VMEM((group * max(1, tq // FLASH_TQ), HEAD_DIM, min(tq, FLASH_TQ)), F32)],
        compiler_params=_cp("parallel", "arbitrary"),
        name="flash_attention",
    )(qT, *args)
    return outs[0], list(outs[1:])


NA_RB = 8
NA_TOK = NA_RB * GRID_W
NA_PAIR = 2 * GRID_W
NA_SLAB_ROWS = NA_ROWS + 2
NA_SLAB = NA_SLAB_ROWS * GRID_W
NA_PATTERNS = 5


def _na_kernel(q_ref, kp, kc, kn, vp, vc, vn, ck_ref, cv_ref, bias_ref, o_ref, kbuf, vbuf):
    b = pl.program_id(0)
    rows = pl.num_programs(0) * NA_RB
    kbuf[0:NA_TOK, :] = kp[...]
    kbuf[NA_TOK:2 * NA_TOK, :] = kc[...]
    kbuf[2 * NA_TOK:3 * NA_TOK, :] = kn[...]
    vbuf[0:NA_TOK, :] = vp[...]
    vbuf[NA_TOK:2 * NA_TOK, :] = vc[...]
    vbuf[2 * NA_TOK:3 * NA_TOK, :] = vn[...]
    nt = (((1,), (1,)), ((), ()))

    def slab(pair):
        r = b * NA_RB + 2 * pair
        u0 = jnp.clip(r - NA_ROWS // 2, 0, rows - NA_ROWS)
        off = pl.multiple_of((u0 - b * NA_RB + NA_RB) * GRID_W, NA_PAIR)
        pat = jnp.where(r < NA_ROWS // 2, r // 2,
                        jnp.where(r >= rows - NA_ROWS // 2, 3 + (r - (rows - NA_ROWS // 2)) // 2, 2))
        return off, pat

    slabs = [slab(pair) for pair in range(NA_RB // 2)]
    units = [(pair, h) for pair in range(NA_RB // 2) for h in range(NA_HEADS)]

    def scores(pair, h):
        off, pat = slabs[pair]
        sl = slice(h * HEAD_DIM, (h + 1) * HEAD_DIM)
        qh = q_ref[pair * NA_PAIR:(pair + 1) * NA_PAIR, sl]
        s = lax.dot_general(qh, kbuf[pl.ds(off, NA_SLAB), sl], nt, preferred_element_type=F32)
        sc = lax.dot_general(qh, ck_ref[:, sl], nt, preferred_element_type=F32)
        return s + bias_ref[pat, h], sc

    nxt = scores(*units[0])
    for u, (pair, h) in enumerate(units):
        s, sc = nxt
        if u + 1 < len(units):
            nxt = scores(*units[u + 1])
        off, _ = slabs[pair]
        sl = slice(h * HEAD_DIM, (h + 1) * HEAD_DIM)
        m = jnp.maximum(jnp.max(s, axis=-1, keepdims=True), jnp.max(sc, axis=-1, keepdims=True))
        p = jnp.exp2(s - m)
        pc = jnp.exp2(sc - m)
        l = jnp.sum(p, axis=-1, keepdims=True) + jnp.sum(pc, axis=-1, keepdims=True)
        o = (jnp.dot(p.astype(BF16), vbuf[pl.ds(off, NA_SLAB), sl], preferred_element_type=F32)
             + jnp.dot(pc.astype(BF16), cv_ref[:, sl], preferred_element_type=F32))
        o_ref[pair * NA_PAIR:(pair + 1) * NA_PAIR, sl] = (o * (1.0 / l)).astype(o_ref.dtype)


def neighborhood_attention(qn, kn, zg, ckn, czg, bias):
    m = qn.shape[0]
    nb = m // NA_TOK
    vcol = _C_NV // NA_W
    blk = (NA_TOK, NA_W)
    return pl.pallas_call(
        _na_kernel,
        out_shape=jax.ShapeDtypeStruct((m, NA_W), BF16),
        grid=(nb,),
        in_specs=[pl.BlockSpec(blk, lambda b: (b, 0)),
                  pl.BlockSpec(blk, lambda b: (jnp.maximum(b - 1, 0), 0)),
                  pl.BlockSpec(blk, lambda b: (b, 0)),
                  pl.BlockSpec(blk, lambda b: (jnp.minimum(b + 1, nb - 1), 0)),
                  pl.BlockSpec(blk, lambda b: (jnp.maximum(b - 1, 0), vcol)),
                  pl.BlockSpec(blk, lambda b: (b, vcol)),
                  pl.BlockSpec(blk, lambda b: (jnp.minimum(b + 1, nb - 1), vcol)),
                  pl.BlockSpec((CTX_LEN, NA_W), lambda b: (0, 0)),
                  pl.BlockSpec((CTX_LEN, NA_W), lambda b: (0, vcol)),
                  pl.BlockSpec((NA_PATTERNS, NA_HEADS, NA_PAIR, NA_SLAB), lambda b: (0, 0, 0, 0))],
        out_specs=pl.BlockSpec(blk, lambda b: (b, 0)),
        scratch_shapes=[pltpu.VMEM((3 * NA_TOK, NA_W), BF16), pltpu.VMEM((3 * NA_TOK, NA_W), BF16)],
        compiler_params=_cp("arbitrary"),
        name="neighborhood_attention",
    )(qn, kn, kn, kn, zg, zg, zg, ckn, czg, bias)


def na_bias_table(rpb):
    col = np.arange(GRID_W)
    c0 = np.clip(col - NA_COLS // 2, 0, GRID_W - NA_COLS)
    kc = np.arange(GRID_W)
    inside = (kc[None, :] >= c0[:, None]) & (kc[None, :] < c0[:, None] + NA_COLS)
    dc = kc[None, :] - col[:, None] + NA_COLS - 1
    onehot = (dc[None] == np.arange(2 * NA_COLS - 1)[:, None, None]) & inside[None]
    t = jnp.einsum('hrd,dck->hrck', rpb.astype(F32) * LOG2E, jnp.asarray(onehot, F32),
                   precision=lax.Precision.HIGHEST)
    t = jnp.where(inside[None, None], t, NEG_BIG)
    masked = jnp.full((NA_HEADS, GRID_W, GRID_W), NEG_BIG, F32)
    patterns = [(0, 0), (2, 0), (4, 1), (4, 0), (6, 0)]
    tabs = []
    for a, e in patterns:
        per_q = []
        for start, dr0 in ((0, NA_ROWS - 1 - a), (e, NA_ROWS - 2 - a)):
            blocks = [t[:, w + dr0] if start <= w < start + NA_ROWS else masked for w in range(NA_SLAB_ROWS)]
            per_q.append(jnp.stack(blocks, axis=2).reshape(NA_HEADS, GRID_W, NA_SLAB))
        tabs.append(jnp.concatenate(per_q, axis=1))
    return jnp.stack(tabs, axis=0)


def _dft_mats(n):
    a = 2.0 * np.pi * np.outer(np.arange(n), np.arange(n)) / n
    return np.cos(a), np.sin(a)


def _chan_dft_kernel(z_ref, w_ref, o_ref):
    for g in range(FOURIER_GROUPS):
        x = z_ref[:, g * HEAD_DIM:(g + 1) * HEAD_DIM]
        r = jnp.dot(x, w_ref[...], preferred_element_type=F32)
        o_ref[0, :, g * HEAD_DIM:(g + 1) * HEAD_DIM] = r[:, :HEAD_DIM].astype(o_ref.dtype)
        o_ref[1, :, g * HEAD_DIM:(g + 1) * HEAD_DIM] = r[:, HEAD_DIM:].astype(o_ref.dtype)


def chan_dft(zg, wch):
    m = zg.shape[0]
    tm = min(m, 1024)
    return pl.pallas_call(
        _chan_dft_kernel,
        out_shape=jax.ShapeDtypeStruct((2, m, FOURIER_W), BF16),
        grid=(m // tm,),
        in_specs=[pl.BlockSpec((tm, FOURIER_W), lambda i: (i, _C_ZF // FOURIER_W)),
                  pl.BlockSpec((HEAD_DIM, 2 * HEAD_DIM), lambda i: (0, 0))],
        out_specs=pl.BlockSpec((2, tm, FOURIER_W), lambda i: (0, i, 0)),
        compiler_params=_cp("parallel"),
        name="chan_dft",
    )(zg, wch)


def _left_mm_kernel(l_ref, x_ref, o_ref, *, n_out):
    x = jnp.concatenate([x_ref[0], x_ref[1]], axis=0)
    r = jnp.dot(l_ref[...], x, preferred_element_type=F32)
    if n_out == 1:
        o_ref[...] = r.astype(o_ref.dtype)
    else:
        half = r.shape[0] // 2
        o_ref[0] = r[:half].astype(o_ref.dtype)
        o_ref[1] = r[half:].astype(o_ref.dtype)


def dft_stage1(zc, lmat):
    _, r, n = zc.shape
    tn = 4096
    return pl.pallas_call(
        functools.partial(_left_mm_kernel, n_out=2),
        out_shape=jax.ShapeDtypeStruct((2, r, n), BF16),
        grid=(n // tn,),
        in_specs=[pl.BlockSpec((2 * r, 2 * r), lambda j: (0, 0)),
                  pl.BlockSpec((2, r, tn), lambda j: (0, 0, j))],
        out_specs=pl.BlockSpec((2, r, tn), lambda j: (0, 0, j)),
        compiler_params=_cp("parallel"),
        name="dft_stage1",
    )(lmat, zc)


def _dft_stage3_kernel(l_ref, x_ref, o_ref, *, kb):
    for i in range(kb):
        x = jnp.concatenate([x_ref[0, i], x_ref[1, i]], axis=0)
        o_ref[i] = jnp.dot(l_ref[i], x, preferred_element_type=F32).astype(o_ref.dtype)


def dft_stage3(a, tables):
    _, n1, n2, c = a.shape
    kb = 8
    return pl.pallas_call(
        functools.partial(_dft_stage3_kernel, kb=kb),
        out_shape=jax.ShapeDtypeStruct((n1, n2, c), BF16),
        grid=(n1 // kb,),
        in_specs=[pl.BlockSpec((kb, n2, 2 * n2), lambda j: (j, 0, 0)),
                  pl.BlockSpec((2, kb, n2, c), lambda j: (0, j, 0, 0))],
        out_specs=pl.BlockSpec((kb, n2, c), lambda j: (j, 0, 0)),
        compiler_params=_cp("parallel"),
        name="dft_stage3",
    )(tables, a)


def dft_direct(zc, lmat):
    _, t, c = zc.shape
    return pl.pallas_call(
        functools.partial(_left_mm_kernel, n_out=1),
        out_shape=jax.ShapeDtypeStruct((t, c), BF16),
        grid=(1,),
        in_specs=[pl.BlockSpec((t, 2 * t), lambda j: (0, 0)),
                  pl.BlockSpec((2, t, c), lambda j: (0, 0, 0))],
        out_specs=pl.BlockSpec((t, c), lambda j: (0, 0)),
        compiler_params=_cp("arbitrary"),
        name="dft_direct",
    )(lmat, zc)


def _fourier_tables():
    c128, s128 = _dft_mats(HEAD_DIM)
    wch = np.concatenate([c128, -s128], axis=1)
    n1 = SEQ // HEAD_DIM
    l1 = np.block([[c128, s128], [-s128, c128]])
    k1 = np.arange(n1)[:, None, None]
    k2 = np.arange(n1)[None, :, None]
    t2 = np.arange(n1)[None, None, :]
    ang = 2.0 * np.pi * ((n1 * k2 + k1) * t2 % SEQ) / SEQ
    norm = 1.0 / math.sqrt(SEQ * HEAD_DIM)
    l3 = np.concatenate([np.cos(ang), np.sin(ang)], axis=2) * norm
    cc, sc = _dft_mats(CTX_LEN)
    lc = np.concatenate([cc, sc], axis=1) / math.sqrt(CTX_LEN * HEAD_DIM)
    as_bf = lambda a: jnp.asarray(a, F32).astype(BF16)
    return as_bf(wch), as_bf(l1), as_bf(l3), as_bf(lc)


def fourier_latent(zg, tabs):
    wch, l1, l3, _ = tabs
    n1 = SEQ // HEAD_DIM
    zc = chan_dft(zg, wch)
    a = dft_stage1(zc.reshape(2, n1, n1 * FOURIER_W), l1)
    o3 = dft_stage3(a.reshape(2, n1, n1, FOURIER_W), l3)
    return o3.transpose(1, 0, 2).reshape(SEQ, FOURIER_W)


def fourier_ctx(czg, tabs):
    wch, _, _, lc = tabs
    return dft_direct(chan_dft(czg, wch), lc)


def _rope_tables(n_tok):
    t = jnp.arange(n_tok)
    row = (t // GRID_W).astype(F32)
    col = (t % GRID_W).astype(F32)
    n_freq = HEAD_DIM // 4
    inv_freq = ROPE_THETA ** (-jnp.arange(n_freq, dtype=F32) / n_freq)
    ang = jnp.concatenate([row[:, None] * inv_freq, col[:, None] * inv_freq], axis=-1)
    cos, sin = jnp.cos(ang), jnp.sin(ang)
    return jnp.concatenate([cos, cos], axis=-1), jnp.concatenate([-sin, sin], axis=-1)


def kernel(x, c, ctx, c_ctx, w_ada, b_ada, norm1, w_in, conv_w, na_q_gain, na_k_gain, na_rpb, gqa_q_gain, gqa_k_gain, w_conv_out, w_fourier_out, w_na_out, w_gqa_out, w_gate, b_gate, w_o, norm2, w_up, ffn_conv_w, w_down):
    d = D_MODEL
    h = x[0]
    hc = ctx[0]
    rope = _rope_tables(SEQ)
    ftabs = _fourier_tables()
    qscale2 = HEAD_DIM ** -0.5 * LOG2E
    cvec = jnp.zeros((8, d), F32).at[0].set(c[0]).at[1].set(c_ctx)
    row = lambda v: v.reshape(1, -1)

    w_in_b, w_gate_b = w_in[0:1].astype(BF16), w_gate[0:1].astype(BF16)
    late_w = (w_conv_out, w_fourier_out, w_na_out, w_gqa_out, w_o, w_up, w_down)
    b_ada3 = b_ada.reshape(DEPTH, 1, -1)
    b_gate3 = b_gate.reshape(DEPTH, 1, -1)

    for i in range(DEPTH):
        last = i == DEPTH - 1
        bias_tab = na_bias_table(na_rpb[i])
        gains = (row(na_q_gain[i]), row(na_k_gain[i]), row(gqa_q_gain[i]), row(gqa_k_gain[i]))

        mods = ada_mod(cvec, w_ada, b_ada3, i)
        lat = [mods[0:1, k * d:(k + 1) * d] for k in range(6)]
        cm = [mods[1:2, k * d:(k + 1) * d] for k in range(6)]

        zg = in_proj(h, row(norm1[i]), lat[0], lat[1], w_in_b, w_gate_b, b_gate3, i, "in_proj")
        czg = in_proj(hc, row(norm1[i]), cm[0], cm[1], w_in_b, w_gate_b, b_gate3, i, "in_proj_ctx")

        cnk_rows, cnqT, cnk, cnvT, cqT, ck, cvT = prep_ctx(czg, gains, qscale2)
        nq, nk, gqT, gk, gvT = prep_latent(zg, gains, rope, qscale2)

        y_conv = conv_mixer(zg, conv_w, i)
        y_four = fourier_latent(zg, ftabs)
        y_na = neighborhood_attention(nq, nk, zg, cnk_rows, czg, bias_tab)
        casts = [(w, i) for w in late_w] + ([] if last else [(w_in, i + 1), (w_gate, i + 1)])
        y_gqa, wb = flash_attention(gqT, gk, gvT, tq=1024, extra=(ck, cvT), casts=casts)
        merge_w, w_o_b, w_up_b, w_down_b = wb[0:4], wb[4], wb[5], wb[6]

        merged = merge_branches([y_conv, y_four, y_na, y_gqa], merge_w, zg)
        h = mm_residual(merged, w_o_b, h, lat[2], "out_proj", in_place=i > 0)
        act = ffn_up_act(h, row(norm2[i]), lat[3], lat[4], w_up_b, ffn_conv_w, i, "ffn_up_act")
        h = mm_residual(act, w_down_b, h, lat[5], "ffn_down", in_place=True)

        if not last:
            cy_conv = conv_mixer(czg, conv_w, i)
            cy_four = fourier_ctx(czg, ftabs)
            cy_na, _ = flash_attention(cnqT, cnk, cnvT, tq=CTX_LEN)
            cy_gqa, _ = flash_attention(cqT, ck, cvT, tq=CTX_LEN)
            cmerged = merge_branches([cy_conv, cy_four, cy_na, cy_gqa], merge_w, czg)
            hc = mm_residual(cmerged, w_o_b, hc, cm[2], "out_proj_ctx", in_place=i > 0)
            cact = ffn_up_act(hc, row(norm2[i]), cm[3], cm[4], w_up_b, ffn_conv_w, i, "ffn_up_act_ctx")
            hc = mm_residual(cact, w_down_b, hc, cm[5], "ffn_down_ctx", in_place=True)
            w_in_b, w_gate_b = wb[7], wb[8]

    return h[None]
```

```python
import functools
import math

import jax
import jax.numpy as jnp
import numpy as np
from jax import lax
from jax.experimental import pallas as pl
from jax.experimental.pallas import tpu as pltpu

D_MODEL = 2048
SEQ = 16384
DEPTH = 4
GRID_W = 64
CTX_LEN = 256
HEAD_DIM = 128
CONV_W = 512
FOURIER_GROUPS = 4
FOURIER_W = 512
NA_HEADS = 4
NA_W = 512
NA_ROWS = 8
NA_COLS = 16
GQA_Q_HEADS = 8
GQA_KV_HEADS = 2
GQA_GROUP = 4
GQA_W = 1024
GQA_KV_W = 256
N_BRANCH = 4
N_IN = 5120
D_FF = 5632
ROPE_THETA = 10000.0
EPS = 1e-6

_C_XA, _C_BG, _C_CG = 0, 512, 1024
_C_ZF = 1536
_C_NQ, _C_NK, _C_NV = 2048, 2560, 3072
_C_ZQ, _C_ZK, _C_ZV = 3584, 4608, 4864
_C_GATE = N_IN

BF16 = jnp.bfloat16
F32 = jnp.float32
V7X_VMEM_LIMIT = 52 * 1024 * 1024
NEG_BIG = -1e30
HALO = 16
NORM_CHUNK = 256
LOG2E = 1.4426950408889634


def _cp(*sem):
    return pltpu.CompilerParams(dimension_semantics=sem, vmem_limit_bytes=V7X_VMEM_LIMIT)


def _sigmoid(t):
    return 0.5 * jnp.tanh(0.5 * t) + 0.5


def _norm_mod(x, g, sh, sc):
    ms = jnp.mean(x * x, axis=-1, keepdims=True)
    return (x * lax.rsqrt(ms + EPS) * g) * (1.0 + sc) + sh


def _ada_kernel(c_ref, w_ref, b_ref, o_ref):
    c = c_ref[...]
    s = c * _sigmoid(c)
    o_ref[...] = jnp.dot(s.astype(BF16), w_ref[...].astype(BF16),
                         preferred_element_type=F32) + b_ref[...]


def ada_mod(cvec, w, b, layer):
    n = w.shape[2]
    tn = 1024
    return pl.pallas_call(
        _ada_kernel,
        out_shape=jax.ShapeDtypeStruct((8, n), F32),
        grid=(n // tn,),
        in_specs=[pl.BlockSpec((8, D_MODEL), lambda j: (0, 0)),
                  pl.BlockSpec((None, D_MODEL, tn), lambda j: (layer, 0, j)),
                  pl.BlockSpec((None, 1, tn), lambda j: (layer, 0, j))],
        out_specs=pl.BlockSpec((8, tn), lambda j: (0, j)),
        compiler_params=_cp("parallel"),
        name="ada_mod",
    )(cvec, w, b)


def _in_proj_kernel(h_ref, g_ref, sh_ref, sc_ref, win_ref, wgate_ref, b_ref, o_ref, xn_ref, *, n_plain):
    j = pl.program_id(1)
    tm = h_ref.shape[0]
    cm = min(tm, NORM_CHUNK)

    @pl.when(j == 0)
    def _():
        for c in range(tm // cm):
            rows = slice(c * cm, (c + 1) * cm)
            xc = _norm_mod(h_ref[rows, :], g_ref[...], sh_ref[...], sc_ref[...]).astype(BF16)
            xn_ref[rows, :] = xc
            o_ref[rows, :] = jnp.dot(xc, win_ref[...], preferred_element_type=F32).astype(o_ref.dtype)

    @pl.when(jnp.logical_and(j > 0, j < n_plain))
    def _():
        o_ref[...] = jnp.dot(xn_ref[...], win_ref[...], preferred_element_type=F32).astype(o_ref.dtype)

    @pl.when(j >= n_plain)
    def _():
        t = jnp.dot(xn_ref[...], wgate_ref[...], preferred_element_type=F32) + b_ref[...]
        o_ref[...] = _sigmoid(t).astype(o_ref.dtype)


def in_proj(h, gain, shift, scale, w_in, w_gate, b_gate, layer, name):
    m, d = h.shape
    n_in, n_gate = w_in.shape[2], w_gate.shape[2]
    tm = min(m, 1024)
    tn = 1024
    n_plain = n_in // tn
    vec = pl.BlockSpec((1, d), lambda i, j: (0, 0))
    return pl.pallas_call(
        functools.partial(_in_proj_kernel, n_plain=n_plain),
        out_shape=jax.ShapeDtypeStruct((m, n_in + n_gate), BF16),
        grid=(m // tm, (n_in + n_gate) // tn),
        in_specs=[pl.BlockSpec((tm, d), lambda i, j: (i, 0)), vec, vec, vec,
                  pl.BlockSpec((None, d, tn), lambda i, j: (0, 0, jnp.minimum(j, n_plain - 1))),
                  pl.BlockSpec((None, d, tn), lambda i, j: (0, 0, jnp.maximum(j - n_plain, 0))),
                  pl.BlockSpec((None, 1, tn), lambda i, j: (layer, 0, jnp.maximum(j - n_plain, 0)))],
        out_specs=pl.BlockSpec((tm, tn), lambda i, j: (i, j)),
        scratch_shapes=[pltpu.VMEM((tm, d), BF16)],
        compiler_params=_cp("parallel", "arbitrary"),
        name=name,
    )(h, gain, shift, scale, w_in, w_gate, b_gate)


def _mm_res_kernel(x_ref, w_ref, h_ref, g_ref, o_ref):
    acc = jnp.dot(x_ref[...], w_ref[...], preferred_element_type=F32)
    o_ref[...] = h_ref[...] + g_ref[...] * acc


def mm_residual(x, w, h, gate, name, in_place):
    m, k = x.shape
    n = w.shape[2]
    if k <= D_MODEL:
        tm, tn = min(m, 512), n
    else:
        tm, tn = min(m, 1024), 512
    return pl.pallas_call(
        _mm_res_kernel,
        out_shape=jax.ShapeDtypeStruct((m, n), F32),
        grid=(m // tm, n // tn),
        in_specs=[pl.BlockSpec((tm, k), lambda i, j: (i, 0)),
                  pl.BlockSpec((None, k, tn), lambda i, j: (0, 0, j)),
                  pl.BlockSpec((tm, tn), lambda i, j: (i, j)),
                  pl.BlockSpec((1, tn), lambda i, j: (0, j))],
        out_specs=pl.BlockSpec((tm, tn), lambda i, j: (i, j)),
        input_output_aliases={2: 0} if in_place else {},
        compiler_params=_cp("parallel", "arbitrary"),
        name=name,
    )(x, w, h, gate)


def _merge_kernel(yc, yf, yn, yg, wc, wf, wn, wg, g0, g1, g2, g3, o_ref):
    def term(y, w, g):
        return g[...].astype(F32) * jnp.dot(y[...], w[...], preferred_element_type=F32)

    o_ref[...] = (term(yc, wc, g0) + term(yf, wf, g1) + term(yn, wn, g2)
                  + term(yg, wg, g3)).astype(o_ref.dtype)


def merge_branches(ys, ws, zg):
    m = zg.shape[0]
    tm = min(m, 1024)
    tn = 1024
    y_specs = [pl.BlockSpec((tm, y.shape[1]), lambda i, j: (i, 0)) for y in ys]
    w_specs = [pl.BlockSpec((None, w.shape[1], tn), lambda i, j: (0, 0, j)) for w in ws]
    g_specs = [pl.BlockSpec((tm, tn), functools.partial(
        lambda i, j, b: (i, (_C_GATE + b * D_MODEL) // tn + j), b=b)) for b in range(N_BRANCH)]
    return pl.pallas_call(
        _merge_kernel,
        out_shape=jax.ShapeDtypeStruct((m, D_MODEL), BF16),
        grid=(m // tm, D_MODEL // tn),
        in_specs=y_specs + w_specs + g_specs,
        out_specs=pl.BlockSpec((tm, tn), lambda i, j: (i, j)),
        compiler_params=_cp("parallel", "arbitrary"),
        name="merge",
    )(*ys, *ws, zg, zg, zg, zg)


def _conv3(p, prev_row, next_row, w_ref):
    tm = p.shape[0]
    row = lax.broadcasted_iota(jnp.int32, p.shape, 0)
    up = jnp.where(row == 0, prev_row, pltpu.roll(p, 1, 0))
    dn = jnp.where(row == tm - 1, next_row, pltpu.roll(p, tm - 1, 0))
    return up * w_ref[0:1, :] + p * w_ref[1:2, :] + dn * w_ref[2:3, :]


def _halo_specs(tm, tc, m, col):
    nb = m // HALO
    per = tm // HALO
    main = pl.BlockSpec((tm, tc), lambda i: (i, col))
    prev = pl.BlockSpec((HALO, tc), lambda i: (jnp.maximum(i * per - 1, 0), col))
    nxt = pl.BlockSpec((HALO, tc), lambda i: (jnp.minimum((i + 1) * per, nb - 1), col))
    return main, prev, nxt


def _convmix_kernel(xa, xap, xan, cg, cgp, cgn, bg, w_ref, o_ref):
    i = pl.program_id(0)
    last = pl.num_programs(0) - 1
    p = cg[...].astype(F32) * xa[...].astype(F32)
    pp = cgp[...].astype(F32)[HALO - 1:HALO, :] * xap[...].astype(F32)[HALO - 1:HALO, :]
    pn = cgn[...].astype(F32)[0:1, :] * xan[...].astype(F32)[0:1, :]
    pp = jnp.where(i > 0, pp, 0.0)
    pn = jnp.where(i < last, pn, 0.0)
    o_ref[...] = (bg[...].astype(F32) * _conv3(p, pp, pn, w_ref)).astype(o_ref.dtype)


def conv_mixer(zg, w, layer):
    m = zg.shape[0]
    tm = min(m, 1024)
    tc = CONV_W
    xa = _halo_specs(tm, tc, m, _C_XA // tc)
    cg = _halo_specs(tm, tc, m, _C_CG // tc)
    bg = pl.BlockSpec((tm, tc), lambda i: (i, _C_BG // tc))
    return pl.pallas_call(
        _convmix_kernel,
        out_shape=jax.ShapeDtypeStruct((m, CONV_W), BF16),
        grid=(m // tm,),
        in_specs=[*xa, *cg, bg, pl.BlockSpec((None, 3, tc), lambda i: (layer, 0, 0))],
        out_specs=pl.BlockSpec((tm, tc), lambda i: (i, 0)),
        compiler_params=_cp("parallel"),
        name="conv_mixer",
    )(zg, zg, zg, zg, zg, zg, zg, w)


def _ffn_up_act_kernel(h_ref, hp_ref, hn_ref, g_ref, sh_ref, sc_ref, wa_ref, wg_ref, cwa_ref, cwg_ref,
                       o_ref, xn_ref):
    i = pl.program_id(0)
    j = pl.program_id(1)
    last = pl.num_programs(0) - 1
    tm = h_ref.shape[0]
    rows = tm + 2 * HALO

    cm = min(tm, NORM_CHUNK)
    nch = tm // cm

    def finish(ua, ug):
        def conv(u, cw_ref):
            c = (pltpu.roll(u, 1, 0) * cw_ref[0:1, :] + u * cw_ref[1:2, :]
                 + pltpu.roll(u, rows - 1, 0) * cw_ref[2:3, :])
            return c[HALO:HALO + tm]

        ca = conv(ua, cwa_ref)
        cgt = conv(ug, cwg_ref)
        o_ref[...] = (ca * _sigmoid(ca) * cgt).astype(o_ref.dtype)

    up = lambda x, w_ref: jnp.dot(x, w_ref[...], preferred_element_type=F32)

    @pl.when(j == 0)
    def _():
        nm = lambda x: _norm_mod(x, g_ref[...], sh_ref[...], sc_ref[...])
        xn_ref[0:HALO, :] = jnp.where(i > 0, nm(hp_ref[...]), 0.0).astype(BF16)
        uas, ugs = [], []
        for c in range(nch):
            lo, hi = c * cm, (c + 1) * cm
            xn_ref[HALO + lo:HALO + hi, :] = nm(h_ref[lo:hi, :]).astype(BF16)
            if c == nch - 1:
                xn_ref[HALO + tm:rows, :] = jnp.where(i < last, nm(hn_ref[...]), 0.0).astype(BF16)
            r0 = 0 if c == 0 else HALO + lo
            r1 = rows if c == nch - 1 else HALO + hi
            xc = xn_ref[r0:r1, :]
            uas.append(up(xc, wa_ref))
            ugs.append(up(xc, wg_ref))
        cat = lambda parts: parts[0] if len(parts) == 1 else jnp.concatenate(parts, axis=0)
        finish(cat(uas), cat(ugs))

    @pl.when(j > 0)
    def _():
        xn = xn_ref[...]
        finish(up(xn, wa_ref), up(xn, wg_ref))


def ffn_up_act(h, gain, shift, scale, w_up, conv_w, layer, name):
    m, d = h.shape
    tm = min(m, 1024)
    tn = 512
    nj = D_FF // tn
    nb = m // HALO
    per = tm // HALO
    vec = pl.BlockSpec((1, d), lambda i, j: (0, 0))
    return pl.pallas_call(
        _ffn_up_act_kernel,
        out_shape=jax.ShapeDtypeStruct((m, D_FF), BF16),
        grid=(m // tm, nj),
        in_specs=[pl.BlockSpec((tm, d), lambda i, j: (i, 0)),
                  pl.BlockSpec((HALO, d), lambda i, j: (jnp.maximum(i * per - 1, 0), 0)),
                  pl.BlockSpec((HALO, d), lambda i, j: (jnp.minimum((i + 1) * per, nb - 1), 0)),
                  vec, vec, vec,
                  pl.BlockSpec((None, d, tn), lambda i, j: (0, 0, j)),
                  pl.BlockSpec((None, d, tn), lambda i, j: (0, 0, nj + j)),
                  pl.BlockSpec((None, 3, tn), lambda i, j: (layer, 0, j)),
                  pl.BlockSpec((None, 3, tn), lambda i, j: (layer, 0, nj + j))],
        out_specs=pl.BlockSpec((tm, tn), lambda i, j: (i, j)),
        scratch_shapes=[pltpu.VMEM((tm + 2 * HALO, d), BF16)],
        compiler_params=_cp("parallel", "arbitrary"),
        name=name,
    )(h, h, h, gain, shift, scale, w_up, w_up, conv_w, conv_w)


def _head(z_ref, h, gain_ref=None, rope=None, scale=1.0):
    x = z_ref[:, h * HEAD_DIM:(h + 1) * HEAD_DIM].astype(F32)
    if gain_ref is not None:
        ms = jnp.mean(x * x, axis=-1, keepdims=True)
        x = x * lax.rsqrt(ms + EPS) * gain_ref[...]
    if rope is not None:
        x = x * rope[0][...] + pltpu.roll(x, HEAD_DIM // 2, 1) * rope[1][...]
    if scale != 1.0:
        x = x * scale
    return x


def _prep_latent_kernel(nq_ref, nk_ref, zqa_ref, zqb_ref, zk_ref, zv_ref, nqg, nkg, gqg, gkg, cos_ref, sin_ref,
                        nq_o, nk_o, gqT_o, k_o, vT_o, *, qscale):
    rope = (cos_ref, sin_ref)
    for h in range(NA_HEADS):
        sl = slice(h * HEAD_DIM, (h + 1) * HEAD_DIM)
        nq_o[:, sl] = _head(nq_ref, h, nqg, scale=qscale).astype(BF16)
        nk_o[:, sl] = _head(nk_ref, h, nkg).astype(BF16)
    for h in range(GQA_Q_HEADS):
        src = zqa_ref if h < GQA_Q_HEADS // 2 else zqb_ref
        gqT_o[h] = _head(src, h % (GQA_Q_HEADS // 2), gqg, rope, qscale).T.astype(BF16)
    for h in range(GQA_KV_HEADS):
        k_o[h] = _head(zk_ref, h, gkg, rope).astype(BF16)
        vT_o[h] = _head(zv_ref, h).T.astype(BF16)


def prep_latent(zg, gains, rope, qscale):
    m = zg.shape[0]
    tm = 1024
    col = lambda c0, w: pl.BlockSpec((tm, w), lambda i: (i, c0 // w))
    gain = pl.BlockSpec((1, HEAD_DIM), lambda i: (0, 0))
    tab = pl.BlockSpec((tm, HEAD_DIM), lambda i: (i, 0))
    return pl.pallas_call(
        functools.partial(_prep_latent_kernel, qscale=qscale),
        out_shape=(jax.ShapeDtypeStruct((m, NA_W), BF16), jax.ShapeDtypeStruct((m, NA_W), BF16),
                   jax.ShapeDtypeStruct((GQA_Q_HEADS, HEAD_DIM, m), BF16),
                   jax.ShapeDtypeStruct((GQA_KV_HEADS, m, HEAD_DIM), BF16),
                   jax.ShapeDtypeStruct((GQA_KV_HEADS, HEAD_DIM, m), BF16)),
        grid=(m // tm,),
        in_specs=[col(_C_NQ, NA_W), col(_C_NK, NA_W), col(_C_ZQ, GQA_W // 2), col(_C_ZQ + GQA_W // 2, GQA_W // 2),
                  col(_C_ZK, GQA_KV_W), col(_C_ZV, GQA_KV_W), gain, gain, gain, gain, tab, tab],
        out_specs=(pl.BlockSpec((tm, NA_W), lambda i: (i, 0)), pl.BlockSpec((tm, NA_W), lambda i: (i, 0)),
                   pl.BlockSpec((GQA_Q_HEADS, HEAD_DIM, tm), lambda i: (0, 0, i)),
                   pl.BlockSpec((GQA_KV_HEADS, tm, HEAD_DIM), lambda i: (0, i, 0)),
                   pl.BlockSpec((GQA_KV_HEADS, HEAD_DIM, tm), lambda i: (0, 0, i))),
        compiler_params=_cp("parallel"),
        name="prep_latent",
    )(zg, zg, zg, zg, zg, zg, *gains, *rope)


def _prep_ctx_kernel(nq_ref, nk_ref, nv_ref, zqa_ref, zqb_ref, zk_ref, zv_ref, nqg, nkg, gqg, gkg,
                     nk_o, cnqT_o, cnk_o, cnvT_o, cqT_o, ck_o, cvT_o, *, qscale):
    for h in range(NA_HEADS):
        sl = slice(h * HEAD_DIM, (h + 1) * HEAD_DIM)
        kn = _head(nk_ref, h, nkg).astype(BF16)
        nk_o[:, sl] = kn
        cnk_o[h] = kn
        cnqT_o[h] = _head(nq_ref, h, nqg, scale=qscale).T.astype(BF16)
        cnvT_o[h] = _head(nv_ref, h).T.astype(BF16)
    for h in range(GQA_Q_HEADS):
        src = zqa_ref if h < GQA_Q_HEADS // 2 else zqb_ref
        cqT_o[h] = _head(src, h % (GQA_Q_HEADS // 2), gqg, scale=qscale).T.astype(BF16)
    for h in range(GQA_KV_HEADS):
        ck_o[h] = _head(zk_ref, h, gkg).astype(BF16)
        cvT_o[h] = _head(zv_ref, h).T.astype(BF16)


def prep_ctx(czg, gains, qscale):
    m = czg.shape[0]
    col = lambda c0, w: pl.BlockSpec((m, w), lambda i: (0, c0 // w))
    gain = pl.BlockSpec((1, HEAD_DIM), lambda i: (0, 0))
    full = lambda shape: pl.BlockSpec(shape, lambda i: (0,) * len(shape))
    return pl.pallas_call(
        functools.partial(_prep_ctx_kernel, qscale=qscale),
        out_shape=(jax.ShapeDtypeStruct((m, NA_W), BF16),
                   jax.ShapeDtypeStruct((NA_HEADS, HEAD_DIM, m), BF16),
                   jax.ShapeDtypeStruct((NA_HEADS, m, HEAD_DIM), BF16),
                   jax.ShapeDtypeStruct((NA_HEADS, HEAD_DIM, m), BF16),
                   jax.ShapeDtypeStruct((GQA_Q_HEADS, HEAD_DIM, m), BF16),
                   jax.ShapeDtypeStruct((GQA_KV_HEADS, m, HEAD_DIM), BF16),
                   jax.ShapeDtypeStruct((GQA_KV_HEADS, HEAD_DIM, m), BF16)),
        grid=(1,),
        in_specs=[col(_C_NQ, NA_W), col(_C_NK, NA_W), col(_C_NV, NA_W), col(_C_ZQ, GQA_W // 2),
                  col(_C_ZQ + GQA_W // 2, GQA_W // 2), col(_C_ZK, GQA_KV_W), col(_C_ZV, GQA_KV_W),
                  gain, gain, gain, gain],
        out_specs=(full((m, NA_W)),
                   full((NA_HEADS, HEAD_DIM, m)), full((NA_HEADS, m, HEAD_DIM)), full((NA_HEADS, HEAD_DIM, m)),
                   full((GQA_Q_HEADS, HEAD_DIM, m)), full((GQA_KV_HEADS, m, HEAD_DIM)),
                   full((GQA_KV_HEADS, HEAD_DIM, m))),
        compiler_params=_cp("arbitrary"),
        name="prep_ctx",
    )(czg, czg, czg, czg, czg, czg, czg, *gains)


SCORE_BOUND_SAFE = 60.0
FLASH_TQ = 512
FLASH_UNITS = 64


def _flash_kernel(*refs, group, tk, has_extra, n_cast):
    n_in = 5 if has_extra else 3
    qT_ref, k_ref, vT_ref = refs[:3]
    k2_ref, vT2_ref = refs[3:5] if has_extra else (None, None)
    o_ref = refs[n_in + n_cast]
    kmax_ref, acc_ref = refs[-2:]
    tq = min(qT_ref.shape[2], FLASH_TQ)
    nsub = qT_ref.shape[2] // tq
    cols = [(h, sub) for h in range(group) for sub in range(nsub)]
    n_keys = k_ref.shape[1]
    n_full = n_keys // tk
    rem = n_keys - n_full * tk
    unroll = max(1, FLASH_UNITS // len(cols))
    n_body = n_full // unroll

    cast_pairs = list(zip(refs[n_in:n_in + n_cast], refs[n_in + n_cast + 1:n_in + 2 * n_cast + 1]))
    in_loop = [p for p in cast_pairs if n_body > 0 and p[0].shape[0] % (n_body * HALO) == 0]
    for w_ref, wb_ref in cast_pairs:
        if not any(w_ref is q[0] for q in in_loop):
            wb_ref[...] = w_ref[...].astype(BF16)

    def cast_rows(trip):
        for w_ref, wb_ref in in_loop:
            n = w_ref.shape[0] // n_body
            rows = pl.ds(trip * n if isinstance(trip, int) else pl.multiple_of(trip * n, n), n)
            wb_ref[rows, :] = w_ref[rows, :].astype(BF16)
    tails = ([(k_ref, vT_ref, n_full * tk, rem)] if rem > 0 else []) + (
        [(k2_ref, vT2_ref, 0, k2_ref.shape[1])] if has_extra else [])

    @pl.when(pl.program_id(1) == 0)
    def _():
        def ksq(kr, start, size):
            kb = kr[0, pl.ds(start, size), :].astype(F32)
            return jnp.max(jnp.sum(kb * kb, axis=-1, keepdims=True), axis=0, keepdims=True)

        mx = jnp.zeros((1, 1), F32)
        if n_full > 0:
            mx = lax.fori_loop(
                0, n_full, lambda i, c: jnp.maximum(c, ksq(k_ref, pl.multiple_of(i * tk, tk), tk)), mx)
        for kr, _, start, size in tails:
            mx = jnp.maximum(mx, ksq(kr, start, size))
        kmax_ref[...] = mx

    def run_blocks(step, carry):
        if n_full > 0:
            carry = lax.fori_loop(
                0, n_full, lambda kb, c: step(k_ref, vT_ref, pl.multiple_of(kb * tk, tk), tk, c), carry)
        for kr, vr, start, size in tails:
            carry = step(kr, vr, start, size, carry)
        return carry

    qTs = [qT_ref[h, :, sub * tq:(sub + 1) * tq] for h, sub in cols]
    shifts = []
    for qT in qTs:
        qf = qT.astype(F32)
        qsq = jnp.sum(qf * qf, axis=0, keepdims=True)
        shifts.append(jnp.sqrt(qsq * kmax_ref[...]) * 1.01)
    safe = jnp.max(functools.reduce(jnp.maximum, shifts)) <= SCORE_BOUND_SAFE

    @pl.when(safe)
    def _():
        def step(kr, vr, start, size, l8s, nb=1):
            units = [(j, c) for j in range(nb) for c in range(len(cols))]
            kblk = lambda j: kr[0, pl.ds(start + j * size, size), :]
            vblk = lambda j: vr[0, :, pl.ds(start + j * size, size)]
            qcol = lambda c: qTs[c]
            l8s = list(l8s)
            s_next = jnp.dot(kblk(0), qcol(0), preferred_element_type=F32)
            for u, (j, c) in enumerate(units):
                s = s_next
                if u + 1 < len(units):
                    jn, cn = units[u + 1]
                    s_next = jnp.dot(kblk(jn), qcol(cn), preferred_element_type=F32)
                p = jnp.exp2(s - shifts[c])
                l8s[c] = l8s[c] + jnp.sum(p.reshape(size // 8, 8, tq), axis=0)
                acc_ref[c] += jnp.dot(vblk(j), p.astype(BF16), preferred_element_type=F32)
            return tuple(l8s)

        acc_ref[...] = jnp.zeros_like(acc_ref)
        l8s = tuple(jnp.zeros((8, tq), F32) for _ in cols)

        def body(kb, c):
            cast_rows(kb)
            return step(k_ref, vT_ref, pl.multiple_of(kb * (tk * unroll), tk * unroll), tk, c, nb=unroll)

        if n_body > 0:
            l8s = lax.fori_loop(0, n_body, body, l8s)
        for kb in range(n_body * unroll, n_full):
            l8s = step(k_ref, vT_ref, kb * tk, tk, l8s)
        for kr, vr, start, size in tails:
            l8s = step(kr, vr, start, size, l8s)
        for c in range(len(cols)):
            l = jnp.sum(l8s[c], axis=0, keepdims=True)
            acc_ref[c] = acc_ref[c] * (1.0 / l)

    @pl.when(jnp.logical_not(safe))
    def _():
        for trip in range(n_body if in_loop else 0):
            cast_rows(trip)
        for c in range(len(cols)):
            def step(kr, vr, start, size, carry, qT=qTs[c]):
                m, l, acc = carry
                kblk = kr[0, pl.ds(start, size), :]
                s = jnp.dot(kblk, qT, preferred_element_type=F32)
                m_new = jnp.maximum(m, jnp.max(s, axis=0, keepdims=True))
                alpha = jnp.exp2(m - m_new)
                p = jnp.exp2(s - m_new)
                l = alpha * l + jnp.sum(p, axis=0, keepdims=True)
                vblk = vr[0, :, pl.ds(start, size)]
                acc = alpha * acc + jnp.dot(vblk, p.astype(BF16), preferred_element_type=F32)
                return m_new, l, acc

            carry = (jnp.full((1, tq), NEG_BIG, F32), jnp.zeros((1, tq), F32),
                     jnp.zeros((HEAD_DIM, tq), F32))
            _, l, acc = run_blocks(step, carry)
            acc_ref[c] = acc * (1.0 / l)

    for c, (h, sub) in enumerate(cols):
        o_ref[sub * tq:(sub + 1) * tq, h * HEAD_DIM:(h + 1) * HEAD_DIM] = acc_ref[c].T.astype(o_ref.dtype)


def flash_attention(qT, k, vT, tq, extra=None, casts=()):
    hq, _, mq = qT.shape
    hkv = k.shape[0]
    group = hq // hkv
    nq = mq // tq
    kv_specs = lambda kk, vv: [
        pl.BlockSpec((1, kk.shape[1], HEAD_DIM), lambda g, i: (g, 0, 0), pipeline_mode=pl.Buffered(1)),
        pl.BlockSpec((1, HEAD_DIM, vv.shape[2]), lambda g, i: (g, 0, 0), pipeline_mode=pl.Buffered(1))]
    args, specs = [k, vT], kv_specs(k, vT)
    if extra is not None:
        args += list(extra)
        specs += kv_specs(*extra)
    out_shapes = [jax.ShapeDtypeStruct((mq, hq * HEAD_DIM), BF16)]
    out_specs = [pl.BlockSpec((tq, group * HEAD_DIM), lambda g, i: (i, g))]
    for w, layer in casts:
        _, kk, nn = w.shape
        rb = kk // (hkv * nq)
        args.append(w)
        specs.append(pl.BlockSpec((None, rb, nn), functools.partial(
            lambda g, i, layer: (layer, g * nq + i, 0), layer=layer)))
        out_shapes.append(jax.ShapeDtypeStruct((1, kk, nn), BF16))
        out_specs.append(pl.BlockSpec((None, rb, nn), lambda g, i: (0, g * nq + i, 0)))
    outs = pl.pallas_call(
        functools.partial(_flash_kernel, group=group, tk=512, has_extra=extra is not None, n_cast=len(casts)),
        out_shape=out_shapes,
        grid=(hkv, nq),
        in_specs=[pl.BlockSpec((group, HEAD_DIM, tq), lambda g, i: (g, 0, i))] + specs,
        out_specs=out_specs,
        scratch_shapes=[pltpu.VMEM((1, 1), F32),
                        pltpu.VMEM((group * max(1, tq // FLASH_TQ), HEAD_DIM, min(tq, FLASH_TQ)), F32)],
        compiler_params=_cp("parallel", "arbitrary"),
        name="flash_attention",
    )(qT, *args)
    return outs[0], list(outs[1:])


NA_RB = 8
NA_TOK = NA_RB * GRID_W
NA_PAIR = 2 * GRID_W
NA_SLAB_ROWS = NA_ROWS + 2
NA_SLAB = NA_SLAB_ROWS * GRID_W
NA_PATTERNS = 5


def _na_kernel(q_ref, kp, kc, kn, vp, vc, vn, ck_ref, cv_ref, bias_ref, o_ref, kbuf, vbuf):
    b = pl.program_id(0)
    rows = pl.num_programs(0) * NA_RB
    kbuf[0:NA_TOK, :] = kp[...]
    kbuf[NA_TOK:2 * NA_TOK, :] = kc[...]
    kbuf[2 * NA_TOK:3 * NA_TOK, :] = kn[...]
    vbuf[0:NA_TOK, :] = vp[...]
    vbuf[NA_TOK:2 * NA_TOK, :] = vc[...]
    vbuf[2 * NA_TOK:3 * NA_TOK, :] = vn[...]
    nt = (((1,), (1,)), ((), ()))

    def slab(pair):
        r = b * NA_RB + 2 * pair
        u0 = jnp.clip(r - NA_ROWS // 2, 0, rows - NA_ROWS)
        off = pl.multiple_of((u0 - b * NA_RB + NA_RB) * GRID_W, NA_PAIR)
        pat = jnp.where(r < NA_ROWS // 2, r // 2,
                        jnp.where(r >= rows - NA_ROWS // 2, 3 + (r - (rows - NA_ROWS // 2)) // 2, 2))
        return off, pat

    slabs = [slab(pair) for pair in range(NA_RB // 2)]
    units = [(pair, h) for pair in range(NA_RB // 2) for h in range(NA_HEADS)]

    def scores(pair, h):
        off, pat = slabs[pair]
        sl = slice(h * HEAD_DIM, (h + 1) * HEAD_DIM)
        qh = q_ref[pair * NA_PAIR:(pair + 1) * NA_PAIR, sl]
        s = lax.dot_general(qh, kbuf[pl.ds(off, NA_SLAB), sl], nt, preferred_element_type=F32)
        sc = lax.dot_general(qh, ck_ref[:, sl], nt, preferred_element_type=F32)
        return s + bias_ref[pat, h], sc

    nxt = scores(*units[0])
    for u, (pair, h) in enumerate(units):
        s, sc = nxt
        if u + 1 < len(units):
            nxt = scores(*units[u + 1])
        off, _ = slabs[pair]
        sl = slice(h * HEAD_DIM, (h + 1) * HEAD_DIM)
        m = jnp.maximum(jnp.max(s, axis=-1, keepdims=True), jnp.max(sc, axis=-1, keepdims=True))
        p = jnp.exp2(s - m)
        pc = jnp.exp2(sc - m)
        l = jnp.sum(p, axis=-1, keepdims=True) + jnp.sum(pc, axis=-1, keepdims=True)
        o = (jnp.dot(p.astype(BF16), vbuf[pl.ds(off, NA_SLAB), sl], preferred_element_type=F32)
             + jnp.dot(pc.astype(BF16), cv_ref[:, sl], preferred_element_type=F32))
        o_ref[pair * NA_PAIR:(pair + 1) * NA_PAIR, sl] = (o * (1.0 / l)).astype(o_ref.dtype)


def neighborhood_attention(qn, kn, zg, ckn, czg, bias):
    m = qn.shape[0]
    nb = m // NA_TOK
    vcol = _C_NV // NA_W
    blk = (NA_TOK, NA_W)
    return pl.pallas_call(
        _na_kernel,
        out_shape=jax.ShapeDtypeStruct((m, NA_W), BF16),
        grid=(nb,),
        in_specs=[pl.BlockSpec(blk, lambda b: (b, 0)),
                  pl.BlockSpec(blk, lambda b: (jnp.maximum(b - 1, 0), 0)),
                  pl.BlockSpec(blk, lambda b: (b, 0)),
                  pl.BlockSpec(blk, lambda b: (jnp.minimum(b + 1, nb - 1), 0)),
                  pl.BlockSpec(blk, lambda b: (jnp.maximum(b - 1, 0), vcol)),
                  pl.BlockSpec(blk, lambda b: (b, vcol)),
                  pl.BlockSpec(blk, lambda b: (jnp.minimum(b + 1, nb - 1), vcol)),
                  pl.BlockSpec((CTX_LEN, NA_W), lambda b: (0, 0)),
                  pl.BlockSpec((CTX_LEN, NA_W), lambda b: (0, vcol)),
                  pl.BlockSpec((NA_PATTERNS, NA_HEADS, NA_PAIR, NA_SLAB), lambda b: (0, 0, 0, 0))],
        out_specs=pl.BlockSpec(blk, lambda b: (b, 0)),
        scratch_shapes=[pltpu.VMEM((3 * NA_TOK, NA_W), BF16), pltpu.VMEM((3 * NA_TOK, NA_W), BF16)],
        compiler_params=_cp("arbitrary"),
        name="neighborhood_attention",
    )(qn, kn, kn, kn, zg, zg, zg, ckn, czg, bias)


def na_bias_table(rpb):
    col = np.arange(GRID_W)
    c0 = np.clip(col - NA_COLS // 2, 0, GRID_W - NA_COLS)
    kc = np.arange(GRID_W)
    inside = (kc[None, :] >= c0[:, None]) & (kc[None, :] < c0[:, None] + NA_COLS)
    dc = kc[None, :] - col[:, None] + NA_COLS - 1
    onehot = (dc[None] == np.arange(2 * NA_COLS - 1)[:, None, None]) & inside[None]
    t = jnp.einsum('hrd,dck->hrck', rpb.astype(F32) * LOG2E, jnp.asarray(onehot, F32),
                   precision=lax.Precision.HIGHEST)
    t = jnp.where(inside[None, None], t, NEG_BIG)
    masked = jnp.full((NA_HEADS, GRID_W, GRID_W), NEG_BIG, F32)
    patterns = [(0, 0), (2, 0), (4, 1), (4, 0), (6, 0)]
    tabs = []
    for a, e in patterns:
        per_q = []
        for start, dr0 in ((0, NA_ROWS - 1 - a), (e, NA_ROWS - 2 - a)):
            blocks = [t[:, w + dr0] if start <= w < start + NA_ROWS else masked for w in range(NA_SLAB_ROWS)]
            per_q.append(jnp.stack(blocks, axis=2).reshape(NA_HEADS, GRID_W, NA_SLAB))
        tabs.append(jnp.concatenate(per_q, axis=1))
    return jnp.stack(tabs, axis=0)


def _dft_mats(n):
    a = 2.0 * np.pi * np.outer(np.arange(n), np.arange(n)) / n
    return np.cos(a), np.sin(a)


def _chan_dft_kernel(z_ref, w_ref, o_ref):
    for g in range(FOURIER_GROUPS):
        x = z_ref[:, g * HEAD_DIM:(g + 1) * HEAD_DIM]
        r = jnp.dot(x, w_ref[...], preferred_element_type=F32)
        o_ref[0, :, g * HEAD_DIM:(g + 1) * HEAD_DIM] = r[:, :HEAD_DIM].astype(o_ref.dtype)
        o_ref[1, :, g * HEAD_DIM:(g + 1) * HEAD_DIM] = r[:, HEAD_DIM:].astype(o_ref.dtype)


def chan_dft(zg, wch):
    m = zg.shape[0]
    tm = min(m, 1024)
    return pl.pallas_call(
        _chan_dft_kernel,
        out_shape=jax.ShapeDtypeStruct((2, m, FOURIER_W), BF16),
        grid=(m // tm,),
        in_specs=[pl.BlockSpec((tm, FOURIER_W), lambda i: (i, _C_ZF // FOURIER_W)),
                  pl.BlockSpec((HEAD_DIM, 2 * HEAD_DIM), lambda i: (0, 0))],
        out_specs=pl.BlockSpec((2, tm, FOURIER_W), lambda i: (0, i, 0)),
        compiler_params=_cp("parallel"),
        name="chan_dft",
    )(zg, wch)


def _left_mm_kernel(l_ref, x_ref, o_ref, *, n_out):
    x = jnp.concatenate([x_ref[0], x_ref[1]], axis=0)
    r = jnp.dot(l_ref[...], x, preferred_element_type=F32)
    if n_out == 1:
        o_ref[...] = r.astype(o_ref.dtype)
    else:
        half = r.shape[0] // 2
        o_ref[0] = r[:half].astype(o_ref.dtype)
        o_ref[1] = r[half:].astype(o_ref.dtype)


def dft_stage1(zc, lmat):
    _, r, n = zc.shape
    tn = 4096
    return pl.pallas_call(
        functools.partial(_left_mm_kernel, n_out=2),
        out_shape=jax.ShapeDtypeStruct((2, r, n), BF16),
        grid=(n // tn,),
        in_specs=[pl.BlockSpec((2 * r, 2 * r), lambda j: (0, 0)),
                  pl.BlockSpec((2, r, tn), lambda j: (0, 0, j))],
        out_specs=pl.BlockSpec((2, r, tn), lambda j: (0, 0, j)),
        compiler_params=_cp("parallel"),
        name="dft_stage1",
    )(lmat, zc)


def _dft_stage3_kernel(l_ref, x_ref, o_ref, *, kb):
    for i in range(kb):
        x = jnp.concatenate([x_ref[0, i], x_ref[1, i]], axis=0)
        o_ref[i] = jnp.dot(l_ref[i], x, preferred_element_type=F32).astype(o_ref.dtype)


def dft_stage3(a, tables):
    _, n1, n2, c = a.shape
    kb = 8
    return pl.pallas_call(
        functools.partial(_dft_stage3_kernel, kb=kb),
        out_shape=jax.ShapeDtypeStruct((n1, n2, c), BF16),
        grid=(n1 // kb,),
        in_specs=[pl.BlockSpec((kb, n2, 2 * n2), lambda j: (j, 0, 0)),
                  pl.BlockSpec((2, kb, n2, c), lambda j: (0, j, 0, 0))],
        out_specs=pl.BlockSpec((kb, n2, c), lambda j: (j, 0, 0)),
        compiler_params=_cp("parallel"),
        name="dft_stage3",
    )(tables, a)


def dft_direct(zc, lmat):
    _, t, c = zc.shape
    return pl.pallas_call(
        functools.partial(_left_mm_kernel, n_out=1),
        out_shape=jax.ShapeDtypeStruct((t, c), BF16),
        grid=(1,),
        in_specs=[pl.BlockSpec((t, 2 * t), lambda j: (0, 0)),
                  pl.BlockSpec((2, t, c), lambda j: (0, 0, 0))],
        out_specs=pl.BlockSpec((t, c), lambda j: (0, 0)),
        compiler_params=_cp("arbitrary"),
        name="dft_direct",
    )(lmat, zc)


def _fourier_tables():
    c128, s128 = _dft_mats(HEAD_DIM)
    wch = np.concatenate([c128, -s128], axis=1)
    n1 = SEQ // HEAD_DIM
    l1 = np.block([[c128, s128], [-s128, c128]])
    k1 = np.arange(n1)[:, None, None]
    k2 = np.arange(n1)[None, :, None]
    t2 = np.arange(n1)[None, None, :]
    ang = 2.0 * np.pi * ((n1 * k2 + k1) * t2 % SEQ) / SEQ
    norm = 1.0 / math.sqrt(SEQ * HEAD_DIM)
    l3 = np.concatenate([np.cos(ang), np.sin(ang)], axis=2) * norm
    cc, sc = _dft_mats(CTX_LEN)
    lc = np.concatenate([cc, sc], axis=1) / math.sqrt(CTX_LEN * HEAD_DIM)
    as_bf = lambda a: jnp.asarray(a, F32).astype(BF16)
    return as_bf(wch), as_bf(l1), as_bf(l3), as_bf(lc)


def fourier_latent(zg, tabs):
    wch, l1, l3, _ = tabs
    n1 = SEQ // HEAD_DIM
    zc = chan_dft(zg, wch)
    a = dft_stage1(zc.reshape(2, n1, n1 * FOURIER_W), l1)
    o3 = dft_stage3(a.reshape(2, n1, n1, FOURIER_W), l3)
    return o3.transpose(1, 0, 2).reshape(SEQ, FOURIER_W)


def fourier_ctx(czg, tabs):
    wch, _, _, lc = tabs
    return dft_direct(chan_dft(czg, wch), lc)


def _rope_tables(n_tok):
    t = jnp.arange(n_tok)
    row = (t // GRID_W).astype(F32)
    col = (t % GRID_W).astype(F32)
    n_freq = HEAD_DIM // 4
    inv_freq = ROPE_THETA ** (-jnp.arange(n_freq, dtype=F32) / n_freq)
    ang = jnp.concatenate([row[:, None] * inv_freq, col[:, None] * inv_freq], axis=-1)
    cos, sin = jnp.cos(ang), jnp.sin(ang)
    return jnp.concatenate([cos, cos], axis=-1), jnp.concatenate([-sin, sin], axis=-1)


def kernel(x, c, ctx, c_ctx, w_ada, b_ada, norm1, w_in, conv_w, na_q_gain, na_k_gain, na_rpb, gqa_q_gain, gqa_k_gain, w_conv_out, w_fourier_out, w_na_out, w_gqa_out, w_gate, b_gate, w_o, norm2, w_up, ffn_conv_w, w_down):
    d = D_MODEL
    h = x[0]
    hc = ctx[0]
    rope = _rope_tables(SEQ)
    ftabs = _fourier_tables()
    qscale2 = HEAD_DIM ** -0.5 * LOG2E
    cvec = jnp.zeros((8, d), F32).at[0].set(c[0]).at[1].set(c_ctx)
    row = lambda v: v.reshape(1, -1)

    w_in_b, w_gate_b = w_in[0:1].astype(BF16), w_gate[0:1].astype(BF16)
    late_w = (w_conv_out, w_fourier_out, w_na_out, w_gqa_out, w_o, w_up, w_down)
    b_ada3 = b_ada.reshape(DEPTH, 1, -1)
    b_gate3 = b_gate.reshape(DEPTH, 1, -1)

    for i in range(DEPTH):
        last = i == DEPTH - 1
        bias_tab = na_bias_table(na_rpb[i])
        gains = (row(na_q_gain[i]), row(na_k_gain[i]), row(gqa_q_gain[i]), row(gqa_k_gain[i]))

        mods = ada_mod(cvec, w_ada, b_ada3, i)
        lat = [mods[0:1, k * d:(k + 1) * d] for k in range(6)]
        cm = [mods[1:2, k * d:(k + 1) * d] for k in range(6)]

        zg = in_proj(h, row(norm1[i]), lat[0], lat[1], w_in_b, w_gate_b, b_gate3, i, "in_proj")
        czg = in_proj(hc, row(norm1[i]), cm[0], cm[1], w_in_b, w_gate_b, b_gate3, i, "in_proj_ctx")

        cnk_rows, cnqT, cnk, cnvT, cqT, ck, cvT = prep_ctx(czg, gains, qscale2)
        nq, nk, gqT, gk, gvT = prep_latent(zg, gains, rope, qscale2)

        y_conv = conv_mixer(zg, conv_w, i)
        y_four = fourier_latent(zg, ftabs)
        y_na = neighborhood_attention(nq, nk, zg, cnk_rows, czg, bias_tab)
        casts = [(w, i) for w in late_w] + ([] if last else [(w_in, i + 1), (w_gate, i + 1)])
        y_gqa, wb = flash_attention(gqT, gk, gvT, tq=1024, extra=(ck, cvT), casts=casts)
        merge_w, w_o_b, w_up_b, w_down_b = wb[0:4], wb[4], wb[5], wb[6]

        merged = merge_branches([y_conv, y_four, y_na, y_gqa], merge_w, zg)
        h = mm_residual(merged, w_o_b, h, lat[2], "out_proj", in_place=i > 0)
        act = ffn_up_act(h, row(norm2[i]), lat[3], lat[4], w_up_b, ffn_conv_w, i, "ffn_up_act")
        h = mm_residual(act, w_down_b, h, lat[5], "ffn_down", in_place=True)

        if not last:
            cy_conv = conv_mixer(czg, conv_w, i)
            cy_four = fourier_ctx(czg, ftabs)
            cy_na, _ = flash_attention(cnqT, cnk, cnvT, tq=CTX_LEN)
            cy_gqa, _ = flash_attention(cqT, ck, cvT, tq=CTX_LEN)
            cmerged = merge_branches([cy_conv, cy_four, cy_na, cy_gqa], merge_w, czg)
            hc = mm_residual(cmerged, w_o_b, hc, cm[2], "out_proj_ctx", in_place=i > 0)
            cact = ffn_up_act(hc, row(norm2[i]), cm[3], cm[4], w_up_b, ffn_conv_w, i, "ffn_up_act_ctx")
            hc = mm_residual(cact, w_down_b, hc, cm[5], "ffn_down_ctx", in_place=True)
            w_in_b, w_gate_b = wb[7], wb[8]

    return h[None]
```
